```python
import math, functools
import jax, jax.numpy as jnp
from jax import lax
import numpy as np

D_MODEL = 1024
BATCH = 8
SEQ = 2048
DEPTH = 1
DEC_BATCH = 32
DEC_SEQ = 1
PAST_LEN = 16384
PAGE_SIZE = 128

HEAD_DIM = 128
MIX_WIDTH = D_MODEL
H_A = MIX_WIDTH // (2 * HEAD_DIM)
H_B = MIX_WIDTH // HEAD_DIM - H_A
GDN_QKV = 3 * H_A * HEAD_DIM
MOBA_QKV = 3 * H_B * HEAD_DIM
SPLITS = [GDN_QKV, GDN_QKV + H_A * HEAD_DIM, GDN_QKV + H_A * HEAD_DIM + H_A, GDN_QKV + H_A * HEAD_DIM + 2 * H_A]
IN_COLS = GDN_QKV + H_A * HEAD_DIM + 2 * H_A + MOBA_QKV
CONV_W = 4
CHUNK = 64
MOBA_BLOCK = 256
MOBA_TOPK = 3
QBLOCK = 32
ROT_DIM = HEAD_DIM // 4
ROPE_THETA = 500000.0
D_FF = -(-8 * D_MODEL // (3 * 256)) * 256
EPS = 1e-6

kernel_name = 'hymba_gdn_moba_step'


def rmsnorm(x, w):
    xf = x.astype(jnp.float32)
    y = xf * lax.rsqrt(jnp.mean(xf * xf, axis=-1, keepdims=True) + EPS)
    return (y * w.astype(jnp.float32)).astype(x.dtype)


def l2norm(x):
    xf = x.astype(jnp.float32)
    return xf * lax.rsqrt(jnp.sum(xf * xf, axis=-1, keepdims=True) + EPS)


def rotary(x, pos):
    half = ROT_DIM // 2
    inv_freq = ROPE_THETA ** (-jnp.arange(half, dtype=jnp.float32) / half)
    ang = pos.astype(jnp.float32)[:, None] * inv_freq[None, :]
    cos = jnp.cos(ang)[None, :, None, :]
    sin = jnp.sin(ang)[None, :, None, :]
    xr = x[..., :ROT_DIM].astype(jnp.float32)
    x1, x2 = xr[..., :half], xr[..., half:]
    rot = jnp.concatenate([x1 * cos - x2 * sin, x2 * cos + x1 * sin], axis=-1)
    return jnp.concatenate([rot.astype(x.dtype), x[..., ROT_DIM:]], axis=-1)


def short_conv(xc, prev, w):
    L = xc.shape[1]
    full = jnp.concatenate([prev.astype(xc.dtype), xc], axis=1)
    y = full[:, 0:L] * w[0]
    for j in range(1, CONV_W):
        y = y + full[:, j:j + L] * w[j]
    return jax.nn.silu(y), full[:, L:]


def gated_delta_rule(q, k, v, g, beta, s0):
    B, L, H, _ = q.shape
    pad = (-L) % CHUNK
    n = (L + pad) // CHUNK

    def chunks(t):
        t = jnp.pad(t.astype(jnp.float32), [(0, 0), (0, pad)] + [(0, 0)] * (t.ndim - 2))
        t = t.reshape((B, n, CHUNK) + t.shape[2:])
        return jnp.swapaxes(jnp.swapaxes(t, 2, 3), 0, 1)

    qc, kc, vc, gc, bc = chunks(q), chunks(k), chunks(v), chunks(g), chunks(beta)
    gcum = jnp.cumsum(gc, axis=-1)
    tri = jnp.tril(jnp.ones((CHUNK, CHUNK), bool))
    strict = jnp.tril(jnp.ones((CHUNK, CHUNK), bool), -1)
    decay = jnp.exp(jnp.where(tri, gcum[..., :, None] - gcum[..., None, :], -jnp.inf))
    kb = kc * bc[..., None]
    a = jnp.where(strict, jnp.einsum('nbhid,nbhjd->nbhij', kb, kc) * decay, 0.0)
    eye = jnp.eye(CHUNK, dtype=jnp.float32)
    tinv = lax.linalg.triangular_solve(eye + a, jnp.broadcast_to(eye, a.shape), left_side=True,
                                       lower=True, unit_diagonal=True)
    u = tinv @ (vc * bc[..., None])
    w = tinv @ (kb * jnp.exp(gcum)[..., None])
    qk = jnp.einsum('nbhid,nbhjd->nbhij', qc, kc) * decay
    q_dec = qc * jnp.exp(gcum)[..., None]
    g_last = gcum[..., -1]
    k_dec = kc * jnp.exp(g_last[..., None] - gcum)[..., None]

    def step(S, xs):
        u_i, w_i, qk_i, qd_i, kd_i, gl_i = xs
        v_new = u_i - w_i @ S
        o = qd_i @ S + qk_i @ v_new
        S = S * jnp.exp(gl_i)[..., None, None] + jnp.swapaxes(kd_i, -1, -2) @ v_new
        return S, o

    S, o = lax.scan(step, s0.astype(jnp.float32), (u, w, qk, q_dec, k_dec, g_last))
    o = jnp.swapaxes(jnp.swapaxes(o, 0, 1), 2, 3).reshape(B, n * CHUNK, H, -1)[:, :L]
    return o, S


def moba_attend(q, q_pos, k_means, fetch):
    nb = k_means.shape[2]
    n_past = q_pos // MOBA_BLOCK
    qf = q.astype(jnp.float32)
    gate = jnp.einsum('bhqd,bhnd->bhqn', qf, k_means.astype(jnp.float32))
    is_past = jnp.arange(nb)[None, :] < n_past[:, None]
    gate = jnp.where(is_past, gate, -jnp.inf)
    if nb < MOBA_TOPK:
        gate = jnp.pad(gate, ((0, 0), (0, 0), (0, 0), (0, MOBA_TOPK - nb)), constant_values=-jnp.inf)
    _, idx = lax.top_k(gate, MOBA_TOPK)
    idx = jnp.minimum(idx, nb - 1)
    slot_valid = jnp.arange(MOBA_TOPK)[None, :] < n_past[:, None]
    own = jnp.broadcast_to(n_past[:, None], idx.shape[:-1] + (1,))
    blocks = jnp.concatenate([idx, own.astype(idx.dtype)], axis=-1)
    pos = blocks[..., None] * MOBA_BLOCK + jnp.arange(MOBA_BLOCK)
    slots = jnp.concatenate([slot_valid, jnp.ones((q_pos.shape[0], 1), bool)], axis=-1)
    mask = slots[:, :, None] & (pos <= q_pos[:, None, None])
    kg, vg = fetch(pos)
    logits = jnp.einsum('bhqd,bhqnkd->bhqnk', qf, kg.astype(jnp.float32)) * (HEAD_DIM ** -0.5)
    logits = jnp.where(mask, logits, -jnp.inf)
    sh = logits.shape
    p = jax.nn.softmax(logits.reshape(sh[:3] + (-1,)), axis=-1).reshape(sh)
    out = jnp.einsum('bhqnk,bhqnkd->bhqd', p, vg.astype(jnp.float32))
    return out.astype(q.dtype)


def moba_prompt(q, k, v):
    B, L, H, D = q.shape
    nb = -(-L // MOBA_BLOCK)
    pad = nb * MOBA_BLOCK - L
    k_log = jnp.pad(k, ((0, 0), (0, pad), (0, 0), (0, 0)))
    v_log = jnp.pad(v, ((0, 0), (0, pad), (0, 0), (0, 0)))
    k_means = jnp.swapaxes(k_log.astype(jnp.float32).reshape(B, nb, MOBA_BLOCK, H, D).mean(axis=2), 1, 2)
    bi = jnp.arange(B)[:, None, None, None, None]
    hi = jnp.arange(H)[None, :, None, None, None]

    def fetch(pos):
        return k_log[bi, pos, hi], v_log[bi, pos, hi]

    nqb = L // QBLOCK
    qblk = q.reshape(B, nqb, QBLOCK, H, D).transpose(1, 0, 3, 2, 4)
    pblk = jnp.arange(L, dtype=jnp.int32).reshape(nqb, QBLOCK)
    out = lax.map(lambda qp: moba_attend(qp[0], qp[1], k_means, fetch), (qblk, pblk))
    return out.transpose(1, 0, 3, 2, 4).reshape(B, L, H * D)


def moba_sample(cache_k, cache_v, page_table, q, k, v):
    DB, LS, H, D = q.shape
    total = PAST_LEN + LS
    nb = -(-total // MOBA_BLOCK)
    ppb = MOBA_BLOCK // PAGE_SIZE
    n_pages = page_table.shape[1]
    page_sums = cache_k[page_table].astype(jnp.float32).sum(axis=2)
    page_sums = jnp.pad(page_sums, ((0, 0), (0, nb * ppb - n_pages), (0, 0), (0, 0)))
    blk = page_sums.reshape(DB, nb, ppb, H, D).sum(axis=2)
    new_pos = PAST_LEN + jnp.arange(LS, dtype=jnp.int32)
    blk = blk.at[:, new_pos // MOBA_BLOCK].add(k.astype(jnp.float32))
    k_means = jnp.swapaxes(blk / MOBA_BLOCK, 1, 2)
    bi = jnp.arange(DB)[:, None, None, None, None]
    hi = jnp.arange(H)[None, :, None, None, None]

    def fetch(pos):
        past_pos = jnp.minimum(pos, PAST_LEN - 1)
        phys = page_table[bi, past_pos // PAGE_SIZE]
        off = past_pos % PAGE_SIZE
        new_idx = jnp.clip(pos - PAST_LEN, 0, LS - 1)
        from_past = (pos < PAST_LEN)[..., None]
        kg = jnp.where(from_past, cache_k[phys, off, hi], k[bi, new_idx, hi])
        vg = jnp.where(from_past, cache_v[phys, off, hi], v[bi, new_idx, hi])
        return kg, vg

    out = moba_attend(jnp.swapaxes(q, 1, 2), new_pos, k_means, fetch)
    return jnp.swapaxes(out, 1, 2).reshape(DB, LS, H * D)


def layer_step(x, pos, gdn_s0, conv_s0, moba_fn, norm1_w, w_in, conv_w, a_log, dt_bias, gdn_norm_w,
               w_out, norm2_w, w_gate, w_up, w_down):
    B, L, _ = x.shape
    proj = rmsnorm(x, norm1_w) @ w_in
    qkv_a, z, a, b, qkv_b = jnp.split(proj, SPLITS, axis=-1)
    conv_out, conv_new = short_conv(qkv_a, conv_s0, conv_w)
    qa, ka, va = [t.reshape(B, L, H_A, HEAD_DIM) for t in jnp.split(conv_out, 3, axis=-1)]
    qa = l2norm(qa) * (HEAD_DIM ** -0.5)
    ka = l2norm(ka)
    g = -jnp.exp(a_log.astype(jnp.float32)) * jax.nn.softplus(a.astype(jnp.float32) + dt_bias.astype(jnp.float32))
    beta = jax.nn.sigmoid(b.astype(jnp.float32))
    oa, s_new = gated_delta_rule(qa, ka, va, g, beta, gdn_s0)
    oa = rmsnorm(oa, gdn_norm_w) * jax.nn.silu(z.reshape(B, L, H_A, HEAD_DIM).astype(jnp.float32))
    oa = oa.reshape(B, L, H_A * HEAD_DIM).astype(x.dtype)
    qb, kb, vb = [t.reshape(B, L, H_B, HEAD_DIM) for t in jnp.split(qkv_b, 3, axis=-1)]
    qb = rotary(qb, pos)
    kb = rotary(kb, pos)
    ob = moba_fn(qb, kb, vb)
    x = x + jnp.concatenate([oa, ob], axis=-1) @ w_out
    h = rmsnorm(x, norm2_w)
    x = x + (jax.nn.silu(h @ w_gate) * (h @ w_up)) @ w_down
    return x, kb, vb, s_new, conv_new


def setup_inputs(seed: int = 0) -> dict:
    key = jax.random.key(seed)
    ks = jax.random.split(key, 20)
    n_pages = PAST_LEN // PAGE_SIZE
    n_used = DEC_BATCH * n_pages
    n_phys = n_used + -(-n_used // 4)
    nrm = jax.random.normal
    page_table = jax.random.permutation(ks[4], n_phys)[:n_used].reshape(DEC_BATCH, n_pages).astype(jnp.int32)
    dt = jnp.exp(jax.random.uniform(ks[9], (DEPTH, H_A)) * (math.log(0.1) - math.log(0.001)) + math.log(0.001))
    return {
        'x_prompt': nrm(ks[0], (BATCH, SEQ, D_MODEL), jnp.float32),
        'x_sample': nrm(ks[1], (DEC_BATCH, DEC_SEQ, D_MODEL), jnp.float32),
        'cache_k': nrm(ks[2], (DEPTH, n_phys, PAGE_SIZE, H_B, HEAD_DIM), jnp.float32),
        'cache_v': nrm(ks[3], (DEPTH, n_phys, PAGE_SIZE, H_B, HEAD_DIM), jnp.float32),
        'page_table': page_table,
        'state_gdn': nrm(ks[5], (DEPTH, DEC_BATCH, H_A, HEAD_DIM, HEAD_DIM), jnp.float32) * HEAD_DIM ** -0.5,
        'state_conv': nrm(ks[6], (DEPTH, DEC_BATCH, CONV_W - 1, GDN_QKV), jnp.float32),
        'norm1_w': 1.0 + 0.1 * nrm(ks[7], (DEPTH, D_MODEL), jnp.float32),
        'w_in': nrm(ks[8], (DEPTH, D_MODEL, IN_COLS), jnp.float32) * D_MODEL ** -0.5,
        'conv_w': nrm(ks[10], (DEPTH, CONV_W, GDN_QKV), jnp.float32) * CONV_W ** -0.5,
        'a_log': jnp.log(jax.random.uniform(ks[11], (DEPTH, H_A), minval=1.0, maxval=16.0)),
        'dt_bias': dt + jnp.log(-jnp.expm1(-dt)),
        'gdn_norm_w': 1.0 + 0.1 * nrm(ks[12], (DEPTH, HEAD_DIM), jnp.float32),
        'w_out': nrm(ks[13], (DEPTH, MIX_WIDTH, D_MODEL), jnp.float32) * MIX_WIDTH ** -0.5,
        'norm2_w': 1.0 + 0.1 * nrm(ks[14], (DEPTH, D_MODEL), jnp.float32),
        'w_gate': nrm(ks[15], (DEPTH, D_MODEL, D_FF), jnp.float32) * D_MODEL ** -0.5,
        'w_up': nrm(ks[16], (DEPTH, D_MODEL, D_FF), jnp.float32) * D_MODEL ** -0.5,
        'w_down': nrm(ks[17], (DEPTH, D_FF, D_MODEL), jnp.float32) * D_FF ** -0.5,
        'final_norm_w': 1.0 + 0.1 * nrm(ks[18], (D_MODEL,), jnp.float32),
    }


def reference(x_prompt, x_sample, cache_k, cache_v, page_table, state_gdn, state_conv, norm1_w, w_in,
              conv_w, a_log, dt_bias, gdn_norm_w, w_out, norm2_w, w_gate, w_up, w_down, final_norm_w):
    pos_p = jnp.arange(x_prompt.shape[1], dtype=jnp.int32)
    pos_s = PAST_LEN + jnp.arange(x_sample.shape[1], dtype=jnp.int32)
    xp, xs = x_prompt, x_sample
    kp, vp, ks, vs, gp, gs, cp, cs = [], [], [], [], [], [], [], []
    for layer in range(DEPTH):
        weights = (norm1_w[layer], w_in[layer], conv_w[layer], a_log[layer], dt_bias[layer], gdn_norm_w[layer],
                   w_out[layer], norm2_w[layer], w_gate[layer], w_up[layer], w_down[layer])
        s0 = jnp.zeros((xp.shape[0], H_A, HEAD_DIM, HEAD_DIM), jnp.float32)
        c0 = jnp.zeros((xp.shape[0], CONV_W - 1, GDN_QKV), xp.dtype)
        xp, k1, v1, s1, c1 = layer_step(xp, pos_p, s0, c0, moba_prompt, *weights)
        sample_mixer = functools.partial(moba_sample, cache_k[layer], cache_v[layer], page_table)
        xs, k2, v2, s2, c2 = layer_step(xs, pos_s, state_gdn[layer], state_conv[layer], sample_mixer, *weights)
        kp.append(k1); vp.append(v1); gp.append(s1); cp.append(c1)
        ks.append(k2); vs.append(v2); gs.append(s2); cs.append(c2)
    y_prompt = rmsnorm(xp, final_norm_w)
    y_sample = rmsnorm(xs, final_norm_w)
    return (y_prompt, y_sample, jnp.stack(kp), jnp.stack(vp), jnp.stack(ks), jnp.stack(vs),
            jnp.stack(gp), jnp.stack(gs), jnp.stack(cp), jnp.stack(cs))
```

```python
import functools

import jax
import jax.numpy as jnp
from jax import lax
from jax.experimental import pallas as pl
from jax.experimental.pallas import tpu as pltpu

D_MODEL = 1024
HEAD_DIM = 128
H_A = 4
H_B = 4
GDN_QKV = 3 * H_A * HEAD_DIM
MOBA_QKV = 3 * H_B * HEAD_DIM
Z_COLS = H_A * HEAD_DIM
CONV_W = 4
CHUNK = 64
MOBA_BLOCK = 256
MOBA_TOPK = 3
PAGE_SIZE = 128
PAGES_PER_BLOCK = MOBA_BLOCK // PAGE_SIZE
ROT_DIM = HEAD_DIM // 4
ROPE_THETA = 500000.0
EPS = 1e-6

LANES = 128
SUBLANES = 8
AB_PAD = LANES
PROJ_COLS = GDN_QKV + Z_COLS + MOBA_QKV + AB_PAD
VMEM_LIMIT = 56 * 1024 * 1024

F32 = jnp.float32
BF16 = jnp.bfloat16
HIGHEST = lax.Precision.HIGHEST
NEG_INF = float("-inf")


def _dot(a, b, precision=None):
    return jnp.dot(a, b, precision=precision, preferred_element_type=F32)


def _dot_nt(a, b, precision=None):
    return lax.dot_general(a, b, (((1,), (1,)), ((), ())), precision=precision,
                           preferred_element_type=F32)


def _rmsnorm(x, w):
    return x * lax.rsqrt(jnp.mean(x * x, axis=-1, keepdims=True) + EPS) * w


def _l2norm(x):
    return x * lax.rsqrt(jnp.sum(x * x, axis=-1, keepdims=True) + EPS)


def _silu(x):
    return x * jax.nn.sigmoid(x)


def _const_spec(shape):
    return pl.BlockSpec(shape, lambda *_: (0,) * len(shape), pipeline_mode=pl.Buffered(1))


def _params(semantics):
    return pltpu.CompilerParams(dimension_semantics=semantics, vmem_limit_bytes=VMEM_LIMIT)


def _in_proj_kernel(x_ref, nw_ref, w_ref, gpar_ref, cos_ref, sa_ref, sb_ref,
                    qkva_ref, z_ref, gb_ref, qb_ref, kb_ref, vb_ref):
    xn = _rmsnorm(x_ref[...], nw_ref[...])
    proj = _dot(xn.astype(BF16), w_ref[...])
    qkva_ref[...] = proj[:, :GDN_QKV]
    z_ref[...] = proj[:, GDN_QKV:GDN_QKV + Z_COLS]
    b0 = GDN_QKV + Z_COLS
    ab = proj[:, b0 + MOBA_QKV:]
    lane = lax.broadcasted_iota(jnp.int32, ab.shape, 1)
    g = -jnp.exp(gpar_ref[0:1, :]) * jax.nn.softplus(ab + gpar_ref[1:2, :])
    gb_ref[...] = jnp.where(lane < H_A, g, jax.nn.sigmoid(ab))
    cos, sa, sb = cos_ref[...], sa_ref[...], sb_ref[...]
    half = ROT_DIM // 2

    def rot(t):
        return (t * cos + pltpu.roll(t, HEAD_DIM - half, 1) * sa + pltpu.roll(t, half, 1) * sb)

    for h in range(H_B):
        lo, hi = h * HEAD_DIM, (h + 1) * HEAD_DIM
        qb_ref[:, lo:hi] = rot(proj[:, b0 + lo:b0 + hi])
        kb_ref[:, lo:hi] = rot(proj[:, b0 + H_B * HEAD_DIM + lo:b0 + H_B * HEAD_DIM + hi])
    vb_ref[...] = proj[:, b0 + 2 * H_B * HEAD_DIM:b0 + MOBA_QKV]


def _in_proj(x, nw, w_all, gpar, cos, sa, sb, tm):
    t = x.shape[0]
    n_tab = cos.shape[0] // tm
    row = lambda i: (i, 0)
    tab = lambda i: (i % n_tab, 0)
    widths = (GDN_QKV, Z_COLS, AB_PAD, H_B * HEAD_DIM, H_B * HEAD_DIM, H_B * HEAD_DIM)
    return pl.pallas_call(
        _in_proj_kernel,
        grid=(t // tm,),
        in_specs=[pl.BlockSpec((tm, D_MODEL), row), _const_spec((1, D_MODEL)),
                  _const_spec((D_MODEL, PROJ_COLS)), _const_spec((SUBLANES, LANES)),
                  pl.BlockSpec((tm, LANES), tab), pl.BlockSpec((tm, LANES), tab),
                  pl.BlockSpec((tm, LANES), tab)],
        out_specs=[pl.BlockSpec((tm, w), row) for w in widths],
        out_shape=[jax.ShapeDtypeStruct((t, w), F32) for w in widths],
        compiler_params=_params(("parallel",)),
        name="in_proj",
    )(x, nw, w_all, gpar, cos, sa, sb)


def _gdn_head_norms(y):
    parts = []
    for h in range(H_A):
        parts.append(_l2norm(y[:, h * HEAD_DIM:(h + 1) * HEAD_DIM]) * (HEAD_DIM ** -0.5))
    for h in range(H_A, 2 * H_A):
        parts.append(_l2norm(y[:, h * HEAD_DIM:(h + 1) * HEAD_DIM]))
    return parts


def _gdn_out_norm(o, gnw, z):
    return _rmsnorm(o, gnw) * _silu(z)


def _gdn_prompt_kernel(qkva_ref, gb_ref, z_ref, cw_ref, gnw_ref, o_ref, sfin_ref,
                       xbuf, qkv_s, s_s, *, tl, n_t):
    t = pl.program_id(1)

    @pl.when(t == 0)
    def _():
        xbuf[0:SUBLANES, :] = jnp.zeros((SUBLANES, GDN_QKV), F32)
        s_s[...] = jnp.zeros_like(s_s)

    xbuf[SUBLANES:SUBLANES + tl, :] = qkva_ref[...]
    cw = cw_ref[...]
    y = xbuf[pl.ds(SUBLANES - 3, tl), :] * cw[0:1]
    for j in range(1, CONV_W):
        y = y + xbuf[pl.ds(SUBLANES - 3 + j, tl), :] * cw[j:j + 1]
    y = _silu(y)
    xbuf[0:SUBLANES, :] = xbuf[tl:tl + SUBLANES, :]
    for i, part in enumerate(_gdn_head_norms(y)):
        qkv_s[:, i * HEAD_DIM:(i + 1) * HEAD_DIM] = part
    qkv_s[:, 2 * H_A * HEAD_DIM:] = y[:, 2 * H_A * HEAD_DIM:]

    ri = lax.broadcasted_iota(jnp.int32, (CHUNK, CHUNK), 0)
    ci = lax.broadcasted_iota(jnp.int32, (CHUNK, CHUNK), 1)
    tri = ri >= ci
    strict = ri > ci
    ltri = tri.astype(F32)
    ones = jnp.ones((CHUNK, CHUNK), F32)
    eye = (ri == ci).astype(F32)
    gnw = gnw_ref[...]

    def chunk_body(c, carry):
        r0 = pl.multiple_of(c * CHUNK, CHUNK)
        gbt = gb_ref[pl.ds(r0, CHUNK), :]
        zt = z_ref[pl.ds(r0, CHUNK), :]
        for h in range(H_A):
            lo, hi = h * HEAD_DIM, (h + 1) * HEAD_DIM
            q = qkv_s[pl.ds(r0, CHUNK), lo:hi]
            k = qkv_s[pl.ds(r0, CHUNK), H_A * HEAD_DIM + lo:H_A * HEAD_DIM + hi]
            v = qkv_s[pl.ds(r0, CHUNK), 2 * H_A * HEAD_DIM + lo:2 * H_A * HEAD_DIM + hi]
            gcol = gbt[:, h:h + 1]
            bcol = gbt[:, H_A + h:H_A + h + 1]
            gwide = jnp.broadcast_to(gcol, (CHUNK, HEAD_DIM))
            gcum = _dot(ltri, gwide, HIGHEST)
            gsq = gwide[:, :CHUNK]
            gcum_j = _dot(ones, jnp.where(ri <= ci, gsq, 0.0), HIGHEST)
            decay = jnp.where(tri, jnp.exp(gcum[:, :CHUNK] - gcum_j), 0.0)
            egc = jnp.exp(gcum)
            g_last = gcum[CHUNK - 1:CHUNK, :]
            kbeta = k * bcol
            a = jnp.where(strict, _dot_nt(kbeta, k, HIGHEST) * decay, 0.0)
            p = -a
            tinv = eye + p
            for _ in range(5):
                p = _dot(p, p, HIGHEST)
                tinv = tinv + _dot(tinv, p, HIGHEST)
            u = _dot(tinv, v * bcol, HIGHEST)
            w = _dot(tinv, kbeta * egc, HIGHEST)
            qk = _dot_nt(q, k, HIGHEST) * decay
            s = s_s[h]
            v_new = u - _dot(w, s, HIGHEST)
            o = _dot(q * egc, s, HIGHEST) + _dot(qk, v_new, HIGHEST)
            k_dec = k * jnp.exp(g_last - gcum)
            s_s[h] = s * jnp.exp(g_last) + _dot(k_dec.T, v_new, HIGHEST)
            o_ref[pl.ds(r0, CHUNK), lo:hi] = _gdn_out_norm(o, gnw, zt[:, lo:hi])
        return carry

    lax.fori_loop(0, tl // CHUNK, chunk_body, 0)

    @pl.when(t == n_t - 1)
    def _():
        sfin_ref[0] = s_s[...]


def _gdn_prompt(qkva, gb, z, cw, gnw, batch, seq, tl):
    n_t = seq // tl
    row = lambda b, t: (b * n_t + t, 0)
    return pl.pallas_call(
        functools.partial(_gdn_prompt_kernel, tl=tl, n_t=n_t),
        grid=(batch, n_t),
        in_specs=[pl.BlockSpec((tl, GDN_QKV), row), pl.BlockSpec((tl, AB_PAD), row),
                  pl.BlockSpec((tl, Z_COLS), row), _const_spec((CONV_W, GDN_QKV)),
                  _const_spec((1, HEAD_DIM))],
        out_specs=[pl.BlockSpec((tl, Z_COLS), row),
                   pl.BlockSpec((1, H_A, HEAD_DIM, HEAD_DIM), lambda b, t: (b, 0, 0, 0))],
        out_shape=[jax.ShapeDtypeStruct((batch * seq, Z_COLS), F32),
                   jax.ShapeDtypeStruct((batch, H_A, HEAD_DIM, HEAD_DIM), F32)],
        scratch_shapes=[pltpu.VMEM((tl + SUBLANES, GDN_QKV), F32),
                        pltpu.VMEM((tl, GDN_QKV), F32),
                        pltpu.VMEM((H_A, HEAD_DIM, HEAD_DIM), F32)],
        compiler_params=_params(("parallel", "arbitrary")),
        name="gdn_prompt",
    )(qkva, gb, z, cw, gnw)


def _row0(x, rows=SUBLANES):
    r = lax.broadcasted_iota(jnp.int32, (rows, x.shape[1]), 0)
    return jnp.where(r == 0, jnp.broadcast_to(x, (rows, x.shape[1])), 0.0)


def _gdn_sample_kernel(xn_ref, p0_ref, p1_ref, p2_ref, gb_ref, z_ref, cw_ref, gnw_ref, s0_ref,
                       o_ref, sout_ref):
    b = pl.program_id(0)
    row = lambda r: r[pl.ds(b, 1), :]
    cw = cw_ref[...]
    y = row(p0_ref) * cw[0:1]
    y = y + row(p1_ref) * cw[1:2]
    y = y + row(p2_ref) * cw[2:3]
    y = y + row(xn_ref) * cw[3:4]
    y = _silu(y)
    qk = _gdn_head_norms(y)
    gbr = row(gb_ref)
    zr = row(z_ref)
    gnw = gnw_ref[...]
    ri = lax.broadcasted_iota(jnp.int32, (HEAD_DIM, HEAD_DIM), 0)
    ci = lax.broadcasted_iota(jnp.int32, (HEAD_DIM, HEAD_DIM), 1)
    eye = (ri == ci).astype(F32)
    for h in range(H_A):
        lo, hi = h * HEAD_DIM, (h + 1) * HEAD_DIM
        q, k = qk[h], qk[H_A + h]
        v = y[:, 2 * H_A * HEAD_DIM + lo:2 * H_A * HEAD_DIM + hi]
        eg = jnp.exp(gbr[:, h:h + 1])
        beta = gbr[:, H_A + h:H_A + h + 1]
        s0 = s0_ref[0, h]
        k8 = _row0(k)
        v_new = beta * (v - eg * _dot(k8, s0, HIGHEST)[0:1])
        k_col = _dot_nt(eye, k8, HIGHEST)[:, 0:1]
        s_new = s0 * eg + k_col * v_new
        o = _dot(_row0(q), s_new, HIGHEST)[0:1]
        o_ref[0, :, lo:hi] = _gdn_out_norm(o, gnw, zr[:, lo:hi])
        sout_ref[0, h] = s_new


def _gdn_sample(xn, p0, p1, p2, gb, z, cw, gnw, s0):
    nb = xn.shape[0]
    full = lambda a: _const_spec(a.shape)
    st = pl.BlockSpec((1, H_A, HEAD_DIM, HEAD_DIM), lambda b: (b, 0, 0, 0))
    return pl.pallas_call(
        _gdn_sample_kernel,
        grid=(nb,),
        in_specs=[full(xn), full(p0), full(p1), full(p2), full(gb), full(z), full(cw), full(gnw), st],
        out_specs=[pl.BlockSpec((1, 1, Z_COLS), lambda b: (b, 0, 0)), st],
        out_shape=[jax.ShapeDtypeStruct((nb, 1, Z_COLS), F32),
                   jax.ShapeDtypeStruct(s0.shape, F32)],
        compiler_params=_params(("parallel",)),
        name="gdn_sample",
    )(xn, p0, p1, p2, gb, z, cw, gnw, s0)


def _moba_prompt_kernel(q_ref, k_ref, v_ref, o_ref, *, seq):
    nb = seq // MOBA_BLOCK
    k = k_ref[...]
    kb = k.astype(BF16)
    vb = v_ref[...].astype(BF16)
    means = [jnp.sum(k[j * MOBA_BLOCK:(j + 1) * MOBA_BLOCK], axis=0, keepdims=True) * (1.0 / MOBA_BLOCK)
             for j in range(nb)]
    km = jnp.concatenate(means + [jnp.zeros((LANES - nb, HEAD_DIM), F32)], axis=0)
    lane = lax.broadcasted_iota(jnp.int32, (MOBA_BLOCK, LANES), 1)
    ri = lax.broadcasted_iota(jnp.int32, (MOBA_BLOCK, MOBA_BLOCK), 0)
    ci = lax.broadcasted_iota(jnp.int32, (MOBA_BLOCK, MOBA_BLOCK), 1)
    causal_bias = jnp.where(ri >= ci, 0.0, NEG_INF)
    scale = HEAD_DIM ** -0.5
    for n in range(nb):
        q = q_ref[n * MOBA_BLOCK:(n + 1) * MOBA_BLOCK, :]
        gate = _dot_nt(q, km, HIGHEST)
        cnt = jnp.zeros((MOBA_BLOCK, LANES), jnp.int32)
        for i in range(n):
            gi = gate[:, i:i + 1]
            beats = (gi > gate) | ((gi == gate) & (i < lane))
            cnt = cnt + beats.astype(jnp.int32)
        sel_bias = jnp.where((cnt < MOBA_TOPK) & (lane < n), 0.0, NEG_INF)
        nk = (n + 1) * MOBA_BLOCK
        s = _dot_nt(q.astype(BF16), kb[:nk]) * scale
        bias = jnp.concatenate(
            [jnp.broadcast_to(sel_bias[:, j:j + 1], (MOBA_BLOCK, MOBA_BLOCK)) for j in range(n)]
            + [causal_bias], axis=1)
        s = s + bias
        m = jnp.max(s, axis=-1, keepdims=True)
        p = jnp.exp(s - m)
        l = jnp.sum(p, axis=-1, keepdims=True)
        o_ref[n * MOBA_BLOCK:(n + 1) * MOBA_BLOCK, :] = _dot(p.astype(BF16), vb[:nk]) / l


def _moba_prompt(qb, kb, vb, batch, seq):
    spec = pl.BlockSpec((seq, HEAD_DIM), lambda b, h: (b, h))
    return pl.pallas_call(
        functools.partial(_moba_prompt_kernel, seq=seq),
        grid=(batch, H_B),
        in_specs=[spec, spec, spec],
        out_specs=spec,
        out_shape=jax.ShapeDtypeStruct((batch * seq, H_B * HEAD_DIM), F32),
        compiler_params=_params(("parallel", "parallel")),
        name="moba_prompt",
    )(qb, kb, vb)


PAGES_PER_STEP = 16


def _page_sum_kernel(pt_ref, *refs):
    pages, out_ref = refs[:PAGES_PER_STEP], refs[PAGES_PER_STEP]
    for i in range(0, PAGES_PER_STEP, PAGES_PER_BLOCK):
        s = jnp.sum(pages[i][0], axis=0, keepdims=True)
        for j in range(1, PAGES_PER_BLOCK):
            s = s + jnp.sum(pages[i + j][0], axis=0, keepdims=True)
        r = i // PAGES_PER_BLOCK
        out_ref[0, r:r + 1, :] = s


def _page_sums(cache2, page_table):
    nb, n_pages = page_table.shape
    width = cache2.shape[2]
    steps = n_pages // PAGES_PER_STEP
    blocks_per_step = PAGES_PER_STEP // PAGES_PER_BLOCK

    def page_spec(i):
        return pl.BlockSpec((1, PAGE_SIZE, width), lambda b, g, pt: (pt[b, g * PAGES_PER_STEP + i], 0, 0))

    return pl.pallas_call(
        _page_sum_kernel,
        grid_spec=pltpu.PrefetchScalarGridSpec(
            num_scalar_prefetch=1, grid=(nb, steps),
            in_specs=[page_spec(i) for i in range(PAGES_PER_STEP)],
            out_specs=pl.BlockSpec((1, blocks_per_step, width), lambda b, g, pt: (b, g, 0))),
        out_shape=jax.ShapeDtypeStruct((nb, n_pages // PAGES_PER_BLOCK, width), F32),
        compiler_params=_params(("parallel", "parallel")),
        name="page_sums",
    )(page_table, *([cache2] * PAGES_PER_STEP))


def _moba_select_kernel(q_ref, ps_ref, idx_ref):
    b = pl.program_id(0)
    q = q_ref[pl.ds(b, 1), :]
    km = ps_ref[0] * (1.0 / MOBA_BLOCK)
    n_blocks = km.shape[0]
    row = lax.broadcasted_iota(jnp.int32, (SUBLANES, LANES), 0)
    lane = lax.broadcasted_iota(jnp.int32, (SUBLANES, LANES), 1)
    blk = lax.broadcasted_iota(jnp.int32, (1, n_blocks), 1).astype(F32)
    out = jnp.zeros((SUBLANES, LANES), F32)
    for h in range(H_B):
        lo, hi = h * HEAD_DIM, (h + 1) * HEAD_DIM
        g = _dot_nt(_row0(q[:, lo:hi]), km[:, lo:hi], HIGHEST)[0:1]
        for s in range(MOBA_TOPK):
            m = jnp.max(g, axis=1, keepdims=True)
            idx = jnp.min(jnp.where(g == m, blk, float(n_blocks)), axis=1, keepdims=True)
            out = jnp.where((row == h) & (lane == s), idx, out)
            g = jnp.where(blk == idx, NEG_INF, g)
    idx_ref[0] = out.astype(jnp.int32)


def _moba_select(qb, psums):
    nb, n_blocks, width = psums.shape
    return pl.pallas_call(
        _moba_select_kernel,
        grid=(nb,),
        in_specs=[_const_spec(qb.shape), pl.BlockSpec((1, n_blocks, width), lambda b: (b, 0, 0))],
        out_specs=pl.BlockSpec((1, SUBLANES, LANES), lambda b: (b, 0, 0)),
        out_shape=jax.ShapeDtypeStruct((nb, SUBLANES, LANES), jnp.int32),
        compiler_params=_params(("parallel",)),
        name="moba_select",
    )(qb, psums)


N_SEL_PAGES = MOBA_TOPK * PAGES_PER_BLOCK


def _moba_attend_kernel(idx_ref, pt_ref, *refs):
    kp, vp = refs[:N_SEL_PAGES], refs[N_SEL_PAGES:2 * N_SEL_PAGES]
    q_ref, k_ref, v_ref, o_ref = refs[2 * N_SEL_PAGES:]
    b = pl.program_id(0)
    q = q_ref[pl.ds(b, 1), :]
    k_new = k_ref[pl.ds(b, 1), :]
    v_new = v_ref[pl.ds(b, 1), :]
    keys = jnp.concatenate([r[0] for r in kp], axis=0)
    vals = jnp.concatenate([r[0] for r in vp], axis=0)
    scale = HEAD_DIM ** -0.5
    s = _dot_nt(_row0(q), keys, HIGHEST)[0:1] * scale
    s_new = jnp.sum(q * k_new, axis=1, keepdims=True) * scale
    m = jnp.maximum(jnp.max(s, axis=1, keepdims=True), s_new)
    p = jnp.exp(s - m)
    p_new = jnp.exp(s_new - m)
    l = jnp.sum(p, axis=1, keepdims=True) + p_new
    o_ref[0] = (_dot(_row0(p), vals, HIGHEST)[0:1] + p_new * v_new) / l


def _moba_attend(idx_flat, page_table, cache_k2, cache_v2, qb, kb, vb):
    nb = qb.shape[0]

    def page_spec(s, p):
        def index(b, h, idx, pt):
            blk = idx[(b * H_B + h) * MOBA_TOPK + s]
            return (pt[b, blk * PAGES_PER_BLOCK + p], 0, h)
        return pl.BlockSpec((1, PAGE_SIZE, HEAD_DIM), index)

    page_specs = [page_spec(s, p) for s in range(MOBA_TOPK) for p in range(PAGES_PER_BLOCK)]
    head = pl.BlockSpec((nb, HEAD_DIM), lambda b, h, idx, pt: (0, h))
    out = pl.pallas_call(
        _moba_attend_kernel,
        grid_spec=pltpu.PrefetchScalarGridSpec(
            num_scalar_prefetch=2, grid=(nb, H_B),
            in_specs=page_specs + page_specs + [head, head, head],
            out_specs=pl.BlockSpec((1, 1, HEAD_DIM), lambda b, h, idx, pt: (b, 0, h))),
        out_shape=jax.ShapeDtypeStruct((nb, 1, H_B * HEAD_DIM), F32),
        compiler_params=_params(("parallel", "parallel")),
        name="moba_attend",
    )(idx_flat, page_table, *([cache_k2] * N_SEL_PAGES), *([cache_v2] * N_SEL_PAGES), qb, kb, vb)
    return out.reshape(nb, H_B * HEAD_DIM)


def _out_ffn_kernel(x_ref, oa_ref, ob_ref, woa_ref, wob_ref, n2_ref, wg_ref, wu_ref, wd_ref, fn_ref, y_ref):
    x1 = (x_ref[...] + _dot(oa_ref[...].astype(BF16), woa_ref[...])
          + _dot(ob_ref[...].astype(BF16), wob_ref[...]))
    h = _rmsnorm(x1, n2_ref[...]).astype(BF16)
    act = _silu(_dot(h, wg_ref[...])) * _dot(h, wu_ref[...])
    x2 = x1 + _dot(act.astype(BF16), wd_ref[...])
    y_ref[...] = _rmsnorm(x2, fn_ref[...])


def _out_ffn(x, oa, ob, woa, wob, n2, wg, wu, wd, fn, tm):
    t = x.shape[0]
    row = lambda i: (i, 0)
    full = lambda a: _const_spec(a.shape)
    return pl.pallas_call(
        _out_ffn_kernel,
        grid=(t // tm,),
        in_specs=[pl.BlockSpec((tm, D_MODEL), row), pl.BlockSpec((tm, Z_COLS), row),
                  pl.BlockSpec((tm, H_B * HEAD_DIM), row), full(woa), full(wob), full(n2),
                  full(wg), full(wu), full(wd), full(fn)],
        out_specs=pl.BlockSpec((tm, D_MODEL), row),
        out_shape=jax.ShapeDtypeStruct((t, D_MODEL), F32),
        compiler_params=_params(("parallel",)),
        name="out_ffn",
    )(x, oa, ob, woa, wob, n2, wg, wu, wd, fn)


def _rotary_tables(pos):
    half = ROT_DIM // 2
    inv_freq = ROPE_THETA ** (-jnp.arange(half, dtype=F32) / half)
    ang = pos.astype(F32)[:, None] * inv_freq[None, :]
    cos, sin = jnp.cos(ang), jnp.sin(ang)
    n = pos.shape[0]
    rest = HEAD_DIM - ROT_DIM
    cos_t = jnp.concatenate([cos, cos, jnp.ones((n, rest), F32)], axis=1)
    zeros_h = jnp.zeros((n, half), F32)
    sin_a = jnp.concatenate([-sin, zeros_h, jnp.zeros((n, rest), F32)], axis=1)
    sin_b = jnp.concatenate([zeros_h, sin, jnp.zeros((n, rest), F32)], axis=1)
    return cos_t, sin_a, sin_b


def kernel(x_prompt, x_sample, cache_k, cache_v, page_table, state_gdn, state_conv, norm1_w, w_in, conv_w,
           a_log, dt_bias, gdn_norm_w, w_out, norm2_w, w_gate, w_up, w_down, final_norm_w):
    depth = w_in.shape[0]
    assert depth == 1, "single-layer trunk"
    batch, seq, _ = x_prompt.shape
    dec_batch, dec_seq, _ = x_sample.shape
    assert dec_seq == 1
    n_phys = cache_k.shape[1]
    past_len = page_table.shape[1] * PAGE_SIZE

    wi = w_in[0]
    ab0 = GDN_QKV + Z_COLS
    w_all = jnp.concatenate(
        [wi[:, :ab0], wi[:, ab0 + 2 * H_A:], wi[:, ab0:ab0 + 2 * H_A],
         jnp.zeros((D_MODEL, AB_PAD - 2 * H_A), wi.dtype)], axis=1).astype(BF16)
    gpar = jnp.zeros((SUBLANES, LANES), F32)
    gpar = gpar.at[0, :H_A].set(a_log[0].astype(F32)).at[1, :H_A].set(dt_bias[0].astype(F32))
    nw1 = norm1_w[0].reshape(1, D_MODEL)
    cw = conv_w[0]
    gnw = gdn_norm_w[0].reshape(1, HEAD_DIM)
    woa = w_out[0][:Z_COLS].astype(BF16)
    wob = w_out[0][Z_COLS:].astype(BF16)
    n2 = norm2_w[0].reshape(1, D_MODEL)
    wg, wu, wd = w_gate[0].astype(BF16), w_up[0].astype(BF16), w_down[0].astype(BF16)
    fn = final_norm_w.reshape(1, D_MODEL)

    xp = x_prompt.reshape(batch * seq, D_MODEL)
    tabs_p = _rotary_tables(jnp.arange(seq, dtype=jnp.int32))
    qkva_p, z_p, gb_p, qb_p, kb_p, vb_p = _in_proj(xp, nw1, w_all, gpar, *tabs_p, tm=256)
    oa_p, gdn_p = _gdn_prompt(qkva_p, gb_p, z_p, cw, gnw, batch, seq, tl=256)
    ob_p = _moba_prompt(qb_p, kb_p, vb_p, batch, seq)
    y_p = _out_ffn(xp, oa_p, ob_p, woa, wob, n2, wg, wu, wd, fn, tm=256)

    xs = x_sample.reshape(dec_batch, D_MODEL)
    tabs_s = _rotary_tables(jnp.full((dec_batch,), past_len, jnp.int32))
    qkva_s, z_s, gb_s, qb_s, kb_s, vb_s = _in_proj(xs, nw1, w_all, gpar, *tabs_s, tm=dec_batch)
    sc = state_conv[0]
    oa_s, gdn_s = _gdn_sample(qkva_s, sc[:, 0], sc[:, 1], sc[:, 2], gb_s, z_s, cw, gnw, state_gdn[0])
    cache_k2 = cache_k[0].reshape(n_phys, PAGE_SIZE, H_B * HEAD_DIM)
    cache_v2 = cache_v[0].reshape(n_phys, PAGE_SIZE, H_B * HEAD_DIM)
    psums = _page_sums(cache_k2, page_table)
    idx = _moba_select(qb_s, psums)
    idx_flat = idx[:, :H_B, :MOBA_TOPK].reshape(-1)
    ob_s = _moba_attend(idx_flat, page_table, cache_k2, cache_v2, qb_s, kb_s, vb_s)
    y_s = _out_ffn(xs, oa_s.reshape(dec_batch, Z_COLS), ob_s, woa, wob, n2, wg, wu, wd, fn, tm=dec_batch)

    conv_p = qkva_p.reshape(batch, seq, GDN_QKV)[:, seq - (CONV_W - 1):]
    conv_s = jnp.concatenate([sc[:, 1:], qkva_s[:, None, :]], axis=1)
    return (y_p.reshape(batch, seq, D_MODEL), y_s.reshape(dec_batch, 1, D_MODEL),
            kb_p.reshape(1, batch, seq, H_B, HEAD_DIM), vb_p.reshape(1, batch, seq, H_B, HEAD_DIM),
            kb_s.reshape(1, dec_batch, 1, H_B, HEAD_DIM), vb_s.reshape(1, dec_batch, 1, H_B, HEAD_DIM),
            gdn_p[None], gdn_s[None], conv_p[None], conv_s[None])
```

```python
import functools

import jax
import jax.numpy as jnp
from jax import lax
from jax.experimental import pallas as pl
from jax.experimental.pallas import tpu as pltpu

D_MODEL = 1024
HEAD_DIM = 128
H_A = 4
H_B = 4
GDN_QKV = 3 * H_A * HEAD_DIM
MOBA_QKV = 3 * H_B * HEAD_DIM
Z_COLS = H_A * HEAD_DIM
CONV_W = 4
CHUNK = 64
MOBA_BLOCK = 256
MOBA_TOPK = 3
PAGE_SIZE = 128
PAGES_PER_BLOCK = MOBA_BLOCK // PAGE_SIZE
ROT_DIM = HEAD_DIM // 4
ROPE_THETA = 500000.0
EPS = 1e-6

LANES = 128
SUBLANES = 8
AB_PAD = LANES
PROJ_COLS = GDN_QKV + Z_COLS + MOBA_QKV + AB_PAD
VMEM_LIMIT = 56 * 1024 * 1024

F32 = jnp.float32
BF16 = jnp.bfloat16
HIGHEST = lax.Precision.HIGHEST
NEG_INF = float("-inf")


def _dot(a, b, precision=None):
    return jnp.dot(a, b, precision=precision, preferred_element_type=F32)


def _dot_nt(a, b, precision=None):
    return lax.dot_general(a, b, (((1,), (1,)), ((), ())), precision=precision,
                           preferred_element_type=F32)


def _rmsnorm(x, w):
    return x * lax.rsqrt(jnp.mean(x * x, axis=-1, keepdims=True) + EPS) * w


def _l2norm(x):
    return x * lax.rsqrt(jnp.sum(x * x, axis=-1, keepdims=True) + EPS)


def _silu(x):
    return x * jax.nn.sigmoid(x)


def _const_spec(shape):
    return pl.BlockSpec(shape, lambda *_: (0,) * len(shape), pipeline_mode=pl.Buffered(1))


def _params(semantics):
    return pltpu.CompilerParams(dimension_semantics=semantics, vmem_limit_bytes=VMEM_LIMIT)


def _in_proj_kernel(x_ref, nw_ref, w_ref, gpar_ref, cos_ref, sa_ref, sb_ref,
                    qkva_ref, z_ref, gb_ref, qb_ref, kb_ref, vb_ref):
    xn = _rmsnorm(x_ref[...], nw_ref[...])
    proj = _dot(xn.astype(BF16), w_ref[...])
    qkva_ref[...] = proj[:, :GDN_QKV]
    z_ref[...] = proj[:, GDN_QKV:GDN_QKV + Z_COLS]
    b0 = GDN_QKV + Z_COLS
    ab = proj[:, b0 + MOBA_QKV:]
    lane = lax.broadcasted_iota(jnp.int32, ab.shape, 1)
    g = -jnp.exp(gpar_ref[0:1, :]) * jax.nn.softplus(ab + gpar_ref[1:2, :])
    gb_ref[...] = jnp.where(lane < H_A, g, jax.nn.sigmoid(ab))
    cos, sa, sb = cos_ref[...], sa_ref[...], sb_ref[...]
    half = ROT_DIM // 2

    def rot(t):
        return (t * cos + pltpu.roll(t, HEAD_DIM - half, 1) * sa + pltpu.roll(t, half, 1) * sb)

    for h in range(H_B):
        lo, hi = h * HEAD_DIM, (h + 1) * HEAD_DIM
        qb_ref[:, lo:hi] = rot(proj[:, b0 + lo:b0 + hi])
        kb_ref[:, lo:hi] = rot(proj[:, b0 + H_B * HEAD_DIM + lo:b0 + H_B * HEAD_DIM + hi])
    vb_ref[...] = proj[:, b0 + 2 * H_B * HEAD_DIM:b0 + MOBA_QKV]


def _in_proj(x, nw, w_all, gpar, cos, sa, sb, tm):
    t = x.shape[0]
    n_tab = cos.shape[0] // tm
    row = lambda i: (i, 0)
    tab = lambda i: (i % n_tab, 0)
    widths = (GDN_QKV, Z_COLS, AB_PAD, H_B * HEAD_DIM, H_B * HEAD_DIM, H_B * HEAD_DIM)
    return pl.pallas_call(
        _in_proj_kernel,
        grid=(t // tm,),
        in_specs=[pl.BlockSpec((tm, D_MODEL), row), _const_spec((1, D_MODEL)),
                  _const_spec((D_MODEL, PROJ_COLS)), _const_spec((SUBLANES, LANES)),
                  pl.BlockSpec((tm, LANES), tab), pl.BlockSpec((tm, LANES), tab),
                  pl.BlockSpec((tm, LANES), tab)],
        out_specs=[pl.BlockSpec((tm, w), row) for w in widths],
        out_shape=[jax.ShapeDtypeStruct((t, w), F32) for w in widths],
        compiler_params=_params(("parallel",)),
        name="in_proj",
    )(x, nw, w_all, gpar, cos, sa, sb)


def _gdn_head_norms(y):
    parts = []
    for h in range(H_A):
        parts.append(_l2norm(y[:, h * HEAD_DIM:(h + 1) * HEAD_DIM]) * (HEAD_DIM ** -0.5))
    for h in range(H_A, 2 * H_A):
        parts.append(_l2norm(y[:, h * HEAD_DIM:(h + 1) * HEAD_DIM]))
    return parts


def _gdn_out_norm(o, gnw, z):
    return _rmsnorm(o, gnw) * _silu(z)


def _gdn_prompt_kernel(qkva_ref, gb_ref, z_ref, cw_ref, gnw_ref, o_ref, sfin_ref,
                       xbuf, u_s, wq_s, kdt_s, qk_s, egl_s, s_s, *, tl, n_t):
    t = pl.program_id(1)
    n_c = tl // CHUNK

    @pl.when(t == 0)
    def _():
        xbuf[0:SUBLANES, :] = jnp.zeros((SUBLANES, GDN_QKV), F32)
        s_s[...] = jnp.zeros_like(s_s)

    xbuf[SUBLANES:SUBLANES + tl, :] = qkva_ref[...]
    cw = cw_ref[...]
    y = xbuf[pl.ds(SUBLANES - 3, tl), :] * cw[0:1]
    for j in range(1, CONV_W):
        y = y + xbuf[pl.ds(SUBLANES - 3 + j, tl), :] * cw[j:j + 1]
    y = _silu(y)
    xbuf[0:SUBLANES, :] = xbuf[tl:tl + SUBLANES, :]
    qk_parts = _gdn_head_norms(y)

    ri = lax.broadcasted_iota(jnp.int32, (tl, tl), 0)
    ci = lax.broadcasted_iota(jnp.int32, (tl, tl), 1)
    same = (ri // CHUNK) == (ci // CHUNK)
    tri = same & (ri >= ci)
    strict = same & (ri > ci)
    eye = (ri == ci).astype(F32)
    gb = gb_ref[...]
    sums = _dot(jnp.concatenate([tri.astype(F32), same.astype(F32)], axis=0), gb, HIGHEST)
    gc_all, gl_all = sums[:tl], sums[tl:]
    gc_t = gc_all.T

    for h in range(H_A):
        lo, hi = h * HEAD_DIM, (h + 1) * HEAD_DIM
        q, k = qk_parts[h], qk_parts[H_A + h]
        v = y[:, 2 * H_A * HEAD_DIM + lo:2 * H_A * HEAD_DIM + hi]
        beta = gb[:, H_A + h:H_A + h + 1]
        gcol = gc_all[:, h:h + 1]
        glcol = gl_all[:, h:h + 1]
        decay = jnp.where(tri, jnp.exp(gcol - gc_t[h:h + 1, :]), 0.0)
        egc = jnp.exp(gcol)
        kbeta = k * beta
        qkk = _dot_nt(jnp.concatenate([q, kbeta], axis=0).astype(BF16), k.astype(BF16))
        qk = qkk[:tl] * decay
        a = jnp.where(strict, qkk[tl:] * decay, 0.0)
        p = -a
        tinv = eye + p
        for _ in range(5):
            pb = p.astype(BF16)
            p = _dot(pb, pb)
            tinv = tinv + _dot(tinv.astype(BF16), p.astype(BF16))
        uw = _dot(tinv.astype(BF16), jnp.concatenate([v * beta, kbeta * egc], axis=1).astype(BF16))
        u_s[:, lo:hi] = uw[:, :HEAD_DIM]
        w = uw[:, HEAD_DIM:].astype(BF16)
        qd = (q * egc).astype(BF16)
        kd = k * jnp.exp(glcol - gcol)
        egl = jnp.exp(glcol)
        for c in range(n_c):
            r0, r1 = c * CHUNK, (c + 1) * CHUNK
            wq_s[h, c, 0:CHUNK, :] = w[r0:r1]
            wq_s[h, c, CHUNK:2 * CHUNK, :] = qd[r0:r1]
            kdt_s[h, c] = kd[r0:r1].T.astype(BF16)
            qk_s[h, c] = qk[r0:r1, r0:r1].astype(BF16)
            egl_s[h, c] = jnp.broadcast_to(egl[r0:r0 + 1, :], (SUBLANES, HEAD_DIM))

    gnw = gnw_ref[...]
    for c in range(n_c):
        r0, r1 = c * CHUNK, (c + 1) * CHUNK
        zt = z_ref[r0:r1, :]
        for h in range(H_A):
            lo, hi = h * HEAD_DIM, (h + 1) * HEAD_DIM
            s = s_s[h]
            ws = _dot(wq_s[h, c], s.astype(BF16))
            v_new = u_s[r0:r1, lo:hi] - ws[:CHUNK]
            vnb = v_new.astype(BF16)
            o = ws[CHUNK:] + _dot(qk_s[h, c], vnb)
            s_s[h] = s * egl_s[h, c][0:1] + _dot(kdt_s[h, c], vnb)
            o_ref[r0:r1, lo:hi] = _gdn_out_norm(o, gnw, zt[:, lo:hi])

    @pl.when(t == n_t - 1)
    def _():
        sfin_ref[0] = s_s[...]


def _gdn_prompt(qkva, gb, z, cw, gnw, batch, seq, tl):
    n_t = seq // tl
    n_c = tl // CHUNK
    row = lambda b, t: (b * n_t + t, 0)
    return pl.pallas_call(
        functools.partial(_gdn_prompt_kernel, tl=tl, n_t=n_t),
        grid=(batch, n_t),
        in_specs=[pl.BlockSpec((tl, GDN_QKV), row), pl.BlockSpec((tl, AB_PAD), row),
                  pl.BlockSpec((tl, Z_COLS), row), _const_spec((CONV_W, GDN_QKV)),
                  _const_spec((1, HEAD_DIM))],
        out_specs=[pl.BlockSpec((tl, Z_COLS), row),
                   pl.BlockSpec((1, H_A, HEAD_DIM, HEAD_DIM), lambda b, t: (b, 0, 0, 0))],
        out_shape=[jax.ShapeDtypeStruct((batch * seq, Z_COLS), F32),
                   jax.ShapeDtypeStruct((batch, H_A, HEAD_DIM, HEAD_DIM), F32)],
        scratch_shapes=[pltpu.VMEM((tl + SUBLANES, GDN_QKV), F32),
                        pltpu.VMEM((tl, Z_COLS), F32),
                        pltpu.VMEM((H_A, n_c, 2 * CHUNK, HEAD_DIM), BF16),
                        pltpu.VMEM((H_A, n_c, HEAD_DIM, CHUNK), BF16),
                        pltpu.VMEM((H_A, n_c, CHUNK, CHUNK), BF16),
                        pltpu.VMEM((H_A, n_c, SUBLANES, HEAD_DIM), F32),
                        pltpu.VMEM((H_A, HEAD_DIM, HEAD_DIM), F32)],
        compiler_params=_params(("parallel", "arbitrary")),
        name="gdn_prompt",
    )(qkva, gb, z, cw, gnw)


def _row0(x, rows=SUBLANES):
    r = lax.broadcasted_iota(jnp.int32, (rows, x.shape[1]), 0)
    return jnp.where(r == 0, jnp.broadcast_to(x, (rows, x.shape[1])), 0.0)


def _gdn_sample_kernel(xn_ref, p0_ref, p1_ref, p2_ref, gb_ref, z_ref, cw_ref, gnw_ref, s0_ref,
                       o_ref, sout_ref):
    b = pl.program_id(0)
    row = lambda r: r[pl.ds(b, 1), :]
    cw = cw_ref[...]
    y = row(p0_ref) * cw[0:1]
    y = y + row(p1_ref) * cw[1:2]
    y = y + row(p2_ref) * cw[2:3]
    y = y + row(xn_ref) * cw[3:4]
    y = _silu(y)
    qk = _gdn_head_norms(y)
    gbr = row(gb_ref)
    zr = row(z_ref)
    gnw = gnw_ref[...]
    ri = lax.broadcasted_iota(jnp.int32, (HEAD_DIM, HEAD_DIM), 0)
    ci = lax.broadcasted_iota(jnp.int32, (HEAD_DIM, HEAD_DIM), 1)
    eye = (ri == ci).astype(F32)
    for h in range(H_A):
        lo, hi = h * HEAD_DIM, (h + 1) * HEAD_DIM
        q, k = qk[h], qk[H_A + h]
        v = y[:, 2 * H_A * HEAD_DIM + lo:2 * H_A * HEAD_DIM + hi]
        eg = jnp.exp(gbr[:, h:h + 1])
        beta = gbr[:, H_A + h:H_A + h + 1]
        s0 = s0_ref[0, h]
        k8 = _row0(k)
        v_new = beta * (v - eg * _dot(k8, s0, HIGHEST)[0:1])
        k_col = _dot_nt(eye, k8, HIGHEST)[:, 0:1]
        s_new = s0 * eg + k_col * v_new
        o = _dot(_row0(q), s_new, HIGHEST)[0:1]
        o_ref[0, :, lo:hi] = _gdn_out_norm(o, gnw, zr[:, lo:hi])
        sout_ref[0, h] = s_new


def _gdn_sample(xn, p0, p1, p2, gb, z, cw, gnw, s0):
    nb = xn.shape[0]
    full = lambda a: _const_spec(a.shape)
    st = pl.BlockSpec((1, H_A, HEAD_DIM, HEAD_DIM), lambda b: (b, 0, 0, 0))
    return pl.pallas_call(
        _gdn_sample_kernel,
        grid=(nb,),
        in_specs=[full(xn), full(p0), full(p1), full(p2), full(gb), full(z), full(cw), full(gnw), st],
        out_specs=[pl.BlockSpec((1, 1, Z_COLS), lambda b: (b, 0, 0)), st],
        out_shape=[jax.ShapeDtypeStruct((nb, 1, Z_COLS), F32),
                   jax.ShapeDtypeStruct(s0.shape, F32)],
        compiler_params=_params(("parallel",)),
        name="gdn_sample",
    )(xn, p0, p1, p2, gb, z, cw, gnw, s0)


def _moba_prompt_kernel(q_ref, k_ref, v_ref, o_ref, *, seq):
    nb = seq // MOBA_BLOCK
    k = k_ref[...]
    kb = k.astype(BF16)
    vb = v_ref[...].astype(BF16)
    means = [jnp.sum(k[j * MOBA_BLOCK:(j + 1) * MOBA_BLOCK], axis=0, keepdims=True) * (1.0 / MOBA_BLOCK)
             for j in range(nb)]
    km = jnp.concatenate(means + [jnp.zeros((LANES - nb, HEAD_DIM), F32)], axis=0)
    lane = lax.broadcasted_iota(jnp.int32, (MOBA_BLOCK, LANES), 1)
    ri = lax.broadcasted_iota(jnp.int32, (MOBA_BLOCK, MOBA_BLOCK), 0)
    ci = lax.broadcasted_iota(jnp.int32, (MOBA_BLOCK, MOBA_BLOCK), 1)
    causal_bias = jnp.where(ri >= ci, 0.0, NEG_INF)
    scale = HEAD_DIM ** -0.5
    for n in range(nb):
        q = q_ref[n * MOBA_BLOCK:(n + 1) * MOBA_BLOCK, :]
        gate = _dot_nt(q, km, HIGHEST)
        cnt = jnp.zeros((MOBA_BLOCK, LANES), jnp.int32)
        for i in range(n):
            gi = gate[:, i:i + 1]
            beats = (gi > gate) | ((gi == gate) & (i < lane))
            cnt = cnt + beats.astype(jnp.int32)
        sel_bias = jnp.where((cnt < MOBA_TOPK) & (lane < n), 0.0, NEG_INF)
        nk = (n + 1) * MOBA_BLOCK
        s = _dot_nt(q.astype(BF16), kb[:nk]) * scale
        bias = jnp.concatenate(
            [jnp.broadcast_to(sel_bias[:, j:j + 1], (MOBA_BLOCK, MOBA_BLOCK)) for j in range(n)]
            + [causal_bias], axis=1)
        s = s + bias
        m = jnp.max(s, axis=-1, keepdims=True)
        p = jnp.exp(s - m)
        l = jnp.sum(p, axis=-1, keepdims=True)
        o_ref[n * MOBA_BLOCK:(n + 1) * MOBA_BLOCK, :] = _dot(p.astype(BF16), vb[:nk]) / l


def _moba_prompt(qb, kb, vb, batch, seq):
    spec = pl.BlockSpec((seq, HEAD_DIM), lambda b, h: (b, h))
    return pl.pallas_call(
        functools.partial(_moba_prompt_kernel, seq=seq),
        grid=(batch, H_B),
        in_specs=[spec, spec, spec],
        out_specs=spec,
        out_shape=jax.ShapeDtypeStruct((batch * seq, H_B * HEAD_DIM), F32),
        compiler_params=_params(("parallel", "parallel")),
        name="moba_prompt",
    )(qb, kb, vb)


PAGES_PER_STEP = 16


def _page_sum_kernel(pt_ref, *refs):
    pages, out_ref = refs[:PAGES_PER_STEP], refs[PAGES_PER_STEP]
    for i in range(0, PAGES_PER_STEP, PAGES_PER_BLOCK):
        s = jnp.sum(pages[i][0, 0], axis=0)
        for j in range(1, PAGES_PER_BLOCK):
            s = s + jnp.sum(pages[i + j][0, 0], axis=0)
        r = i // PAGES_PER_BLOCK
        for h in range(H_B):
            out_ref[0, r:r + 1, h * HEAD_DIM:(h + 1) * HEAD_DIM] = s[h:h + 1, :]


def _page_sums(cache, page_table):
    nb, n_pages = page_table.shape
    width = H_B * HEAD_DIM
    steps = n_pages // PAGES_PER_STEP
    blocks_per_step = PAGES_PER_STEP // PAGES_PER_BLOCK

    def page_spec(i):
        return pl.BlockSpec((1, 1, PAGE_SIZE, H_B, HEAD_DIM),
                            lambda b, g, pt: (0, pt[b, g * PAGES_PER_STEP + i], 0, 0, 0))

    return pl.pallas_call(
        _page_sum_kernel,
        grid_spec=pltpu.PrefetchScalarGridSpec(
            num_scalar_prefetch=1, grid=(nb, steps),
            in_specs=[page_spec(i) for i in range(PAGES_PER_STEP)],
            out_specs=pl.BlockSpec((1, blocks_per_step, width), lambda b, g, pt: (b, g, 0))),
        out_shape=jax.ShapeDtypeStruct((nb, n_pages // PAGES_PER_BLOCK, width), F32),
        compiler_params=_params(("parallel", "parallel")),
        name="page_sums",
    )(page_table, *([cache] * PAGES_PER_STEP))


def _moba_select_kernel(q_ref, ps_ref, idx_ref):
    b = pl.program_id(0)
    q = q_ref[pl.ds(b, 1), :]
    km = ps_ref[0] * (1.0 / MOBA_BLOCK)
    n_blocks = km.shape[0]
    row = lax.broadcasted_iota(jnp.int32, (SUBLANES, LANES), 0)
    lane = lax.broadcasted_iota(jnp.int32, (SUBLANES, LANES), 1)
    blk = lax.broadcasted_iota(jnp.int32, (1, n_blocks), 1).astype(F32)
    out = jnp.zeros((SUBLANES, LANES), F32)
    for h in range(H_B):
        lo, hi = h * HEAD_DIM, (h + 1) * HEAD_DIM
        g = _dot_nt(_row0(q[:, lo:hi]), km[:, lo:hi], HIGHEST)[0:1]
        for s in range(MOBA_TOPK):
            m = jnp.max(g, axis=1, keepdims=True)
            idx = jnp.min(jnp.where(g == m, blk, float(n_blocks)), axis=1, keepdims=True)
            out = jnp.where((row == h) & (lane == s), idx, out)
            g = jnp.where(blk == idx, NEG_INF, g)
    idx_ref[0] = out.astype(jnp.int32)


def _moba_select(qb, psums):
    nb, n_blocks, width = psums.shape
    return pl.pallas_call(
        _moba_select_kernel,
        grid=(nb,),
        in_specs=[_const_spec(qb.shape), pl.BlockSpec((1, n_blocks, width), lambda b: (b, 0, 0))],
        out_specs=pl.BlockSpec((1, SUBLANES, LANES), lambda b: (b, 0, 0)),
        out_shape=jax.ShapeDtypeStruct((nb, SUBLANES, LANES), jnp.int32),
        compiler_params=_params(("parallel",)),
        name="moba_select",
    )(qb, psums)


N_SEL_PAGES = MOBA_TOPK * PAGES_PER_BLOCK


def _moba_attend_kernel(idx_ref, pt_ref, *refs):
    kp, vp = refs[:N_SEL_PAGES], refs[N_SEL_PAGES:2 * N_SEL_PAGES]
    q_ref, k_ref, v_ref, o_ref = refs[2 * N_SEL_PAGES:]
    b = pl.program_id(0)
    head = pl.program_id(1)
    q = q_ref[pl.ds(b, 1), :]
    k_new = k_ref[pl.ds(b, 1), :]
    v_new = v_ref[pl.ds(b, 1), :]
    scale = HEAD_DIM ** -0.5
    s_new = jnp.sum(q * k_new, axis=1, keepdims=True) * scale
    for h in range(H_B):
        @pl.when(head == h)
        def _():
            keys = jnp.concatenate([r[0, 0, :, h, :] for r in kp], axis=0)
            vals = jnp.concatenate([r[0, 0, :, h, :] for r in vp], axis=0)
            s = _dot_nt(_row0(q), keys, HIGHEST)[0:1] * scale
            m = jnp.maximum(jnp.max(s, axis=1, keepdims=True), s_new)
            p = jnp.exp(s - m)
            p_new = jnp.exp(s_new - m)
            l = jnp.sum(p, axis=1, keepdims=True) + p_new
            o_ref[0] = (_dot(_row0(p), vals, HIGHEST)[0:1] + p_new * v_new) / l


def _moba_attend(idx_flat, page_table, cache_k, cache_v, qb, kb, vb):
    nb = qb.shape[0]

    def page_spec(s, p):
        def index(b, h, idx, pt):
            blk = idx[(b * H_B + h) * MOBA_TOPK + s]
            return (0, pt[b, blk * PAGES_PER_BLOCK + p], 0, 0, 0)
        return pl.BlockSpec((1, 1, PAGE_SIZE, H_B, HEAD_DIM), index)

    page_specs = [page_spec(s, p) for s in range(MOBA_TOPK) for p in range(PAGES_PER_BLOCK)]
    head = pl.BlockSpec((nb, HEAD_DIM), lambda b, h, idx, pt: (0, h))
    out = pl.pallas_call(
        _moba_attend_kernel,
        grid_spec=pltpu.PrefetchScalarGridSpec(
            num_scalar_prefetch=2, grid=(nb, H_B),
            in_specs=page_specs + page_specs + [head, head, head],
            out_specs=pl.BlockSpec((1, 1, HEAD_DIM), lambda b, h, idx, pt: (b, 0, h))),
        out_shape=jax.ShapeDtypeStruct((nb, 1, H_B * HEAD_DIM), F32),
        compiler_params=_params(("parallel", "parallel")),
        name="moba_attend",
    )(idx_flat, page_table, *([cache_k] * N_SEL_PAGES), *([cache_v] * N_SEL_PAGES), qb, kb, vb)
    return out.reshape(nb, H_B * HEAD_DIM)


def _out_ffn_kernel(x_ref, oa_ref, ob_ref, woa_ref, wob_ref, n2_ref, wg_ref, wu_ref, wd_ref, fn_ref, y_ref):
    x1 = (x_ref[...] + _dot(oa_ref[...].astype(BF16), woa_ref[...])
          + _dot(ob_ref[...].astype(BF16), wob_ref[...]))
    h = _rmsnorm(x1, n2_ref[...]).astype(BF16)
    act = _silu(_dot(h, wg_ref[...])) * _dot(h, wu_ref[...])
    x2 = x1 + _dot(act.astype(BF16), wd_ref[...])
    y_ref[...] = _rmsnorm(x2, fn_ref[...])


def _out_ffn(x, oa, ob, woa, wob, n2, wg, wu, wd, fn, tm):
    t = x.shape[0]
    row = lambda i: (i, 0)
    full = lambda a: _const_spec(a.shape)
    return pl.pallas_call(
        _out_ffn_kernel,
        grid=(t // tm,),
        in_specs=[pl.BlockSpec((tm, D_MODEL), row), pl.BlockSpec((tm, Z_COLS), row),
                  pl.BlockSpec((tm, H_B * HEAD_DIM), row), full(woa), full(wob), full(n2),
                  full(wg), full(wu), full(wd), full(fn)],
        out_specs=pl.BlockSpec((tm, D_MODEL), row),
        out_shape=jax.ShapeDtypeStruct((t, D_MODEL), F32),
        compiler_params=_params(("parallel",)),
        name="out_ffn",
    )(x, oa, ob, woa, wob, n2, wg, wu, wd, fn)


def _rotary_tables(pos):
    half = ROT_DIM // 2
    inv_freq = ROPE_THETA ** (-jnp.arange(half, dtype=F32) / half)
    ang = pos.astype(F32)[:, None] * inv_freq[None, :]
    cos, sin = jnp.cos(ang), jnp.sin(ang)
    n = pos.shape[0]
    rest = HEAD_DIM - ROT_DIM
    cos_t = jnp.concatenate([cos, cos, jnp.ones((n, rest), F32)], axis=1)
    zeros_h = jnp.zeros((n, half), F32)
    sin_a = jnp.concatenate([-sin, zeros_h, jnp.zeros((n, rest), F32)], axis=1)
    sin_b = jnp.concatenate([zeros_h, sin, jnp.zeros((n, rest), F32)], axis=1)
    return cos_t, sin_a, sin_b


def kernel(x_prompt, x_sample, cache_k, cache_v, page_table, state_gdn, state_conv, norm1_w, w_in, conv_w,
           a_log, dt_bias, gdn_norm_w, w_out, norm2_w, w_gate, w_up, w_down, final_norm_w):
    depth = w_in.shape[0]
    assert depth == 1, "single-layer trunk"
    batch, seq, _ = x_prompt.shape
    dec_batch, dec_seq, _ = x_sample.shape
    assert dec_seq == 1
    n_phys = cache_k.shape[1]
    past_len = page_table.shape[1] * PAGE_SIZE

    wi = w_in[0]
    ab0 = GDN_QKV + Z_COLS
    w_all = jnp.concatenate(
        [wi[:, :ab0], wi[:, ab0 + 2 * H_A:], wi[:, ab0:ab0 + 2 * H_A],
         jnp.zeros((D_MODEL, AB_PAD - 2 * H_A), wi.dtype)], axis=1).astype(BF16)
    gpar = jnp.zeros((SUBLANES, LANES), F32)
    gpar = gpar.at[0, :H_A].set(a_log[0].astype(F32)).at[1, :H_A].set(dt_bias[0].astype(F32))
    nw1 = norm1_w[0].reshape(1, D_MODEL)
    cw = conv_w[0]
    gnw = gdn_norm_w[0].reshape(1, HEAD_DIM)
    woa = w_out[0][:Z_COLS].astype(BF16)
    wob = w_out[0][Z_COLS:].astype(BF16)
    n2 = norm2_w[0].reshape(1, D_MODEL)
    wg, wu, wd = w_gate[0].astype(BF16), w_up[0].astype(BF16), w_down[0].astype(BF16)
    fn = final_norm_w.reshape(1, D_MODEL)

    xp = x_prompt.reshape(batch * seq, D_MODEL)
    tabs_p = _rotary_tables(jnp.arange(seq, dtype=jnp.int32))
    qkva_p, z_p, gb_p, qb_p, kb_p, vb_p = _in_proj(xp, nw1, w_all, gpar, *tabs_p, tm=256)
    oa_p, gdn_p = _gdn_prompt(qkva_p, gb_p, z_p, cw, gnw, batch, seq, tl=256)
    ob_p = _moba_prompt(qb_p, kb_p, vb_p, batch, seq)
    y_p = _out_ffn(xp, oa_p, ob_p, woa, wob, n2, wg, wu, wd, fn, tm=256)

    xs = x_sample.reshape(dec_batch, D_MODEL)
    tabs_s = _rotary_tables(jnp.full((dec_batch,), past_len, jnp.int32))
    qkva_s, z_s, gb_s, qb_s, kb_s, vb_s = _in_proj(xs, nw1, w_all, gpar, *tabs_s, tm=dec_batch)
    sc = state_conv[0]
    oa_s, gdn_s = _gdn_sample(qkva_s, sc[:, 0], sc[:, 1], sc[:, 2], gb_s, z_s, cw, gnw, state_gdn[0])
    psums = _page_sums(cache_k, page_table)
    idx = _moba_select(qb_s, psums)
    idx_flat = idx[:, :H_B, :MOBA_TOPK].reshape(-1)
    ob_s = _moba_attend(idx_flat, page_table, cache_k, cache_v, qb_s, kb_s, vb_s)
    y_s = _out_ffn(xs, oa_s.reshape(dec_batch, Z_COLS), ob_s, woa, wob, n2, wg, wu, wd, fn, tm=dec_batch)

    conv_p = qkva_p.reshape(batch, seq, GDN_QKV)[:, seq - (CONV_W - 1):]
    conv_s = jnp.concatenate([sc[:, 1:], qkva_s[:, None, :]], axis=1)
    return (y_p.reshape(batch, seq, D_MODEL), y_s.reshape(dec_batch, 1, D_MODEL),
            kb_p.reshape(1, batch, seq, H_B, HEAD_DIM), vb_p.reshape(1, batch, seq, H_B, HEAD_DIM),
            kb_s.reshape(1, dec_batch, 1, H_B, HEAD_DIM), vb_s.reshape(1, dec_batch, 1, H_B, HEAD_DIM),
            gdn_p[None], gdn_s[None], conv_p[None], conv_s[None])
```

```python
import functools

import jax
import jax.numpy as jnp
from jax import lax
from jax.experimental import pallas as pl
from jax.experimental.pallas import tpu as pltpu

D_MODEL = 1024
HEAD_DIM = 128
H_A = 4
H_B = 4
GDN_QKV = 3 * H_A * HEAD_DIM
MOBA_QKV = 3 * H_B * HEAD_DIM
Z_COLS = H_A * HEAD_DIM
CONV_W = 4
CHUNK = 64
MOBA_BLOCK = 256
MOBA_TOPK = 3
PAGE_SIZE = 128
PAGES_PER_BLOCK = MOBA_BLOCK // PAGE_SIZE
ROT_DIM = HEAD_DIM // 4
ROPE_THETA = 500000.0
EPS = 1e-6

LANES = 128
SUBLANES = 8
AB_PAD = LANES
PROJ_COLS = GDN_QKV + Z_COLS + MOBA_QKV + AB_PAD
VMEM_LIMIT = 56 * 1024 * 1024

F32 = jnp.float32
BF16 = jnp.bfloat16
HIGHEST = lax.Precision.HIGHEST
NEG_INF = float("-inf")


def _dot(a, b, precision=None):
    return jnp.dot(a, b, precision=precision, preferred_element_type=F32)


def _dot_nt(a, b, precision=None):
    return lax.dot_general(a, b, (((1,), (1,)), ((), ())), precision=precision,
                           preferred_element_type=F32)


def _rmsnorm(x, w):
    return x * lax.rsqrt(jnp.mean(x * x, axis=-1, keepdims=True) + EPS) * w


def _l2norm(x):
    return x * lax.rsqrt(jnp.sum(x * x, axis=-1, keepdims=True) + EPS)


def _silu(x):
    return x * jax.nn.sigmoid(x)


def _const_spec(shape):
    return pl.BlockSpec(shape, lambda *_: (0,) * len(shape), pipeline_mode=pl.Buffered(1))


def _params(semantics):
    return pltpu.CompilerParams(dimension_semantics=semantics, vmem_limit_bytes=VMEM_LIMIT)


def _in_proj_kernel(x_ref, nw_ref, w_ref, gpar_ref, cos_ref, sa_ref, sb_ref,
                    qkva_ref, z_ref, gb_ref, qb_ref, kb_ref, vb_ref, k5_ref, v5_ref):
    xn = _rmsnorm(x_ref[...], nw_ref[...])
    proj = _dot(xn.astype(BF16), w_ref[...])
    qkva_ref[...] = proj[:, :GDN_QKV]
    z_ref[...] = proj[:, GDN_QKV:GDN_QKV + Z_COLS]
    b0 = GDN_QKV + Z_COLS
    ab = proj[:, b0 + MOBA_QKV:]
    lane = lax.broadcasted_iota(jnp.int32, ab.shape, 1)
    g = -jnp.exp(gpar_ref[0:1, :]) * jax.nn.softplus(ab + gpar_ref[1:2, :])
    gb_ref[...] = jnp.where(lane < H_A, g, jax.nn.sigmoid(ab))
    cos, sa, sb = cos_ref[...], sa_ref[...], sb_ref[...]
    half = ROT_DIM // 2

    def rot(t):
        return (t * cos + pltpu.roll(t, HEAD_DIM - half, 1) * sa + pltpu.roll(t, half, 1) * sb)

    for h in range(H_B):
        lo, hi = h * HEAD_DIM, (h + 1) * HEAD_DIM
        qb_ref[:, lo:hi] = rot(proj[:, b0 + lo:b0 + hi])
        kh = rot(proj[:, b0 + H_B * HEAD_DIM + lo:b0 + H_B * HEAD_DIM + hi])
        vh = proj[:, b0 + 2 * H_B * HEAD_DIM + lo:b0 + 2 * H_B * HEAD_DIM + hi]
        kb_ref[:, lo:hi] = kh
        vb_ref[:, lo:hi] = vh
        k5_ref[0, 0, :, h, :] = kh
        v5_ref[0, 0, :, h, :] = vh


def _in_proj(x, nw, w_all, gpar, cos, sa, sb, batch, seq, tm):
    t = batch * seq
    n_t = seq // tm
    row = lambda i: (i, 0)
    tab = lambda i: (i % n_t, 0)
    kv = lambda i: (0, i // n_t, i % n_t, 0, 0)
    widths = (GDN_QKV, Z_COLS, AB_PAD, H_B * HEAD_DIM, H_B * HEAD_DIM, H_B * HEAD_DIM)
    kv_shape = jax.ShapeDtypeStruct((1, batch, seq, H_B, HEAD_DIM), F32)
    kv_spec = pl.BlockSpec((1, 1, tm, H_B, HEAD_DIM), kv)
    return pl.pallas_call(
        _in_proj_kernel,
        grid=(t // tm,),
        in_specs=[pl.BlockSpec((tm, D_MODEL), row), _const_spec((1, D_MODEL)),
                  _const_spec((D_MODEL, PROJ_COLS)), _const_spec((SUBLANES, LANES)),
                  pl.BlockSpec((tm, LANES), tab), pl.BlockSpec((tm, LANES), tab),
                  pl.BlockSpec((tm, LANES), tab)],
        out_specs=[pl.BlockSpec((tm, w), row) for w in widths] + [kv_spec, kv_spec],
        out_shape=[jax.ShapeDtypeStruct((t, w), F32) for w in widths] + [kv_shape, kv_shape],
        compiler_params=_params(("parallel",)),
        name="in_proj",
    )(x, nw, w_all, gpar, cos, sa, sb)


def _gdn_head_norms(y):
    parts = []
    for h in range(H_A):
        parts.append(_l2norm(y[:, h * HEAD_DIM:(h + 1) * HEAD_DIM]) * (HEAD_DIM ** -0.5))
    for h in range(H_A, 2 * H_A):
        parts.append(_l2norm(y[:, h * HEAD_DIM:(h + 1) * HEAD_DIM]))
    return parts


def _gdn_out_norm(o, gnw, z):
    return _rmsnorm(o, gnw) * _silu(z)


def _gdn_prompt_kernel(qkva_ref, gb_ref, z_ref, cw_ref, gnw_ref, o_ref, sfin_ref,
                       xbuf, u_s, wq_s, kdt_s, qk_s, egl_s, s_s, *, tl, n_t):
    t = pl.program_id(1)
    n_c = tl // CHUNK

    @pl.when(t == 0)
    def _():
        xbuf[0:SUBLANES, :] = jnp.zeros((SUBLANES, GDN_QKV), F32)
        s_s[...] = jnp.zeros_like(s_s)

    xbuf[SUBLANES:SUBLANES + tl, :] = qkva_ref[...]
    cw = cw_ref[...]
    y = xbuf[pl.ds(SUBLANES - 3, tl), :] * cw[0:1]
    for j in range(1, CONV_W):
        y = y + xbuf[pl.ds(SUBLANES - 3 + j, tl), :] * cw[j:j + 1]
    y = _silu(y)
    xbuf[0:SUBLANES, :] = xbuf[tl:tl + SUBLANES, :]
    qk_parts = _gdn_head_norms(y)

    ri = lax.broadcasted_iota(jnp.int32, (tl, tl), 0)
    ci = lax.broadcasted_iota(jnp.int32, (tl, tl), 1)
    same = (ri // CHUNK) == (ci // CHUNK)
    tri = same & (ri >= ci)
    strict = same & (ri > ci)
    eye = (ri == ci).astype(F32)
    gb = gb_ref[...]
    sums = _dot(jnp.concatenate([tri.astype(F32), same.astype(F32)], axis=0), gb, HIGHEST)
    gc_all, gl_all = sums[:tl], sums[tl:]
    gc_t = gc_all.T

    for h in range(H_A):
        lo, hi = h * HEAD_DIM, (h + 1) * HEAD_DIM
        q, k = qk_parts[h], qk_parts[H_A + h]
        v = y[:, 2 * H_A * HEAD_DIM + lo:2 * H_A * HEAD_DIM + hi]
        beta = gb[:, H_A + h:H_A + h + 1]
        gcol = gc_all[:, h:h + 1]
        glcol = gl_all[:, h:h + 1]
        decay = jnp.where(tri, jnp.exp(gcol - gc_t[h:h + 1, :]), 0.0)
        egc = jnp.exp(gcol)
        kbeta = k * beta
        qkk = _dot_nt(jnp.concatenate([q, kbeta], axis=0).astype(BF16), k.astype(BF16))
        qk = qkk[:tl] * decay
        a = jnp.where(strict, qkk[tl:] * decay, 0.0)
        p = -a
        tinv = eye + p
        for _ in range(5):
            pb = p.astype(BF16)
            p = _dot(pb, pb)
            tinv = tinv + _dot(tinv.astype(BF16), p.astype(BF16))
        uw = _dot(tinv.astype(BF16), jnp.concatenate([v * beta, kbeta * egc], axis=1).astype(BF16))
        u_s[:, lo:hi] = uw[:, :HEAD_DIM]
        w = uw[:, HEAD_DIM:].astype(BF16)
        qd = (q * egc).astype(BF16)
        kd = k * jnp.exp(glcol - gcol)
        egl = jnp.exp(glcol)
        for c in range(n_c):
            r0, r1 = c * CHUNK, (c + 1) * CHUNK
            wq_s[h, c, 0:CHUNK, :] = w[r0:r1]
            wq_s[h, c, CHUNK:2 * CHUNK, :] = qd[r0:r1]
            kdt_s[h, c] = kd[r0:r1].T.astype(BF16)
            qk_s[h, c] = qk[r0:r1, r0:r1].astype(BF16)
            egl_s[h, c] = jnp.broadcast_to(egl[r0:r0 + 1, :], (SUBLANES, HEAD_DIM))

    gnw = gnw_ref[...]
    for c in range(n_c):
        r0, r1 = c * CHUNK, (c + 1) * CHUNK
        zt = z_ref[r0:r1, :]
        for h in range(H_A):
            lo, hi = h * HEAD_DIM, (h + 1) * HEAD_DIM
            s = s_s[h]
            ws = _dot(wq_s[h, c], s.astype(BF16))
            v_new = u_s[r0:r1, lo:hi] - ws[:CHUNK]
            vnb = v_new.astype(BF16)
            o = ws[CHUNK:] + _dot(qk_s[h, c], vnb)
            s_s[h] = s * egl_s[h, c][0:1] + _dot(kdt_s[h, c], vnb)
            o_ref[r0:r1, lo:hi] = _gdn_out_norm(o, gnw, zt[:, lo:hi])

    @pl.when(t == n_t - 1)
    def _():
        sfin_ref[0] = s_s[...]


def _gdn_prompt(qkva, gb, z, cw, gnw, batch, seq, tl):
    n_t = seq // tl
    n_c = tl // CHUNK
    row = lambda b, t: (b * n_t + t, 0)
    return pl.pallas_call(
        functools.partial(_gdn_prompt_kernel, tl=tl, n_t=n_t),
        grid=(batch, n_t),
        in_specs=[pl.BlockSpec((tl, GDN_QKV), row), pl.BlockSpec((tl, AB_PAD), row),
                  pl.BlockSpec((tl, Z_COLS), row), _const_spec((CONV_W, GDN_QKV)),
                  _const_spec((1, HEAD_DIM))],
        out_specs=[pl.BlockSpec((tl, Z_COLS), row),
                   pl.BlockSpec((1, H_A, HEAD_DIM, HEAD_DIM), lambda b, t: (b, 0, 0, 0))],
        out_shape=[jax.ShapeDtypeStruct((batch * seq, Z_COLS), F32),
                   jax.ShapeDtypeStruct((batch, H_A, HEAD_DIM, HEAD_DIM), F32)],
        scratch_shapes=[pltpu.VMEM((tl + SUBLANES, GDN_QKV), F32),
                        pltpu.VMEM((tl, Z_COLS), F32),
                        pltpu.VMEM((H_A, n_c, 2 * CHUNK, HEAD_DIM), BF16),
                        pltpu.VMEM((H_A, n_c, HEAD_DIM, CHUNK), BF16),
                        pltpu.VMEM((H_A, n_c, CHUNK, CHUNK), BF16),
                        pltpu.VMEM((H_A, n_c, SUBLANES, HEAD_DIM), F32),
                        pltpu.VMEM((H_A, HEAD_DIM, HEAD_DIM), F32)],
        compiler_params=_params(("parallel", "arbitrary")),
        name="gdn_prompt",
    )(qkva, gb, z, cw, gnw)


def _row0(x, rows=SUBLANES):
    r = lax.broadcasted_iota(jnp.int32, (rows, x.shape[1]), 0)
    return jnp.where(r == 0, jnp.broadcast_to(x, (rows, x.shape[1])), 0.0)


def _gdn_sample_kernel(xn_ref, p0_ref, p1_ref, p2_ref, gb_ref, z_ref, cw_ref, gnw_ref, s0_ref,
                       o_ref, sout_ref):
    b = pl.program_id(0)
    row = lambda r: r[pl.ds(b, 1), :]
    cw = cw_ref[...]
    y = row(p0_ref) * cw[0:1]
    y = y + row(p1_ref) * cw[1:2]
    y = y + row(p2_ref) * cw[2:3]
    y = y + row(xn_ref) * cw[3:4]
    y = _silu(y)
    qk = _gdn_head_norms(y)
    gbr = row(gb_ref)
    zr = row(z_ref)
    gnw = gnw_ref[...]
    ri = lax.broadcasted_iota(jnp.int32, (HEAD_DIM, HEAD_DIM), 0)
    ci = lax.broadcasted_iota(jnp.int32, (HEAD_DIM, HEAD_DIM), 1)
    eye = (ri == ci).astype(F32)
    for h in range(H_A):
        lo, hi = h * HEAD_DIM, (h + 1) * HEAD_DIM
        q, k = qk[h], qk[H_A + h]
        v = y[:, 2 * H_A * HEAD_DIM + lo:2 * H_A * HEAD_DIM + hi]
        eg = jnp.exp(gbr[:, h:h + 1])
        beta = gbr[:, H_A + h:H_A + h + 1]
        s0 = s0_ref[0, h]
        k8 = _row0(k)
        v_new = beta * (v - eg * _dot(k8, s0, HIGHEST)[0:1])
        k_col = _dot_nt(eye, k8, HIGHEST)[:, 0:1]
        s_new = s0 * eg + k_col * v_new
        o = _dot(_row0(q), s_new, HIGHEST)[0:1]
        o_ref[0, :, lo:hi] = _gdn_out_norm(o, gnw, zr[:, lo:hi])
        sout_ref[0, h] = s_new


def _gdn_sample(xn, p0, p1, p2, gb, z, cw, gnw, s0):
    nb = xn.shape[0]
    full = lambda a: _const_spec(a.shape)
    st = pl.BlockSpec((1, H_A, HEAD_DIM, HEAD_DIM), lambda b: (b, 0, 0, 0))
    return pl.pallas_call(
        _gdn_sample_kernel,
        grid=(nb,),
        in_specs=[full(xn), full(p0), full(p1), full(p2), full(gb), full(z), full(cw), full(gnw), st],
        out_specs=[pl.BlockSpec((1, 1, Z_COLS), lambda b: (b, 0, 0)), st],
        out_shape=[jax.ShapeDtypeStruct((nb, 1, Z_COLS), F32),
                   jax.ShapeDtypeStruct(s0.shape, F32)],
        compiler_params=_params(("parallel",)),
        name="gdn_sample",
    )(xn, p0, p1, p2, gb, z, cw, gnw, s0)


def _moba_prompt_kernel(q_ref, k_ref, v_ref, o_ref, *, seq):
    nb = seq // MOBA_BLOCK
    k = k_ref[...]
    kb = k.astype(BF16)
    vb = v_ref[...].astype(BF16)
    means = [jnp.sum(k[j * MOBA_BLOCK:(j + 1) * MOBA_BLOCK], axis=0, keepdims=True) * (1.0 / MOBA_BLOCK)
             for j in range(nb)]
    km = jnp.concatenate(means + [jnp.zeros((LANES - nb, HEAD_DIM), F32)], axis=0)
    lane = lax.broadcasted_iota(jnp.int32, (MOBA_BLOCK, LANES), 1)
    ri = lax.broadcasted_iota(jnp.int32, (MOBA_BLOCK, MOBA_BLOCK), 0)
    ci = lax.broadcasted_iota(jnp.int32, (MOBA_BLOCK, MOBA_BLOCK), 1)
    causal_bias = jnp.where(ri >= ci, 0.0, NEG_INF)
    scale = HEAD_DIM ** -0.5
    for n in range(nb):
        q = q_ref[n * MOBA_BLOCK:(n + 1) * MOBA_BLOCK, :]
        gate = _dot_nt(q, km, HIGHEST)
        cnt = jnp.zeros((MOBA_BLOCK, LANES), jnp.int32)
        for i in range(n):
            gi = gate[:, i:i + 1]
            beats = (gi > gate) | ((gi == gate) & (i < lane))
            cnt = cnt + beats.astype(jnp.int32)
        sel_bias = jnp.where((cnt < MOBA_TOPK) & (lane < n), 0.0, NEG_INF)
        nk = (n + 1) * MOBA_BLOCK
        s = _dot_nt(q.astype(BF16), kb[:nk]) * scale
        bias = jnp.concatenate(
            [jnp.broadcast_to(sel_bias[:, j:j + 1], (MOBA_BLOCK, MOBA_BLOCK)) for j in range(n)]
            + [causal_bias], axis=1)
        s = s + bias
        m = jnp.max(s, axis=-1, keepdims=True)
        p = jnp.exp(s - m)
        l = jnp.sum(p, axis=-1, keepdims=True)
        o_ref[n * MOBA_BLOCK:(n + 1) * MOBA_BLOCK, :] = _dot(p.astype(BF16), vb[:nk]) / l


def _moba_prompt(qb, kb, vb, batch, seq):
    spec = pl.BlockSpec((seq, HEAD_DIM), lambda b, h: (b, h))
    return pl.pallas_call(
        functools.partial(_moba_prompt_kernel, seq=seq),
        grid=(batch, H_B),
        in_specs=[spec, spec, spec],
        out_specs=spec,
        out_shape=jax.ShapeDtypeStruct((batch * seq, H_B * HEAD_DIM), F32),
        compiler_params=_params(("parallel", "parallel")),
        name="moba_prompt",
    )(qb, kb, vb)


PAGES_PER_STEP = 32


def _page_sum_kernel(pt_ref, *refs):
    pages, out_ref = refs[:PAGES_PER_STEP], refs[PAGES_PER_STEP]
    for i in range(0, PAGES_PER_STEP, PAGES_PER_BLOCK):
        s = jnp.sum(pages[i][0, 0], axis=0)
        for j in range(1, PAGES_PER_BLOCK):
            s = s + jnp.sum(pages[i + j][0, 0], axis=0)
        r = i // PAGES_PER_BLOCK
        for h in range(H_B):
            out_ref[0, r:r + 1, h * HEAD_DIM:(h + 1) * HEAD_DIM] = s[h:h + 1, :]


def _page_sums(cache, page_table):
    nb, n_pages = page_table.shape
    width = H_B * HEAD_DIM
    steps = n_pages // PAGES_PER_STEP
    blocks_per_step = PAGES_PER_STEP // PAGES_PER_BLOCK

    def page_spec(i):
        return pl.BlockSpec((1, 1, PAGE_SIZE, H_B, HEAD_DIM),
                            lambda b, g, pt: (0, pt[b, g * PAGES_PER_STEP + i], 0, 0, 0))

    return pl.pallas_call(
        _page_sum_kernel,
        grid_spec=pltpu.PrefetchScalarGridSpec(
            num_scalar_prefetch=1, grid=(nb, steps),
            in_specs=[page_spec(i) for i in range(PAGES_PER_STEP)],
            out_specs=pl.BlockSpec((1, blocks_per_step, width), lambda b, g, pt: (b, g, 0))),
        out_shape=jax.ShapeDtypeStruct((nb, n_pages // PAGES_PER_BLOCK, width), F32),
        compiler_params=_params(("parallel", "parallel")),
        name="page_sums",
    )(page_table, *([cache] * PAGES_PER_STEP))


def _moba_select_kernel(q_ref, ps_ref, idx_ref):
    b = pl.program_id(0)
    q = q_ref[pl.ds(b, 1), :]
    km = ps_ref[0] * (1.0 / MOBA_BLOCK)
    n_blocks = km.shape[0]
    row = lax.broadcasted_iota(jnp.int32, (SUBLANES, LANES), 0)
    lane = lax.broadcasted_iota(jnp.int32, (SUBLANES, LANES), 1)
    blk = lax.broadcasted_iota(jnp.int32, (1, n_blocks), 1).astype(F32)
    out = jnp.zeros((SUBLANES, LANES), F32)
    for h in range(H_B):
        lo, hi = h * HEAD_DIM, (h + 1) * HEAD_DIM
        g = _dot_nt(_row0(q[:, lo:hi]), km[:, lo:hi], HIGHEST)[0:1]
        for s in range(MOBA_TOPK):
            m = jnp.max(g, axis=1, keepdims=True)
            idx = jnp.min(jnp.where(g == m, blk, float(n_blocks)), axis=1, keepdims=True)
            out = jnp.where((row == h) & (lane == s), idx, out)
            g = jnp.where(blk == idx, NEG_INF, g)
    idx_ref[0] = out.astype(jnp.int32)


def _moba_select(qb, psums):
    nb, n_blocks, width = psums.shape
    return pl.pallas_call(
        _moba_select_kernel,
        grid=(nb,),
        in_specs=[_const_spec(qb.shape), pl.BlockSpec((1, n_blocks, width), lambda b: (b, 0, 0))],
        out_specs=pl.BlockSpec((1, SUBLANES, LANES), lambda b: (b, 0, 0)),
        out_shape=jax.ShapeDtypeStruct((nb, SUBLANES, LANES), jnp.int32),
        compiler_params=_params(("parallel",)),
        name="moba_select",
    )(qb, psums)


N_SEL_PAGES = MOBA_TOPK * PAGES_PER_BLOCK


def _moba_attend_kernel(idx_ref, pt_ref, *refs):
    n_pages = H_B * N_SEL_PAGES
    kp, vp = refs[:n_pages], refs[n_pages:2 * n_pages]
    q_ref, k_ref, v_ref, o_ref = refs[2 * n_pages:]
    b = pl.program_id(0)
    q = q_ref[pl.ds(b, 1), :]
    k_new = k_ref[pl.ds(b, 1), :]
    v_new = v_ref[pl.ds(b, 1), :]
    scale = HEAD_DIM ** -0.5
    for h in range(H_B):
        lo, hi = h * HEAD_DIM, (h + 1) * HEAD_DIM
        pages = range(h * N_SEL_PAGES, (h + 1) * N_SEL_PAGES)
        keys = jnp.concatenate([kp[i][0, 0, :, h, :] for i in pages], axis=0)
        vals = jnp.concatenate([vp[i][0, 0, :, h, :] for i in pages], axis=0)
        qh = q[:, lo:hi]
        s = _dot_nt(_row0(qh, 2 * SUBLANES).astype(BF16), keys.astype(BF16))[0:1] * scale
        s_new = jnp.sum(qh * k_new[:, lo:hi], axis=1, keepdims=True) * scale
        m = jnp.maximum(jnp.max(s, axis=1, keepdims=True), s_new)
        p = jnp.exp(s - m)
        p_new = jnp.exp(s_new - m)
        l = jnp.sum(p, axis=1, keepdims=True) + p_new
        pv = _dot(_row0(p, 2 * SUBLANES).astype(BF16), vals.astype(BF16))[0:1]
        o_ref[0, :, lo:hi] = (pv + p_new * v_new[:, lo:hi]) / l


def _moba_attend(idx_flat, page_table, cache_k, cache_v, qb, kb, vb):
    nb = qb.shape[0]

    def page_spec(h, s, p):
        def index(b, idx, pt):
            blk = idx[(b * H_B + h) * MOBA_TOPK + s]
            return (0, pt[b, blk * PAGES_PER_BLOCK + p], 0, 0, 0)
        return pl.BlockSpec((1, 1, PAGE_SIZE, H_B, HEAD_DIM), index)

    page_specs = [page_spec(h, s, p) for h in range(H_B) for s in range(MOBA_TOPK)
                  for p in range(PAGES_PER_BLOCK)]
    n_pages = len(page_specs)
    rows = pl.BlockSpec((nb, H_B * HEAD_DIM), lambda b, idx, pt: (0, 0))
    out = pl.pallas_call(
        _moba_attend_kernel,
        grid_spec=pltpu.PrefetchScalarGridSpec(
            num_scalar_prefetch=2, grid=(nb,),
            in_specs=page_specs + page_specs + [rows, rows, rows],
            out_specs=pl.BlockSpec((1, 1, H_B * HEAD_DIM), lambda b, idx, pt: (b, 0, 0))),
        out_shape=jax.ShapeDtypeStruct((nb, 1, H_B * HEAD_DIM), F32),
        compiler_params=_params(("parallel",)),
        name="moba_attend",
    )(idx_flat, page_table, *([cache_k] * n_pages), *([cache_v] * n_pages), qb, kb, vb)
    return out.reshape(nb, H_B * HEAD_DIM)


def _out_ffn_kernel(x_ref, oa_ref, ob_ref, woa_ref, wob_ref, n2_ref, wg_ref, wu_ref, wd_ref, fn_ref, y_ref):
    x1 = (x_ref[...] + _dot(oa_ref[...].astype(BF16), woa_ref[...])
          + _dot(ob_ref[...].astype(BF16), wob_ref[...]))
    h = _rmsnorm(x1, n2_ref[...]).astype(BF16)
    act = _silu(_dot(h, wg_ref[...])) * _dot(h, wu_ref[...])
    x2 = x1 + _dot(act.astype(BF16), wd_ref[...])
    y_ref[...] = _rmsnorm(x2, fn_ref[...])


def _out_ffn(x, oa, ob, woa, wob, n2, wg, wu, wd, fn, tm):
    t = x.shape[0]
    row = lambda i: (i, 0)
    full = lambda a: _const_spec(a.shape)
    return pl.pallas_call(
        _out_ffn_kernel,
        grid=(t // tm,),
        in_specs=[pl.BlockSpec((tm, D_MODEL), row), pl.BlockSpec((tm, Z_COLS), row),
                  pl.BlockSpec((tm, H_B * HEAD_DIM), row), full(woa), full(wob), full(n2),
                  full(wg), full(wu), full(wd), full(fn)],
        out_specs=pl.BlockSpec((tm, D_MODEL), row),
        out_shape=jax.ShapeDtypeStruct((t, D_MODEL), F32),
        compiler_params=_params(("parallel",)),
        name="out_ffn",
    )(x, oa, ob, woa, wob, n2, wg, wu, wd, fn)


def _rotary_tables(pos):
    half = ROT_DIM // 2
    inv_freq = ROPE_THETA ** (-jnp.arange(half, dtype=F32) / half)
    ang = pos.astype(F32)[:, None] * inv_freq[None, :]
    cos, sin = jnp.cos(ang), jnp.sin(ang)
    n = pos.shape[0]
    rest = HEAD_DIM - ROT_DIM
    cos_t = jnp.concatenate([cos, cos, jnp.ones((n, rest), F32)], axis=1)
    zeros_h = jnp.zeros((n, half), F32)
    sin_a = jnp.concatenate([-sin, zeros_h, jnp.zeros((n, rest), F32)], axis=1)
    sin_b = jnp.concatenate([zeros_h, sin, jnp.zeros((n, rest), F32)], axis=1)
    return cos_t, sin_a, sin_b


def kernel(x_prompt, x_sample, cache_k, cache_v, page_table, state_gdn, state_conv, norm1_w, w_in, conv_w,
           a_log, dt_bias, gdn_norm_w, w_out, norm2_w, w_gate, w_up, w_down, final_norm_w):
    depth = w_in.shape[0]
    assert depth == 1, "single-layer trunk"
    batch, seq, _ = x_prompt.shape
    dec_batch, dec_seq, _ = x_sample.shape
    assert dec_seq == 1
    n_phys = cache_k.shape[1]
    past_len = page_table.shape[1] * PAGE_SIZE

    wi = w_in[0]
    ab0 = GDN_QKV + Z_COLS
    w_all = jnp.concatenate(
        [wi[:, :ab0], wi[:, ab0 + 2 * H_A:], wi[:, ab0:ab0 + 2 * H_A],
         jnp.zeros((D_MODEL, AB_PAD - 2 * H_A), wi.dtype)], axis=1).astype(BF16)
    gpar = jnp.zeros((SUBLANES, LANES), F32)
    gpar = gpar.at[0, :H_A].set(a_log[0].astype(F32)).at[1, :H_A].set(dt_bias[0].astype(F32))
    nw1 = norm1_w[0].reshape(1, D_MODEL)
    cw = conv_w[0]
    gnw = gdn_norm_w[0].reshape(1, HEAD_DIM)
    woa = w_out[0][:Z_COLS].astype(BF16)
    wob = w_out[0][Z_COLS:].astype(BF16)
    n2 = norm2_w[0].reshape(1, D_MODEL)
    wg, wu, wd = w_gate[0].astype(BF16), w_up[0].astype(BF16), w_down[0].astype(BF16)
    fn = final_norm_w.reshape(1, D_MODEL)

    xp = x_prompt.reshape(batch * seq, D_MODEL)
    tabs_p = _rotary_tables(jnp.arange(seq, dtype=jnp.int32))
    qkva_p, z_p, gb_p, qb_p, kb_p, vb_p, k5_p, v5_p = _in_proj(xp, nw1, w_all, gpar, *tabs_p,
                                                               batch=batch, seq=seq, tm=512)
    oa_p, gdn_p = _gdn_prompt(qkva_p, gb_p, z_p, cw, gnw, batch, seq, tl=256)
    ob_p = _moba_prompt(qb_p, kb_p, vb_p, batch, seq)
    y_p = _out_ffn(xp, oa_p, ob_p, woa, wob, n2, wg, wu, wd, fn, tm=512)

    xs = x_sample.reshape(dec_batch, D_MODEL)
    tabs_s = _rotary_tables(jnp.full((dec_batch,), past_len, jnp.int32))
    qkva_s, z_s, gb_s, qb_s, kb_s, vb_s, k5_s, v5_s = _in_proj(xs, nw1, w_all, gpar, *tabs_s,
                                                               batch=1, seq=dec_batch, tm=dec_batch)
    sc = state_conv[0]
    oa_s, gdn_s = _gdn_sample(qkva_s, sc[:, 0], sc[:, 1], sc[:, 2], gb_s, z_s, cw, gnw, state_gdn[0])
    psums = _page_sums(cache_k, page_table)
    idx = _moba_select(qb_s, psums)
    idx_flat = idx[:, :H_B, :MOBA_TOPK].reshape(-1)
    ob_s = _moba_attend(idx_flat, page_table, cache_k, cache_v, qb_s, kb_s, vb_s)
    y_s = _out_ffn(xs, oa_s.reshape(dec_batch, Z_COLS), ob_s, woa, wob, n2, wg, wu, wd, fn, tm=dec_batch)

    conv_p = qkva_p.reshape(batch, seq, GDN_QKV)[:, seq - (CONV_W - 1):]
    conv_s = jnp.concatenate([sc[:, 1:], qkva_s[:, None, :]], axis=1)
    return (y_p.reshape(batch, seq, D_MODEL), y_s.reshape(dec_batch, 1, D_MODEL),
            k5_p, v5_p,
            k5_s.reshape(1, dec_batch, 1, H_B, HEAD_DIM), v5_s.reshape(1, dec_batch, 1, H_B, HEAD_DIM),
            gdn_p[None], gdn_s[None], conv_p[None], conv_s[None])
```

```python
import functools

import jax
import jax.numpy as jnp
from jax import lax
from jax.experimental import pallas as pl
from jax.experimental.pallas import tpu as pltpu

D_MODEL = 1024
HEAD_DIM = 128
H_A = 4
H_B = 4
GDN_QKV = 3 * H_A * HEAD_DIM
MOBA_QKV = 3 * H_B * HEAD_DIM
Z_COLS = H_A * HEAD_DIM
CONV_W = 4
CHUNK = 64
SOLVE_ROWS = 128
MOBA_BLOCK = 256
MOBA_TOPK = 3
PAGE_SIZE = 128
PAGES_PER_BLOCK = MOBA_BLOCK // PAGE_SIZE
ROT_DIM = HEAD_DIM // 4
ROPE_THETA = 500000.0
EPS = 1e-6

LANES = 128
SUBLANES = 8
AB_PAD = LANES
PROJ_COLS = GDN_QKV + Z_COLS + MOBA_QKV + AB_PAD
VMEM_LIMIT = 56 * 1024 * 1024

F32 = jnp.float32
BF16 = jnp.bfloat16
HIGHEST = lax.Precision.HIGHEST
NEG_INF = float("-inf")


def _dot(a, b, precision=None):
    return jnp.dot(a, b, precision=precision, preferred_element_type=F32)


def _dot_nt(a, b, precision=None):
    return lax.dot_general(a, b, (((1,), (1,)), ((), ())), precision=precision,
                           preferred_element_type=F32)


def _rmsnorm(x, w):
    return x * lax.rsqrt(jnp.mean(x * x, axis=-1, keepdims=True) + EPS) * w


def _l2norm(x):
    return x * lax.rsqrt(jnp.sum(x * x, axis=-1, keepdims=True) + EPS)


def _silu(x):
    return x * jax.nn.sigmoid(x)


def _const_spec(shape):
    return pl.BlockSpec(shape, lambda *_: (0,) * len(shape), pipeline_mode=pl.Buffered(1))


def _params(semantics):
    return pltpu.CompilerParams(dimension_semantics=semantics, vmem_limit_bytes=VMEM_LIMIT)


def _in_proj_kernel(x_ref, nw_ref, w_ref, gpar_ref, cos_ref, sa_ref, sb_ref,
                    qkva_ref, z_ref, gb_ref, qb_ref, kb_ref, vb_ref, k5_ref, v5_ref):
    xn = _rmsnorm(x_ref[...], nw_ref[...])
    proj = _dot(xn.astype(BF16), w_ref[...])
    qkva_ref[...] = proj[:, :GDN_QKV]
    z_ref[...] = proj[:, GDN_QKV:GDN_QKV + Z_COLS]
    b0 = GDN_QKV + Z_COLS
    ab = proj[:, b0 + MOBA_QKV:]
    lane = lax.broadcasted_iota(jnp.int32, ab.shape, 1)
    g = -jnp.exp(gpar_ref[0:1, :]) * jax.nn.softplus(ab + gpar_ref[1:2, :])
    gb_ref[...] = jnp.where(lane < H_A, g, jax.nn.sigmoid(ab))
    cos, sa, sb = cos_ref[...], sa_ref[...], sb_ref[...]
    half = ROT_DIM // 2

    def rot(t):
        return (t * cos + pltpu.roll(t, HEAD_DIM - half, 1) * sa + pltpu.roll(t, half, 1) * sb)

    for h in range(H_B):
        lo, hi = h * HEAD_DIM, (h + 1) * HEAD_DIM
        qb_ref[:, lo:hi] = rot(proj[:, b0 + lo:b0 + hi])
        kh = rot(proj[:, b0 + H_B * HEAD_DIM + lo:b0 + H_B * HEAD_DIM + hi])
        vh = proj[:, b0 + 2 * H_B * HEAD_DIM + lo:b0 + 2 * H_B * HEAD_DIM + hi]
        kb_ref[:, lo:hi] = kh
        vb_ref[:, lo:hi] = vh
        k5_ref[0, 0, :, h, :] = kh
        v5_ref[0, 0, :, h, :] = vh


def _in_proj(x, nw, w_all, gpar, cos, sa, sb, batch, seq, tm):
    t = batch * seq
    n_t = seq // tm
    row = lambda i: (i, 0)
    tab = lambda i: (i % n_t, 0)
    kv = lambda i: (0, i // n_t, i % n_t, 0, 0)
    widths = (GDN_QKV, Z_COLS, AB_PAD, H_B * HEAD_DIM, H_B * HEAD_DIM, H_B * HEAD_DIM)
    kv_shape = jax.ShapeDtypeStruct((1, batch, seq, H_B, HEAD_DIM), F32)
    kv_spec = pl.BlockSpec((1, 1, tm, H_B, HEAD_DIM), kv)
    return pl.pallas_call(
        _in_proj_kernel,
        grid=(t // tm,),
        in_specs=[pl.BlockSpec((tm, D_MODEL), row), _const_spec((1, D_MODEL)),
                  _const_spec((D_MODEL, PROJ_COLS)), _const_spec((SUBLANES, LANES)),
                  pl.BlockSpec((tm, LANES), tab), pl.BlockSpec((tm, LANES), tab),
                  pl.BlockSpec((tm, LANES), tab)],
        out_specs=[pl.BlockSpec((tm, w), row) for w in widths] + [kv_spec, kv_spec],
        out_shape=[jax.ShapeDtypeStruct((t, w), F32) for w in widths] + [kv_shape, kv_shape],
        compiler_params=_params(("parallel",)),
        name="in_proj",
    )(x, nw, w_all, gpar, cos, sa, sb)


def _gdn_head_norms(y):
    parts = []
    for h in range(H_A):
        parts.append(_l2norm(y[:, h * HEAD_DIM:(h + 1) * HEAD_DIM]) * (HEAD_DIM ** -0.5))
    for h in range(H_A, 2 * H_A):
        parts.append(_l2norm(y[:, h * HEAD_DIM:(h + 1) * HEAD_DIM]))
    return parts


def _gdn_out_norm(o, gnw, z):
    return _rmsnorm(o, gnw) * _silu(z)


def _sum_rows(x):
    while x.shape[0] > 1:
        half = x.shape[0] // 2
        x = x[:half] + x[half:]
    return x[0]


def _block_key_sum(pages, r, out_ref):
    i = r * PAGES_PER_BLOCK
    s = _sum_rows(pages[i][0, 0])
    for j in range(1, PAGES_PER_BLOCK):
        s = s + _sum_rows(pages[i + j][0, 0])
    s = s[:H_B] + s[H_B:]
    for h in range(H_B):
        out_ref[0, r:r + 1, h * HEAD_DIM:(h + 1) * HEAD_DIM] = s[h:h + 1, :]


def _gdn_prompt_kernel(pt_ref, qkva_ref, gb_ref, z_ref, cw_ref, gnw_ref, *refs, tl, n_t, n_pages):
    pages = refs[:n_pages]
    o_ref, sfin_ref, psum_ref, xbuf, u_s, wq_s, kdt_s, qk_s, egl_s, s_s = refs[n_pages:]
    t = pl.program_id(1)
    n_c = tl // CHUNK
    pending_blocks = list(range(n_pages // PAGES_PER_BLOCK))

    def reduce_cache_blocks(count):
        for _ in range(min(count, len(pending_blocks))):
            _block_key_sum(pages, pending_blocks.pop(0), psum_ref)

    @pl.when(t == 0)
    def _():
        xbuf[0:SUBLANES, :] = jnp.zeros((SUBLANES, GDN_QKV), F32)
        s_s[...] = jnp.zeros_like(s_s)

    xbuf[SUBLANES:SUBLANES + tl, :] = qkva_ref[...]
    cw = cw_ref[...]
    y = xbuf[pl.ds(SUBLANES - 3, tl), :] * cw[0:1]
    for j in range(1, CONV_W):
        y = y + xbuf[pl.ds(SUBLANES - 3 + j, tl), :] * cw[j:j + 1]
    y = _silu(y)
    xbuf[0:SUBLANES, :] = xbuf[tl:tl + SUBLANES, :]
    qk_parts = _gdn_head_norms(y)

    ri = lax.broadcasted_iota(jnp.int32, (tl, tl), 0)
    ci = lax.broadcasted_iota(jnp.int32, (tl, tl), 1)
    same = (ri // CHUNK) == (ci // CHUNK)
    gb = gb_ref[...]
    sums = _dot(jnp.concatenate([(same & (ri >= ci)).astype(F32), same.astype(F32)], axis=0), gb, HIGHEST)
    gc_all, gl_all = sums[:tl], sums[tl:]
    gc_t = gc_all.T

    rg = lax.broadcasted_iota(jnp.int32, (SOLVE_ROWS, SOLVE_ROWS), 0)
    cg = lax.broadcasted_iota(jnp.int32, (SOLVE_ROWS, SOLVE_ROWS), 1)
    same_g = (rg // CHUNK) == (cg // CHUNK)
    tri = same_g & (rg >= cg)
    strict = same_g & (rg > cg)
    eye = (rg == cg).astype(F32)
    chunks_per_group = SOLVE_ROWS // CHUNK
    groups = [(h, grp) for h in range(H_A) for grp in range(tl // SOLVE_ROWS)]
    ps, tinvs, rhs = [], [], []
    for h, grp in groups:
        lo, hi = h * HEAD_DIM, (h + 1) * HEAD_DIM
        g0, g1 = grp * SOLVE_ROWS, (grp + 1) * SOLVE_ROWS
        q, k = qk_parts[h][g0:g1], qk_parts[H_A + h][g0:g1]
        v = y[g0:g1, 2 * H_A * HEAD_DIM + lo:2 * H_A * HEAD_DIM + hi]
        beta = gb[g0:g1, H_A + h:H_A + h + 1]
        gcol = gc_all[g0:g1, h:h + 1]
        glcol = gl_all[g0:g1, h:h + 1]
        decay = jnp.where(tri, jnp.exp(gcol - gc_t[h:h + 1, g0:g1]), 0.0)
        egc = jnp.exp(gcol)
        kbeta = k * beta
        qkk = _dot_nt(jnp.concatenate([q, kbeta], axis=0).astype(BF16), k.astype(BF16))
        qk = qkk[:SOLVE_ROWS] * decay
        p = -jnp.where(strict, qkk[SOLVE_ROWS:] * decay, 0.0)
        ps.append(p)
        tinvs.append(eye + p)
        rhs.append(jnp.concatenate([v * beta, kbeta * egc], axis=1).astype(BF16))
        qd = (q * egc).astype(BF16)
        kd = k * jnp.exp(glcol - gcol)
        egl = jnp.exp(glcol)
        for cc in range(chunks_per_group):
            c = grp * chunks_per_group + cc
            r0, r1 = cc * CHUNK, (cc + 1) * CHUNK
            wq_s[h, c, CHUNK:2 * CHUNK, :] = qd[r0:r1]
            kdt_s[h, c] = kd[r0:r1].T.astype(BF16)
            qk_s[h, c] = qk[r0:r1, r0:r1].astype(BF16)
            egl_s[h, c] = jnp.broadcast_to(egl[r0:r0 + 1, :], (SUBLANES, HEAD_DIM))
    for _ in range(5):
        for i in range(len(groups)):
            pb = ps[i].astype(BF16)
            ps[i] = _dot(pb, pb)
        for i in range(len(groups)):
            tinvs[i] = tinvs[i] + _dot(tinvs[i].astype(BF16), ps[i].astype(BF16))
        reduce_cache_blocks(3)
    for i, (h, grp) in enumerate(groups):
        lo, hi = h * HEAD_DIM, (h + 1) * HEAD_DIM
        g0, g1 = grp * SOLVE_ROWS, (grp + 1) * SOLVE_ROWS
        uw = _dot(tinvs[i].astype(BF16), rhs[i])
        u_s[g0:g1, lo:hi] = uw[:, :HEAD_DIM]
        w = uw[:, HEAD_DIM:].astype(BF16)
        for cc in range(chunks_per_group):
            c = grp * chunks_per_group + cc
            wq_s[h, c, 0:CHUNK, :] = w[cc * CHUNK:(cc + 1) * CHUNK]

    gnw = gnw_ref[...]
    for c in range(n_c):
        r0, r1 = c * CHUNK, (c + 1) * CHUNK
        zt = z_ref[r0:r1, :]
        for h in range(H_A):
            lo, hi = h * HEAD_DIM, (h + 1) * HEAD_DIM
            s = s_s[h]
            ws = _dot(wq_s[h, c], s.astype(BF16))
            v_new = u_s[r0:r1, lo:hi] - ws[:CHUNK]
            vnb = v_new.astype(BF16)
            o = ws[CHUNK:] + _dot(qk_s[h, c], vnb)
            s_s[h] = s * egl_s[h, c][0:1] + _dot(kdt_s[h, c], vnb)
            o_ref[r0:r1, lo:hi] = _gdn_out_norm(o, gnw, zt[:, lo:hi])
            reduce_cache_blocks(1)
    reduce_cache_blocks(len(pending_blocks))

    @pl.when(t == n_t - 1)
    def _():
        sfin_ref[0] = s_s[...]


def _gdn_prompt(qkva, gb, z, cw, gnw, cache_k, page_table, batch, seq, tl):
    n_t = seq // tl
    n_c = tl // CHUNK
    steps = batch * n_t
    dec_batch, pages_per_row = page_table.shape
    n_pages = dec_batch * pages_per_row // steps
    steps_per_row = pages_per_row // n_pages
    assert n_pages * steps == dec_batch * pages_per_row and steps_per_row * n_pages == pages_per_row
    blocks_per_step = n_pages // PAGES_PER_BLOCK
    width = H_B * HEAD_DIM
    cache_pairs = cache_k.reshape(1, cache_k.shape[1], PAGE_SIZE // 2, 2 * H_B, HEAD_DIM)
    row = lambda b, t, pt: (b * n_t + t, 0)
    const = lambda shape: pl.BlockSpec(shape, lambda b, t, pt: (0,) * len(shape), pipeline_mode=pl.Buffered(1))

    def page_spec(i):
        def index(b, t, pt):
            s = b * n_t + t
            return (0, pt[s // steps_per_row, (s % steps_per_row) * n_pages + i], 0, 0, 0)
        return pl.BlockSpec((1, 1, PAGE_SIZE // 2, 2 * H_B, HEAD_DIM), index)

    def psum_index(b, t, pt):
        s = b * n_t + t
        return (s // steps_per_row, s % steps_per_row, 0)

    return pl.pallas_call(
        functools.partial(_gdn_prompt_kernel, tl=tl, n_t=n_t, n_pages=n_pages),
        grid_spec=pltpu.PrefetchScalarGridSpec(
            num_scalar_prefetch=1, grid=(batch, n_t),
            in_specs=[pl.BlockSpec((tl, GDN_QKV), row), pl.BlockSpec((tl, AB_PAD), row),
                      pl.BlockSpec((tl, Z_COLS), row), const((CONV_W, GDN_QKV)), const((1, HEAD_DIM))]
                     + [page_spec(i) for i in range(n_pages)],
            out_specs=[pl.BlockSpec((tl, Z_COLS), row),
                       pl.BlockSpec((1, H_A, HEAD_DIM, HEAD_DIM), lambda b, t, pt: (b, 0, 0, 0)),
                       pl.BlockSpec((1, blocks_per_step, width), psum_index)],
            scratch_shapes=[pltpu.VMEM((tl + SUBLANES, GDN_QKV), F32),
                            pltpu.VMEM((tl, Z_COLS), F32),
                            pltpu.VMEM((H_A, n_c, 2 * CHUNK, HEAD_DIM), BF16),
                            pltpu.VMEM((H_A, n_c, HEAD_DIM, CHUNK), BF16),
                            pltpu.VMEM((H_A, n_c, CHUNK, CHUNK), BF16),
                            pltpu.VMEM((H_A, n_c, SUBLANES, HEAD_DIM), F32),
                            pltpu.VMEM((H_A, HEAD_DIM, HEAD_DIM), F32)]),
        out_shape=[jax.ShapeDtypeStruct((batch * seq, Z_COLS), F32),
                   jax.ShapeDtypeStruct((batch, H_A, HEAD_DIM, HEAD_DIM), F32),
                   jax.ShapeDtypeStruct((dec_batch, pages_per_row // PAGES_PER_BLOCK, width), F32)],
        compiler_params=_params(("arbitrary", "arbitrary")),
        name="gdn_prompt",
    )(page_table, qkva, gb, z, cw, gnw, *([cache_pairs] * n_pages))


def _row0(x, rows=SUBLANES):
    r = lax.broadcasted_iota(jnp.int32, (rows, x.shape[1]), 0)
    return jnp.where(r == 0, jnp.broadcast_to(x, (rows, x.shape[1])), 0.0)


def _gdn_sample_kernel(xn_ref, p0_ref, p1_ref, p2_ref, gb_ref, z_ref, cw_ref, gnw_ref, s0_ref,
                       o_ref, sout_ref):
    b = pl.program_id(0)
    row = lambda r: r[pl.ds(b, 1), :]
    cw = cw_ref[...]
    y = row(p0_ref) * cw[0:1]
    y = y + row(p1_ref) * cw[1:2]
    y = y + row(p2_ref) * cw[2:3]
    y = y + row(xn_ref) * cw[3:4]
    y = _silu(y)
    qk = _gdn_head_norms(y)
    gbr = row(gb_ref)
    zr = row(z_ref)
    gnw = gnw_ref[...]
    ri = lax.broadcasted_iota(jnp.int32, (HEAD_DIM, HEAD_DIM), 0)
    ci = lax.broadcasted_iota(jnp.int32, (HEAD_DIM, HEAD_DIM), 1)
    eye = (ri == ci).astype(F32)
    for h in range(H_A):
        lo, hi = h * HEAD_DIM, (h + 1) * HEAD_DIM
        q, k = qk[h], qk[H_A + h]
        v = y[:, 2 * H_A * HEAD_DIM + lo:2 * H_A * HEAD_DIM + hi]
        eg = jnp.exp(gbr[:, h:h + 1])
        beta = gbr[:, H_A + h:H_A + h + 1]
        s0 = s0_ref[0, h]
        k8 = _row0(k)
        v_new = beta * (v - eg * _dot(k8, s0, HIGHEST)[0:1])
        k_col = _dot_nt(eye, k8, HIGHEST)[:, 0:1]
        s_new = s0 * eg + k_col * v_new
        o = _dot(_row0(q), s_new, HIGHEST)[0:1]
        o_ref[0, :, lo:hi] = _gdn_out_norm(o, gnw, zr[:, lo:hi])
        sout_ref[0, h] = s_new


def _gdn_sample(xn, p0, p1, p2, gb, z, cw, gnw, s0):
    nb = xn.shape[0]
    full = lambda a: _const_spec(a.shape)
    st = pl.BlockSpec((1, H_A, HEAD_DIM, HEAD_DIM), lambda b: (b, 0, 0, 0))
    return pl.pallas_call(
        _gdn_sample_kernel,
        grid=(nb,),
        in_specs=[full(xn), full(p0), full(p1), full(p2), full(gb), full(z), full(cw), full(gnw), st],
        out_specs=[pl.BlockSpec((1, 1, Z_COLS), lambda b: (b, 0, 0)), st],
        out_shape=[jax.ShapeDtypeStruct((nb, 1, Z_COLS), F32),
                   jax.ShapeDtypeStruct(s0.shape, F32)],
        compiler_params=_params(("parallel",)),
        name="gdn_sample",
    )(xn, p0, p1, p2, gb, z, cw, gnw, s0)


def _moba_prompt_kernel(q_ref, k_ref, v_ref, o_ref, *, seq):
    nb = seq // MOBA_BLOCK
    k = k_ref[...]
    kb = k.astype(BF16)
    vb = v_ref[...].astype(BF16)
    means = [jnp.sum(k[j * MOBA_BLOCK:(j + 1) * MOBA_BLOCK], axis=0, keepdims=True) * (1.0 / MOBA_BLOCK)
             for j in range(nb)]
    km = jnp.concatenate(means + [jnp.zeros((LANES - nb, HEAD_DIM), F32)], axis=0)
    lane = lax.broadcasted_iota(jnp.int32, (MOBA_BLOCK, LANES), 1)
    ri = lax.broadcasted_iota(jnp.int32, (MOBA_BLOCK, MOBA_BLOCK), 0)
    ci = lax.broadcasted_iota(jnp.int32, (MOBA_BLOCK, MOBA_BLOCK), 1)
    causal_bias = jnp.where(ri >= ci, 0.0, NEG_INF)
    scale = HEAD_DIM ** -0.5
    for n in range(nb):
        q = q_ref[n * MOBA_BLOCK:(n + 1) * MOBA_BLOCK, :]
        gate = _dot_nt(q, km, HIGHEST)
        cnt = jnp.zeros((MOBA_BLOCK, LANES), jnp.int32)
        for i in range(n):
            gi = gate[:, i:i + 1]
            beats = (gi > gate) | ((gi == gate) & (i < lane))
            cnt = cnt + beats.astype(jnp.int32)
        sel_bias = jnp.where((cnt < MOBA_TOPK) & (lane < n), 0.0, NEG_INF)
        nk = (n + 1) * MOBA_BLOCK
        s = _dot_nt(q.astype(BF16), kb[:nk]) * scale
        bias = jnp.concatenate(
            [jnp.broadcast_to(sel_bias[:, j:j + 1], (MOBA_BLOCK, MOBA_BLOCK)) for j in range(n)]
            + [causal_bias], axis=1)
        s = s + bias
        m = jnp.max(s, axis=-1, keepdims=True)
        p = jnp.exp(s - m)
        l = jnp.sum(p, axis=-1, keepdims=True)
        o_ref[n * MOBA_BLOCK:(n + 1) * MOBA_BLOCK, :] = _dot(p.astype(BF16), vb[:nk]) / l


def _moba_prompt(qb, kb, vb, batch, seq):
    spec = pl.BlockSpec((seq, HEAD_DIM), lambda b, h: (b, h))
    return pl.pallas_call(
        functools.partial(_moba_prompt_kernel, seq=seq),
        grid=(batch, H_B),
        in_specs=[spec, spec, spec],
        out_specs=spec,
        out_shape=jax.ShapeDtypeStruct((batch * seq, H_B * HEAD_DIM), F32),
        compiler_params=_params(("parallel", "parallel")),
        name="moba_prompt",
    )(qb, kb, vb)


def _moba_select_kernel(q_ref, ps_ref, idx_ref):
    b = pl.program_id(0)
    q = q_ref[pl.ds(b, 1), :]
    km = ps_ref[0] * (1.0 / MOBA_BLOCK)
    n_blocks = km.shape[0]
    row = lax.broadcasted_iota(jnp.int32, (SUBLANES, LANES), 0)
    lane = lax.broadcasted_iota(jnp.int32, (SUBLANES, LANES), 1)
    blk = lax.broadcasted_iota(jnp.int32, (1, n_blocks), 1).astype(F32)
    out = jnp.zeros((SUBLANES, LANES), F32)
    for h in range(H_B):
        lo, hi = h * HEAD_DIM, (h + 1) * HEAD_DIM
        g = _dot_nt(_row0(q[:, lo:hi]), km[:, lo:hi], HIGHEST)[0:1]
        for s in range(MOBA_TOPK):
            m = jnp.max(g, axis=1, keepdims=True)
            idx = jnp.min(jnp.where(g == m, blk, float(n_blocks)), axis=1, keepdims=True)
            out = jnp.where((row == h) & (lane == s), idx, out)
            g = jnp.where(blk == idx, NEG_INF, g)
    idx_ref[0] = out.astype(jnp.int32)


def _moba_select(qb, psums):
    nb, n_blocks, width = psums.shape
    return pl.pallas_call(
        _moba_select_kernel,
        grid=(nb,),
        in_specs=[_const_spec(qb.shape), pl.BlockSpec((1, n_blocks, width), lambda b: (b, 0, 0))],
        out_specs=pl.BlockSpec((1, SUBLANES, LANES), lambda b: (b, 0, 0)),
        out_shape=jax.ShapeDtypeStruct((nb, SUBLANES, LANES), jnp.int32),
        compiler_params=_params(("parallel",)),
        name="moba_select",
    )(qb, psums)


N_SEL_PAGES = MOBA_TOPK * PAGES_PER_BLOCK


def _moba_attend_kernel(idx_ref, pt_ref, *refs):
    n_pages = H_B * N_SEL_PAGES
    kp, vp = refs[:n_pages], refs[n_pages:2 * n_pages]
    q_ref, k_ref, v_ref, o_ref = refs[2 * n_pages:]
    b = pl.program_id(0)
    q = q_ref[pl.ds(b, 1), :]
    k_new = k_ref[pl.ds(b, 1), :]
    v_new = v_ref[pl.ds(b, 1), :]
    scale = HEAD_DIM ** -0.5
    for h in range(H_B):
        lo, hi = h * HEAD_DIM, (h + 1) * HEAD_DIM
        pages = range(h * N_SEL_PAGES, (h + 1) * N_SEL_PAGES)
        keys = jnp.concatenate([kp[i][0, 0, :, h, :] for i in pages], axis=0)
        vals = jnp.concatenate([vp[i][0, 0, :, h, :] for i in pages], axis=0)
        qh = q[:, lo:hi]
        s = _dot_nt(_row0(qh, 2 * SUBLANES).astype(BF16), keys.astype(BF16))[0:1] * scale
        s_new = jnp.sum(qh * k_new[:, lo:hi], axis=1, keepdims=True) * scale
        m = jnp.maximum(jnp.max(s, axis=1, keepdims=True), s_new)
        p = jnp.exp(s - m)
        p_new = jnp.exp(s_new - m)
        l = jnp.sum(p, axis=1, keepdims=True) + p_new
        pv = _dot(_row0(p, 2 * SUBLANES).astype(BF16), vals.astype(BF16))[0:1]
        o_ref[0, :, lo:hi] = (pv + p_new * v_new[:, lo:hi]) / l


def _moba_attend(idx_flat, page_table, cache_k, cache_v, qb, kb, vb):
    nb = qb.shape[0]

    def page_spec(h, s, p):
        def index(b, idx, pt):
            blk = idx[(b * H_B + h) * MOBA_TOPK + s]
            return (0, pt[b, blk * PAGES_PER_BLOCK + p], 0, 0, 0)
        return pl.BlockSpec((1, 1, PAGE_SIZE, H_B, HEAD_DIM), index)

    page_specs = [page_spec(h, s, p) for h in range(H_B) for s in range(MOBA_TOPK)
                  for p in range(PAGES_PER_BLOCK)]
    n_pages = len(page_specs)
    rows = pl.BlockSpec((nb, H_B * HEAD_DIM), lambda b, idx, pt: (0, 0))
    out = pl.pallas_call(
        _moba_attend_kernel,
        grid_spec=pltpu.PrefetchScalarGridSpec(
            num_scalar_prefetch=2, grid=(nb,),
            in_specs=page_specs + page_specs + [rows, rows, rows],
            out_specs=pl.BlockSpec((1, 1, H_B * HEAD_DIM), lambda b, idx, pt: (b, 0, 0))),
        out_shape=jax.ShapeDtypeStruct((nb, 1, H_B * HEAD_DIM), F32),
        compiler_params=_params(("parallel",)),
        name="moba_attend",
    )(idx_flat, page_table, *([cache_k] * n_pages), *([cache_v] * n_pages), qb, kb, vb)
    return out.reshape(nb, H_B * HEAD_DIM)


def _out_ffn_kernel(x_ref, oa_ref, ob_ref, woa_ref, wob_ref, n2_ref, wg_ref, wu_ref, wd_ref, fn_ref, y_ref):
    x1 = (x_ref[...] + _dot(oa_ref[...].astype(BF16), woa_ref[...])
          + _dot(ob_ref[...].astype(BF16), wob_ref[...]))
    h = _rmsnorm(x1, n2_ref[...]).astype(BF16)
    act = _silu(_dot(h, wg_ref[...])) * _dot(h, wu_ref[...])
    x2 = x1 + _dot(act.astype(BF16), wd_ref[...])
    y_ref[...] = _rmsnorm(x2, fn_ref[...])


def _out_ffn(x, oa, ob, woa, wob, n2, wg, wu, wd, fn, tm):
    t = x.shape[0]
    row = lambda i: (i, 0)
    full = lambda a: _const_spec(a.shape)
    return pl.pallas_call(
        _out_ffn_kernel,
        grid=(t // tm,),
        in_specs=[pl.BlockSpec((tm, D_MODEL), row), pl.BlockSpec((tm, Z_COLS), row),
                  pl.BlockSpec((tm, H_B * HEAD_DIM), row), full(woa), full(wob), full(n2),
                  full(wg), full(wu), full(wd), full(fn)],
        out_specs=pl.BlockSpec((tm, D_MODEL), row),
        out_shape=jax.ShapeDtypeStruct((t, D_MODEL), F32),
        compiler_params=_params(("parallel",)),
        name="out_ffn",
    )(x, oa, ob, woa, wob, n2, wg, wu, wd, fn)


def _rotary_tables(pos):
    half = ROT_DIM // 2
    inv_freq = ROPE_THETA ** (-jnp.arange(half, dtype=F32) / half)
    ang = pos.astype(F32)[:, None] * inv_freq[None, :]
    cos, sin = jnp.cos(ang), jnp.sin(ang)
    n = pos.shape[0]
    rest = HEAD_DIM - ROT_DIM
    cos_t = jnp.concatenate([cos, cos, jnp.ones((n, rest), F32)], axis=1)
    zeros_h = jnp.zeros((n, half), F32)
    sin_a = jnp.concatenate([-sin, zeros_h, jnp.zeros((n, rest), F32)], axis=1)
    sin_b = jnp.concatenate([zeros_h, sin, jnp.zeros((n, rest), F32)], axis=1)
    return cos_t, sin_a, sin_b


def kernel(x_prompt, x_sample, cache_k, cache_v, page_table, state_gdn, state_conv, norm1_w, w_in, conv_w,
           a_log, dt_bias, gdn_norm_w, w_out, norm2_w, w_gate, w_up, w_down, final_norm_w):
    depth = w_in.shape[0]
    assert depth == 1, "single-layer trunk"
    batch, seq, _ = x_prompt.shape
    dec_batch, dec_seq, _ = x_sample.shape
    assert dec_seq == 1
    n_phys = cache_k.shape[1]
    past_len = page_table.shape[1] * PAGE_SIZE

    wi = w_in[0]
    ab0 = GDN_QKV + Z_COLS
    w_all = jnp.concatenate(
        [wi[:, :ab0], wi[:, ab0 + 2 * H_A:], wi[:, ab0:ab0 + 2 * H_A],
         jnp.zeros((D_MODEL, AB_PAD - 2 * H_A), wi.dtype)], axis=1).astype(BF16)
    gpar = jnp.zeros((SUBLANES, LANES), F32)
    gpar = gpar.at[0, :H_A].set(a_log[0].astype(F32)).at[1, :H_A].set(dt_bias[0].astype(F32))
    nw1 = norm1_w[0].reshape(1, D_MODEL)
    cw = conv_w[0]
    gnw = gdn_norm_w[0].reshape(1, HEAD_DIM)
    woa = w_out[0][:Z_COLS].astype(BF16)
    wob = w_out[0][Z_COLS:].astype(BF16)
    n2 = norm2_w[0].reshape(1, D_MODEL)
    wg, wu, wd = w_gate[0].astype(BF16), w_up[0].astype(BF16), w_down[0].astype(BF16)
    fn = final_norm_w.reshape(1, D_MODEL)

    xp = x_prompt.reshape(batch * seq, D_MODEL)
    tabs_p = _rotary_tables(jnp.arange(seq, dtype=jnp.int32))
    qkva_p, z_p, gb_p, qb_p, kb_p, vb_p, k5_p, v5_p = _in_proj(xp, nw1, w_all, gpar, *tabs_p,
                                                               batch=batch, seq=seq, tm=512)
    oa_p, gdn_p, psums = _gdn_prompt(qkva_p, gb_p, z_p, cw, gnw, cache_k, page_table, batch, seq, tl=256)
    ob_p = _moba_prompt(qb_p, kb_p, vb_p, batch, seq)
    y_p = _out_ffn(xp, oa_p, ob_p, woa, wob, n2, wg, wu, wd, fn, tm=512)

    xs = x_sample.reshape(dec_batch, D_MODEL)
    tabs_s = _rotary_tables(jnp.full((dec_batch,), past_len, jnp.int32))
    qkva_s, z_s, gb_s, qb_s, kb_s, vb_s, k5_s, v5_s = _in_proj(xs, nw1, w_all, gpar, *tabs_s,
                                                               batch=1, seq=dec_batch, tm=dec_batch)
    sc = state_conv[0]
    oa_s, gdn_s = _gdn_sample(qkva_s, sc[:, 0], sc[:, 1], sc[:, 2], gb_s, z_s, cw, gnw, state_gdn[0])
    idx = _moba_select(qb_s, psums)
    idx_flat = idx[:, :H_B, :MOBA_TOPK].reshape(-1)
    ob_s = _moba_attend(idx_flat, page_table, cache_k, cache_v, qb_s, kb_s, vb_s)
    y_s = _out_ffn(xs, oa_s.reshape(dec_batch, Z_COLS), ob_s, woa, wob, n2, wg, wu, wd, fn, tm=dec_batch)

    conv_p = qkva_p.reshape(batch, seq, GDN_QKV)[:, seq - (CONV_W - 1):]
    conv_s = jnp.concatenate([sc[:, 1:], qkva_s[:, None, :]], axis=1)
    return (y_p.reshape(batch, seq, D_MODEL), y_s.reshape(dec_batch, 1, D_MODEL),
            k5_p, v5_p,
            k5_s.reshape(1, dec_batch, 1, H_B, HEAD_DIM), v5_s.reshape(1, dec_batch, 1, H_B, HEAD_DIM),
            gdn_p[None], gdn_s[None], conv_p[None], conv_s[None])
```

```python
import functools

import jax
import jax.numpy as jnp
from jax import lax
from jax.experimental import pallas as pl
from jax.experimental.pallas import tpu as pltpu

D_MODEL = 1024
HEAD_DIM = 128
H_A = 4
H_B = 4
GDN_QKV = 3 * H_A * HEAD_DIM
MOBA_QKV = 3 * H_B * HEAD_DIM
Z_COLS = H_A * HEAD_DIM
CONV_W = 4
CHUNK = 64
SOLVE_ROWS = 128
MOBA_BLOCK = 256
MOBA_TOPK = 3
PAGE_SIZE = 128
PAGES_PER_BLOCK = MOBA_BLOCK // PAGE_SIZE
ROT_DIM = HEAD_DIM // 4
ROPE_THETA = 500000.0
EPS = 1e-6

LANES = 128
SUBLANES = 8
AB_PAD = LANES
PROJ_COLS = GDN_QKV + Z_COLS + MOBA_QKV + AB_PAD
VMEM_LIMIT = 56 * 1024 * 1024

F32 = jnp.float32
BF16 = jnp.bfloat16
HIGHEST = lax.Precision.HIGHEST
NEG_INF = float("-inf")


def _dot(a, b, precision=None):
    return jnp.dot(a, b, precision=precision, preferred_element_type=F32)


def _dot_nt(a, b, precision=None):
    return lax.dot_general(a, b, (((1,), (1,)), ((), ())), precision=precision,
                           preferred_element_type=F32)


def _rmsnorm(x, w):
    return x * lax.rsqrt(jnp.mean(x * x, axis=-1, keepdims=True) + EPS) * w


def _l2norm(x):
    return x * lax.rsqrt(jnp.sum(x * x, axis=-1, keepdims=True) + EPS)


def _silu(x):
    return x * jax.nn.sigmoid(x)


def _const_spec(shape):
    return pl.BlockSpec(shape, lambda *_: (0,) * len(shape), pipeline_mode=pl.Buffered(1))


def _params(semantics):
    return pltpu.CompilerParams(dimension_semantics=semantics, vmem_limit_bytes=VMEM_LIMIT)


def _in_proj_kernel(x_ref, nw_ref, w_ref, gpar_ref, cos_ref, sa_ref, sb_ref,
                    qkva_ref, z_ref, gb_ref, qb_ref, kb_ref, vb_ref, k5_ref, v5_ref):
    xn = _rmsnorm(x_ref[...], nw_ref[...]).astype(BF16)
    b0 = GDN_QKV + Z_COLS
    pb = _dot(xn, w_ref[:, b0:b0 + MOBA_QKV])
    pa = _dot(xn, w_ref[:, :GDN_QKV])
    cos, sa, sb = cos_ref[...], sa_ref[...], sb_ref[...]
    half = ROT_DIM // 2

    def rot(t):
        return (t * cos + pltpu.roll(t, HEAD_DIM - half, 1) * sa + pltpu.roll(t, half, 1) * sb)

    for h in range(H_B):
        lo, hi = h * HEAD_DIM, (h + 1) * HEAD_DIM
        qb_ref[:, lo:hi] = rot(pb[:, lo:hi])
        kh = rot(pb[:, H_B * HEAD_DIM + lo:H_B * HEAD_DIM + hi])
        vh = pb[:, 2 * H_B * HEAD_DIM + lo:2 * H_B * HEAD_DIM + hi]
        kb_ref[:, lo:hi] = kh
        vb_ref[:, lo:hi] = vh
        k5_ref[0, 0, :, h, :] = kh
        v5_ref[0, 0, :, h, :] = vh
    pz = _dot(xn, w_ref[:, GDN_QKV:b0])
    ab = _dot(xn, w_ref[:, b0 + MOBA_QKV:])
    qkva_ref[...] = pa
    z_ref[...] = pz
    lane = lax.broadcasted_iota(jnp.int32, ab.shape, 1)
    g = -jnp.exp(gpar_ref[0:1, :]) * jax.nn.softplus(ab + gpar_ref[1:2, :])
    gb_ref[...] = jnp.where(lane < H_A, g, jax.nn.sigmoid(ab))


def _in_proj(x, nw, w_all, gpar, cos, sa, sb, batch, seq, tm):
    t = batch * seq
    n_t = seq // tm
    row = lambda i: (i, 0)
    tab = lambda i: (i % n_t, 0)
    kv = lambda i: (0, i // n_t, i % n_t, 0, 0)
    widths = (GDN_QKV, Z_COLS, AB_PAD, H_B * HEAD_DIM, H_B * HEAD_DIM, H_B * HEAD_DIM)
    kv_shape = jax.ShapeDtypeStruct((1, batch, seq, H_B, HEAD_DIM), F32)
    kv_spec = pl.BlockSpec((1, 1, tm, H_B, HEAD_DIM), kv)
    return pl.pallas_call(
        _in_proj_kernel,
        grid=(t // tm,),
        in_specs=[pl.BlockSpec((tm, D_MODEL), row), _const_spec((1, D_MODEL)),
                  _const_spec((D_MODEL, PROJ_COLS)), _const_spec((SUBLANES, LANES)),
                  pl.BlockSpec((tm, LANES), tab), pl.BlockSpec((tm, LANES), tab),
                  pl.BlockSpec((tm, LANES), tab)],
        out_specs=[pl.BlockSpec((tm, w), row) for w in widths] + [kv_spec, kv_spec],
        out_shape=[jax.ShapeDtypeStruct((t, w), F32) for w in widths] + [kv_shape, kv_shape],
        compiler_params=_params(("parallel",)),
        name="in_proj",
    )(x, nw, w_all, gpar, cos, sa, sb)


def _gdn_head_norms(y):
    parts = []
    for h in range(H_A):
        parts.append(_l2norm(y[:, h * HEAD_DIM:(h + 1) * HEAD_DIM]) * (HEAD_DIM ** -0.5))
    for h in range(H_A, 2 * H_A):
        parts.append(_l2norm(y[:, h * HEAD_DIM:(h + 1) * HEAD_DIM]))
    return parts


def _gdn_out_norm(o, gnw, z):
    return _rmsnorm(o, gnw) * _silu(z)


def _sum_rows(x):
    while x.shape[0] > 1:
        half = x.shape[0] // 2
        x = x[:half] + x[half:]
    return x[0]


def _block_key_sum(pages_ref, r, out_ref):
    i = r * PAGES_PER_BLOCK
    s = _sum_rows(pages_ref[i])
    for j in range(1, PAGES_PER_BLOCK):
        s = s + _sum_rows(pages_ref[i + j])
    s = s[:H_B] + s[H_B:]
    for h in range(H_B):
        out_ref[0, r:r + 1, h * HEAD_DIM:(h + 1) * HEAD_DIM] = s[h:h + 1, :]


def _gdn_prompt_kernel(pt_ref, qkva_ref, gb_ref, z_ref, cw_ref, gnw_ref, cache_ref, *refs,
                       tl, n_t, n_tiles, n_pages, steps_per_row):
    step = pl.program_id(0)
    o_ref, sfin_ref, psum_ref, xbuf, u_s, wq_s, kdt_s, qk_s, egl_s, s_s, pbuf, psem = refs

    def page_copy(tile, i, half):
        page = pt_ref[tile // steps_per_row, (tile % steps_per_row) * n_pages + i]
        return pltpu.make_async_copy(cache_ref.at[0, page], pbuf.at[half, i], psem.at[half])

    @pl.when(step == 0)
    def _():
        xbuf[0:SUBLANES, :] = jnp.zeros((SUBLANES, GDN_QKV), F32)
        u_s[...] = jnp.zeros_like(u_s)
        wq_s[...] = jnp.zeros_like(wq_s)
        kdt_s[...] = jnp.zeros_like(kdt_s)
        qk_s[...] = jnp.zeros_like(qk_s)
        egl_s[...] = jnp.zeros_like(egl_s)
        s_s[...] = jnp.zeros_like(s_s)
        for i in range(n_pages):
            page_copy(0, i, 0).start()

    for parity in range(2):
        @pl.when(step % 2 == parity)
        def _():
            nxt = jnp.minimum(step + 1, n_tiles - 1)
            for i in range(n_pages):
                page_copy(nxt, i, 1 - parity).start()

            def wait_pages():
                for i in range(n_pages):
                    page_copy(0, i, parity).wait()

            _gdn_prompt_step(qkva_ref, gb_ref, z_ref, cw_ref, gnw_ref, o_ref, sfin_ref, psum_ref,
                             xbuf, u_s, wq_s, kdt_s, qk_s, egl_s, s_s, pbuf.at[parity], wait_pages,
                             tl=tl, n_t=n_t, n_pages=n_pages, wr=parity)

    @pl.when(step == n_tiles)
    def _():
        for i in range(n_pages):
            page_copy(0, i, (n_tiles + 1) % 2).wait()


def _gdn_prompt_step(qkva_ref, gb_ref, z_ref, cw_ref, gnw_ref, o_ref, sfin_ref, psum_ref,
                     xbuf, u_s, wq_s, kdt_s, qk_s, egl_s, s_s, pages, wait_pages, *, tl, n_t, n_pages, wr):
    wait_pages()
    step = pl.program_id(0)
    rd = 1 - wr
    a_starts_seq = (step % n_t) == 0
    b_starts_seq = ((step + n_t - 1) % n_t) == 0
    n_c = tl // CHUNK
    pending_blocks = list(range(n_pages // PAGES_PER_BLOCK))

    def reduce_cache_blocks(count):
        for _ in range(min(count, len(pending_blocks))):
            _block_key_sum(pages, pending_blocks.pop(0), psum_ref)

    tail = xbuf[0:SUBLANES, :]
    xbuf[0:SUBLANES, :] = jnp.where(a_starts_seq, 0.0, tail)
    xbuf[SUBLANES:SUBLANES + tl, :] = qkva_ref[...]
    cw = cw_ref[...]
    y = xbuf[pl.ds(SUBLANES - 3, tl), :] * cw[0:1]
    for j in range(1, CONV_W):
        y = y + xbuf[pl.ds(SUBLANES - 3 + j, tl), :] * cw[j:j + 1]
    y = _silu(y)
    xbuf[0:SUBLANES, :] = xbuf[tl:tl + SUBLANES, :]
    qk_parts = _gdn_head_norms(y)

    ri = lax.broadcasted_iota(jnp.int32, (tl, tl), 0)
    ci = lax.broadcasted_iota(jnp.int32, (tl, tl), 1)
    same = (ri // CHUNK) == (ci // CHUNK)
    gb = gb_ref[...]
    sums = _dot(jnp.concatenate([(same & (ri >= ci)).astype(F32), same.astype(F32)], axis=0), gb, HIGHEST)
    gc_all, gl_all = sums[:tl], sums[tl:]
    gc_t = gc_all.T

    rg = lax.broadcasted_iota(jnp.int32, (SOLVE_ROWS, SOLVE_ROWS), 0)
    cg = lax.broadcasted_iota(jnp.int32, (SOLVE_ROWS, SOLVE_ROWS), 1)
    same_g = (rg // CHUNK) == (cg // CHUNK)
    tri = same_g & (rg >= cg)
    strict = same_g & (rg > cg)
    eye = (rg == cg).astype(F32)
    chunks_per_group = SOLVE_ROWS // CHUNK
    groups = [(h, grp) for h in range(H_A) for grp in range(tl // SOLVE_ROWS)]
    ps, tinvs, rhs = [], [], []
    for h, grp in groups:
        lo, hi = h * HEAD_DIM, (h + 1) * HEAD_DIM
        g0, g1 = grp * SOLVE_ROWS, (grp + 1) * SOLVE_ROWS
        q, k = qk_parts[h][g0:g1], qk_parts[H_A + h][g0:g1]
        v = y[g0:g1, 2 * H_A * HEAD_DIM + lo:2 * H_A * HEAD_DIM + hi]
        beta = gb[g0:g1, H_A + h:H_A + h + 1]
        gcol = gc_all[g0:g1, h:h + 1]
        glcol = gl_all[g0:g1, h:h + 1]
        decay = jnp.where(tri, jnp.exp(gcol - gc_t[h:h + 1, g0:g1]), 0.0)
        egc = jnp.exp(gcol)
        kbeta = k * beta
        qkk = _dot_nt(jnp.concatenate([q, kbeta], axis=0).astype(BF16), k.astype(BF16))
        qk = qkk[:SOLVE_ROWS] * decay
        p = -jnp.where(strict, qkk[SOLVE_ROWS:] * decay, 0.0)
        ps.append(p)
        tinvs.append(eye + p)
        rhs.append(jnp.concatenate([v * beta, kbeta * egc], axis=1).astype(BF16))
        qd = (q * egc).astype(BF16)
        kd = k * jnp.exp(glcol - gcol)
        egl = jnp.exp(glcol)
        for cc in range(chunks_per_group):
            c = grp * chunks_per_group + cc
            r0, r1 = cc * CHUNK, (cc + 1) * CHUNK
            wq_s[wr, h, c, CHUNK:2 * CHUNK, :] = qd[r0:r1]
            kdt_s[wr, h, c] = kd[r0:r1].T.astype(BF16)
            qk_s[wr, h, c] = qk[r0:r1, r0:r1].astype(BF16)
            egl_s[wr, h, c] = jnp.broadcast_to(egl[r0:r0 + 1, :], (SUBLANES, HEAD_DIM))

    gnw = gnw_ref[...]

    def recurrence_unit(c, h):
        r0, r1 = c * CHUNK, (c + 1) * CHUNK
        lo, hi = h * HEAD_DIM, (h + 1) * HEAD_DIM
        s = s_s[h]
        if c == 0:
            s = jnp.where(b_starts_seq, 0.0, s)
        ws = _dot(wq_s[rd, h, c], s.astype(BF16))
        v_new = u_s[rd, r0:r1, lo:hi] - ws[:CHUNK]
        vnb = v_new.astype(BF16)
        o = ws[CHUNK:] + _dot(qk_s[rd, h, c], vnb)
        s_s[h] = s * egl_s[rd, h, c][0:1] + _dot(kdt_s[rd, h, c], vnb)
        o_ref[r0:r1, lo:hi] = _gdn_out_norm(o, gnw, z_ref[r0:r1, lo:hi])

    side_work = []
    for c in range(n_c):
        for h in range(H_A):
            side_work.append(functools.partial(recurrence_unit, c, h))
            side_work.extend([functools.partial(reduce_cache_blocks, 1)] * 2)
    n_iter = 5
    n_slots = 2 * n_iter * len(groups)
    done = [0]

    def side_slot(k):
        target = (k + 1) * len(side_work) // n_slots
        while done[0] < target:
            side_work[done[0]]()
            done[0] += 1

    slot = 0
    for it in range(n_iter):
        for i in range(len(groups)):
            pb = ps[i].astype(BF16)
            ps[i] = _dot(pb, pb)
            side_slot(slot)
            slot += 1
        for i in range(len(groups)):
            tinvs[i] = tinvs[i] + _dot(tinvs[i].astype(BF16), ps[i].astype(BF16))
            side_slot(slot)
            slot += 1
    for i, (h, grp) in enumerate(groups):
        lo, hi = h * HEAD_DIM, (h + 1) * HEAD_DIM
        g0, g1 = grp * SOLVE_ROWS, (grp + 1) * SOLVE_ROWS
        uw = _dot(tinvs[i].astype(BF16), rhs[i])
        u_s[wr, g0:g1, lo:hi] = uw[:, :HEAD_DIM]
        w = uw[:, HEAD_DIM:].astype(BF16)
        for cc in range(chunks_per_group):
            c = grp * chunks_per_group + cc
            wq_s[wr, h, c, 0:CHUNK, :] = w[cc * CHUNK:(cc + 1) * CHUNK]
    reduce_cache_blocks(len(pending_blocks))
    sfin_ref[0] = s_s[...]


def _gdn_prompt(qkva, gb, z, cw, gnw, cache_k, page_table, batch, seq, tl):
    n_t = seq // tl
    n_c = tl // CHUNK
    n_tiles = batch * n_t
    dec_batch, pages_per_row = page_table.shape
    n_pages = dec_batch * pages_per_row // n_tiles
    steps_per_row = pages_per_row // n_pages
    assert n_pages * n_tiles == dec_batch * pages_per_row and steps_per_row * n_pages == pages_per_row
    blocks_per_step = n_pages // PAGES_PER_BLOCK
    width = H_B * HEAD_DIM
    cache_pairs = cache_k.reshape(1, cache_k.shape[1], PAGE_SIZE // 2, 2 * H_B, HEAD_DIM)
    tile_a = lambda s: jnp.minimum(s, n_tiles - 1)
    tile_b = lambda s: jnp.maximum(s - 1, 0)
    row_a = lambda s, pt: (tile_a(s), 0)
    row_b = lambda s, pt: (tile_b(s), 0)
    const = lambda shape: pl.BlockSpec(shape, lambda s, pt: (0,) * len(shape), pipeline_mode=pl.Buffered(1))

    def psum_index(s, pt):
        a = tile_a(s)
        return (a // steps_per_row, a % steps_per_row, 0)

    return pl.pallas_call(
        functools.partial(_gdn_prompt_kernel, tl=tl, n_t=n_t, n_tiles=n_tiles, n_pages=n_pages,
                          steps_per_row=steps_per_row),
        grid_spec=pltpu.PrefetchScalarGridSpec(
            num_scalar_prefetch=1, grid=(n_tiles + 1,),
            in_specs=[pl.BlockSpec((tl, GDN_QKV), row_a), pl.BlockSpec((tl, AB_PAD), row_a),
                      pl.BlockSpec((tl, Z_COLS), row_b), const((CONV_W, GDN_QKV)), const((1, HEAD_DIM)),
                      pl.BlockSpec(memory_space=pl.ANY)],
            out_specs=[pl.BlockSpec((tl, Z_COLS), row_b),
                       pl.BlockSpec((1, H_A, HEAD_DIM, HEAD_DIM), lambda s, pt: (tile_b(s) // n_t, 0, 0, 0)),
                       pl.BlockSpec((1, blocks_per_step, width), psum_index)],
            scratch_shapes=[pltpu.VMEM((tl + SUBLANES, GDN_QKV), F32),
                            pltpu.VMEM((2, tl, Z_COLS), F32),
                            pltpu.VMEM((2, H_A, n_c, 2 * CHUNK, HEAD_DIM), BF16),
                            pltpu.VMEM((2, H_A, n_c, HEAD_DIM, CHUNK), BF16),
                            pltpu.VMEM((2, H_A, n_c, CHUNK, CHUNK), BF16),
                            pltpu.VMEM((2, H_A, n_c, SUBLANES, HEAD_DIM), F32),
                            pltpu.VMEM((H_A, HEAD_DIM, HEAD_DIM), F32),
                            pltpu.VMEM((2, n_pages, PAGE_SIZE // 2, 2 * H_B, HEAD_DIM), F32),
                            pltpu.SemaphoreType.DMA((2,))]),
        out_shape=[jax.ShapeDtypeStruct((batch * seq, Z_COLS), F32),
                   jax.ShapeDtypeStruct((batch, H_A, HEAD_DIM, HEAD_DIM), F32),
                   jax.ShapeDtypeStruct((dec_batch, pages_per_row // PAGES_PER_BLOCK, width), F32)],
        compiler_params=_params(("arbitrary",)),
        name="gdn_prompt",
    )(page_table, qkva, gb, z, cw, gnw, cache_pairs)


def _row0(x, rows=SUBLANES):
    r = lax.broadcasted_iota(jnp.int32, (rows, x.shape[1]), 0)
    return jnp.where(r == 0, jnp.broadcast_to(x, (rows, x.shape[1])), 0.0)


GDN_SAMPLE_ROWS = SUBLANES


def _gdn_sample_kernel(xn_ref, p0_ref, p1_ref, p2_ref, gb_ref, z_ref, cw_ref, gnw_ref, s0_ref,
                       o_ref, sout_ref):
    r0 = pl.multiple_of(pl.program_id(0) * GDN_SAMPLE_ROWS, GDN_SAMPLE_ROWS)
    rows = lambda r: r[pl.ds(r0, GDN_SAMPLE_ROWS), :]
    cw = cw_ref[...]
    y = rows(p0_ref) * cw[0:1]
    y = y + rows(p1_ref) * cw[1:2]
    y = y + rows(p2_ref) * cw[2:3]
    y = y + rows(xn_ref) * cw[3:4]
    y = _silu(y)
    qk = _gdn_head_norms(y)
    gbr = rows(gb_ref)
    zr = rows(z_ref)
    gnw = gnw_ref[...]
    ri = lax.broadcasted_iota(jnp.int32, (HEAD_DIM, HEAD_DIM), 0)
    ci = lax.broadcasted_iota(jnp.int32, (HEAD_DIM, HEAD_DIM), 1)
    eye = (ri == ci).astype(BF16)
    pad = 2 * SUBLANES
    for j in range(GDN_SAMPLE_ROWS):
        for h in range(H_A):
            lo, hi = h * HEAD_DIM, (h + 1) * HEAD_DIM
            q, k = qk[h][j:j + 1], qk[H_A + h][j:j + 1]
            v = y[j:j + 1, 2 * H_A * HEAD_DIM + lo:2 * H_A * HEAD_DIM + hi]
            eg = jnp.exp(gbr[j:j + 1, h:h + 1])
            beta = gbr[j:j + 1, H_A + h:H_A + h + 1]
            s0 = s0_ref[j, h]
            kp = _row0(k, pad).astype(BF16)
            v_new = beta * (v - eg * _dot(kp, s0.astype(BF16))[0:1])
            k_col = _dot_nt(eye, kp)[:, 0:1]
            s_new = s0 * eg + k_col * v_new
            o = _dot(_row0(q, pad).astype(BF16), s_new.astype(BF16))[0:1]
            o_ref[j, :, lo:hi] = _gdn_out_norm(o, gnw, zr[j:j + 1, lo:hi])
            sout_ref[j, h] = s_new


def _gdn_sample(xn, p0, p1, p2, gb, z, cw, gnw, s0):
    nb = xn.shape[0]
    full = lambda a: _const_spec(a.shape)
    st = pl.BlockSpec((GDN_SAMPLE_ROWS, H_A, HEAD_DIM, HEAD_DIM), lambda b: (b, 0, 0, 0))
    return pl.pallas_call(
        _gdn_sample_kernel,
        grid=(nb // GDN_SAMPLE_ROWS,),
        in_specs=[full(xn), full(p0), full(p1), full(p2), full(gb), full(z), full(cw), full(gnw), st],
        out_specs=[pl.BlockSpec((GDN_SAMPLE_ROWS, 1, Z_COLS), lambda b: (b, 0, 0)), st],
        out_shape=[jax.ShapeDtypeStruct((nb, 1, Z_COLS), F32),
                   jax.ShapeDtypeStruct(s0.shape, F32)],
        compiler_params=_params(("parallel",)),
        name="gdn_sample",
    )(xn, p0, p1, p2, gb, z, cw, gnw, s0)


def _moba_prompt_kernel(q_ref, k_ref, v_ref, o_ref, *, seq):
    nb = seq // MOBA_BLOCK
    k = k_ref[...]
    kb = k.astype(BF16)
    vb = v_ref[...].astype(BF16)
    means = [jnp.sum(k[j * MOBA_BLOCK:(j + 1) * MOBA_BLOCK], axis=0, keepdims=True) * (1.0 / MOBA_BLOCK)
             for j in range(nb)]
    km = jnp.concatenate(means + [jnp.zeros((LANES - nb, HEAD_DIM), F32)], axis=0)
    lane = lax.broadcasted_iota(jnp.int32, (MOBA_BLOCK, LANES), 1)
    ri = lax.broadcasted_iota(jnp.int32, (MOBA_BLOCK, MOBA_BLOCK), 0)
    ci = lax.broadcasted_iota(jnp.int32, (MOBA_BLOCK, MOBA_BLOCK), 1)
    causal_bias = jnp.where(ri >= ci, 0.0, NEG_INF)
    scale = HEAD_DIM ** -0.5
    for n in range(nb):
        q = q_ref[n * MOBA_BLOCK:(n + 1) * MOBA_BLOCK, :]
        gate = _dot_nt(q, km, HIGHEST)
        cnt = jnp.zeros((MOBA_BLOCK, LANES), jnp.int32)
        for i in range(n):
            gi = gate[:, i:i + 1]
            beats = (gi > gate) | ((gi == gate) & (i < lane))
            cnt = cnt + beats.astype(jnp.int32)
        sel_bias = jnp.where((cnt < MOBA_TOPK) & (lane < n), 0.0, NEG_INF)
        nk = (n + 1) * MOBA_BLOCK
        s = _dot_nt(q.astype(BF16), kb[:nk]) * scale
        bias = jnp.concatenate(
            [jnp.broadcast_to(sel_bias[:, j:j + 1], (MOBA_BLOCK, MOBA_BLOCK)) for j in range(n)]
            + [causal_bias], axis=1)
        s = s + bias
        m = jnp.max(s, axis=-1, keepdims=True)
        p = jnp.exp(s - m)
        l = jnp.sum(p, axis=-1, keepdims=True)
        o_ref[n * MOBA_BLOCK:(n + 1) * MOBA_BLOCK, :] = _dot(p.astype(BF16), vb[:nk]) / l


def _moba_prompt(qb, kb, vb, batch, seq):
    spec = pl.BlockSpec((seq, HEAD_DIM), lambda b, h: (b, h))
    return pl.pallas_call(
        functools.partial(_moba_prompt_kernel, seq=seq),
        grid=(batch, H_B),
        in_specs=[spec, spec, spec],
        out_specs=spec,
        out_shape=jax.ShapeDtypeStruct((batch * seq, H_B * HEAD_DIM), F32),
        compiler_params=_params(("parallel", "parallel")),
        name="moba_prompt",
    )(qb, kb, vb)


def _moba_select_kernel(q_ref, ps_ref, idx_ref):
    b = pl.program_id(0)
    q = q_ref[pl.ds(b, 1), :]
    km = ps_ref[0] * (1.0 / MOBA_BLOCK)
    n_blocks = km.shape[0]
    row = lax.broadcasted_iota(jnp.int32, (SUBLANES, LANES), 0)
    lane = lax.broadcasted_iota(jnp.int32, (SUBLANES, LANES), 1)
    blk = lax.broadcasted_iota(jnp.int32, (1, n_blocks), 1).astype(F32)
    out = jnp.zeros((SUBLANES, LANES), F32)
    for h in range(H_B):
        lo, hi = h * HEAD_DIM, (h + 1) * HEAD_DIM
        g = _dot_nt(_row0(q[:, lo:hi]), km[:, lo:hi], HIGHEST)[0:1]
        for s in range(MOBA_TOPK):
            m = jnp.max(g, axis=1, keepdims=True)
            idx = jnp.min(jnp.where(g == m, blk, float(n_blocks)), axis=1, keepdims=True)
            out = jnp.where((row == h) & (lane == s), idx, out)
            g = jnp.where(blk == idx, NEG_INF, g)
    idx_ref[0] = out.astype(jnp.int32)


def _moba_select(qb, psums):
    nb, n_blocks, width = psums.shape
    return pl.pallas_call(
        _moba_select_kernel,
        grid=(nb,),
        in_specs=[_const_spec(qb.shape), pl.BlockSpec((1, n_blocks, width), lambda b: (b, 0, 0))],
        out_specs=pl.BlockSpec((1, SUBLANES, LANES), lambda b: (b, 0, 0)),
        out_shape=jax.ShapeDtypeStruct((nb, SUBLANES, LANES), jnp.int32),
        compiler_params=_params(("parallel",)),
        name="moba_select",
    )(qb, psums)


N_SEL_PAGES = MOBA_TOPK * PAGES_PER_BLOCK


def _moba_attend_kernel(idx_ref, pt_ref, *refs):
    n_pages = H_B * N_SEL_PAGES
    kp, vp = refs[:n_pages], refs[n_pages:2 * n_pages]
    q_ref, k_ref, v_ref, o_ref = refs[2 * n_pages:]
    b = pl.program_id(0)
    q = q_ref[pl.ds(b, 1), :]
    k_new = k_ref[pl.ds(b, 1), :]
    v_new = v_ref[pl.ds(b, 1), :]
    scale = HEAD_DIM ** -0.5
    for h in range(H_B):
        lo, hi = h * HEAD_DIM, (h + 1) * HEAD_DIM
        pages = range(h * N_SEL_PAGES, (h + 1) * N_SEL_PAGES)
        keys = jnp.concatenate([kp[i][0, 0, :, h, :] for i in pages], axis=0)
        vals = jnp.concatenate([vp[i][0, 0, :, h, :] for i in pages], axis=0)
        qh = q[:, lo:hi]
        s = _dot_nt(_row0(qh, 2 * SUBLANES).astype(BF16), keys.astype(BF16))[0:1] * scale
        s_new = jnp.sum(qh * k_new[:, lo:hi], axis=1, keepdims=True) * scale
        m = jnp.maximum(jnp.max(s, axis=1, keepdims=True), s_new)
        p = jnp.exp(s - m)
        p_new = jnp.exp(s_new - m)
        l = jnp.sum(p, axis=1, keepdims=True) + p_new
        pv = _dot(_row0(p, 2 * SUBLANES).astype(BF16), vals.astype(BF16))[0:1]
        o_ref[0, :, lo:hi] = (pv + p_new * v_new[:, lo:hi]) / l


def _moba_attend(idx_flat, page_table, cache_k, cache_v, qb, kb, vb):
    nb = qb.shape[0]

    def page_spec(h, s, p):
        def index(b, idx, pt):
            blk = idx[(b * H_B + h) * MOBA_TOPK + s]
            return (0, pt[b, blk * PAGES_PER_BLOCK + p], 0, 0, 0)
        return pl.BlockSpec((1, 1, PAGE_SIZE, H_B, HEAD_DIM), index)

    page_specs = [page_spec(h, s, p) for h in range(H_B) for s in range(MOBA_TOPK)
                  for p in range(PAGES_PER_BLOCK)]
    n_pages = len(page_specs)
    rows = pl.BlockSpec((nb, H_B * HEAD_DIM), lambda b, idx, pt: (0, 0))
    out = pl.pallas_call(
        _moba_attend_kernel,
        grid_spec=pltpu.PrefetchScalarGridSpec(
            num_scalar_prefetch=2, grid=(nb,),
            in_specs=page_specs + page_specs + [rows, rows, rows],
            out_specs=pl.BlockSpec((1, 1, H_B * HEAD_DIM), lambda b, idx, pt: (b, 0, 0))),
        out_shape=jax.ShapeDtypeStruct((nb, 1, H_B * HEAD_DIM), F32),
        compiler_params=_params(("parallel",)),
        name="moba_attend",
    )(idx_flat, page_table, *([cache_k] * n_pages), *([cache_v] * n_pages), qb, kb, vb)
    return out.reshape(nb, H_B * HEAD_DIM)


def _out_ffn_kernel(x_ref, oa_ref, ob_ref, woa_ref, wob_ref, n2_ref, wg_ref, wu_ref, wd_ref, fn_ref, y_ref):
    x1 = (x_ref[...] + _dot(oa_ref[...].astype(BF16), woa_ref[...])
          + _dot(ob_ref[...].astype(BF16), wob_ref[...]))
    h = _rmsnorm(x1, n2_ref[...]).astype(BF16)
    act = _silu(_dot(h, wg_ref[...])) * _dot(h, wu_ref[...])
    x2 = x1 + _dot(act.astype(BF16), wd_ref[...])
    y_ref[...] = _rmsnorm(x2, fn_ref[...])


def _out_ffn(x, oa, ob, woa, wob, n2, wg, wu, wd, fn, tm):
    t = x.shape[0]
    row = lambda i: (i, 0)
    full = lambda a: _const_spec(a.shape)
    return pl.pallas_call(
        _out_ffn_kernel,
        grid=(t // tm,),
        in_specs=[pl.BlockSpec((tm, D_MODEL), row), pl.BlockSpec((tm, Z_COLS), row),
                  pl.BlockSpec((tm, H_B * HEAD_DIM), row), full(woa), full(wob), full(n2),
                  full(wg), full(wu), full(wd), full(fn)],
        out_specs=pl.BlockSpec((tm, D_MODEL), row),
        out_shape=jax.ShapeDtypeStruct((t, D_MODEL), F32),
        compiler_params=_params(("parallel",)),
        name="out_ffn",
    )(x, oa, ob, woa, wob, n2, wg, wu, wd, fn)


def _rotary_tables(pos):
    half = ROT_DIM // 2
    inv_freq = ROPE_THETA ** (-jnp.arange(half, dtype=F32) / half)
    ang = pos.astype(F32)[:, None] * inv_freq[None, :]
    cos, sin = jnp.cos(ang), jnp.sin(ang)
    n = pos.shape[0]
    rest = HEAD_DIM - ROT_DIM
    cos_t = jnp.concatenate([cos, cos, jnp.ones((n, rest), F32)], axis=1)
    zeros_h = jnp.zeros((n, half), F32)
    sin_a = jnp.concatenate([-sin, zeros_h, jnp.zeros((n, rest), F32)], axis=1)
    sin_b = jnp.concatenate([zeros_h, sin, jnp.zeros((n, rest), F32)], axis=1)
    return cos_t, sin_a, sin_b


def kernel(x_prompt, x_sample, cache_k, cache_v, page_table, state_gdn, state_conv, norm1_w, w_in, conv_w,
           a_log, dt_bias, gdn_norm_w, w_out, norm2_w, w_gate, w_up, w_down, final_norm_w):
    depth = w_in.shape[0]
    assert depth == 1, "single-layer trunk"
    batch, seq, _ = x_prompt.shape
    dec_batch, dec_seq, _ = x_sample.shape
    assert dec_seq == 1
    n_phys = cache_k.shape[1]
    past_len = page_table.shape[1] * PAGE_SIZE

    wi = w_in[0]
    ab0 = GDN_QKV + Z_COLS
    w_all = jnp.concatenate(
        [wi[:, :ab0], wi[:, ab0 + 2 * H_A:], wi[:, ab0:ab0 + 2 * H_A],
         jnp.zeros((D_MODEL, AB_PAD - 2 * H_A), wi.dtype)], axis=1).astype(BF16)
    gpar = jnp.zeros((SUBLANES, LANES), F32)
    gpar = gpar.at[0, :H_A].set(a_log[0].astype(F32)).at[1, :H_A].set(dt_bias[0].astype(F32))
    nw1 = norm1_w[0].reshape(1, D_MODEL)
    cw = conv_w[0]
    gnw = gdn_norm_w[0].reshape(1, HEAD_DIM)
    woa = w_out[0][:Z_COLS].astype(BF16)
    wob = w_out[0][Z_COLS:].astype(BF16)
    n2 = norm2_w[0].reshape(1, D_MODEL)
    wg, wu, wd = w_gate[0].astype(BF16), w_up[0].astype(BF16), w_down[0].astype(BF16)
    fn = final_norm_w.reshape(1, D_MODEL)

    xp = x_prompt.reshape(batch * seq, D_MODEL)
    tabs_p = _rotary_tables(jnp.arange(seq, dtype=jnp.int32))
    qkva_p, z_p, gb_p, qb_p, kb_p, vb_p, k5_p, v5_p = _in_proj(xp, nw1, w_all, gpar, *tabs_p,
                                                               batch=batch, seq=seq, tm=512)
    oa_p, gdn_p, psums = _gdn_prompt(qkva_p, gb_p, z_p, cw, gnw, cache_k, page_table, batch, seq, tl=256)
    ob_p = _moba_prompt(qb_p, kb_p, vb_p, batch, seq)
    y_p = _out_ffn(xp, oa_p, ob_p, woa, wob, n2, wg, wu, wd, fn, tm=512)

    xs = x_sample.reshape(dec_batch, D_MODEL)
    tabs_s = _rotary_tables(jnp.full((dec_batch,), past_len, jnp.int32))
    qkva_s, z_s, gb_s, qb_s, kb_s, vb_s, k5_s, v5_s = _in_proj(xs, nw1, w_all, gpar, *tabs_s,
                                                               batch=1, seq=dec_batch, tm=dec_batch)
    sc = state_conv[0]
    oa_s, gdn_s = _gdn_sample(qkva_s, sc[:, 0], sc[:, 1], sc[:, 2], gb_s, z_s, cw, gnw, state_gdn[0])
    idx = _moba_select(qb_s, psums)
    idx_flat = idx[:, :H_B, :MOBA_TOPK].reshape(-1)
    ob_s = _moba_attend(idx_flat, page_table, cache_k, cache_v, qb_s, kb_s, vb_s)
    y_s = _out_ffn(xs, oa_s.reshape(dec_batch, Z_COLS), ob_s, woa, wob, n2, wg, wu, wd, fn, tm=dec_batch)

    conv_p = qkva_p.reshape(batch, seq, GDN_QKV)[:, seq - (CONV_W - 1):]
    conv_s = jnp.concatenate([sc[:, 1:], qkva_s[:, None, :]], axis=1)
    return (y_p.reshape(batch, seq, D_MODEL), y_s.reshape(dec_batch, 1, D_MODEL),
            k5_p, v5_p,
            k5_s.reshape(1, dec_batch, 1, H_B, HEAD_DIM), v5_s.reshape(1, dec_batch, 1, H_B, HEAD_DIM),
            gdn_p[None], gdn_s[None], conv_p[None], conv_s[None])
```

```python
import functools

import jax
import jax.numpy as jnp
from jax import lax
from jax.experimental import pallas as pl
from jax.experimental.pallas import tpu as pltpu

D_MODEL = 1024
HEAD_DIM = 128
H_A = 4
H_B = 4
GDN_QKV = 3 * H_A * HEAD_DIM
MOBA_QKV = 3 * H_B * HEAD_DIM
Z_COLS = H_A * HEAD_DIM
CONV_W = 4
CHUNK = 64
SOLVE_ROWS = 128
MOBA_BLOCK = 256
MOBA_TOPK = 3
PAGE_SIZE = 128
PAGES_PER_BLOCK = MOBA_BLOCK // PAGE_SIZE
ROT_DIM = HEAD_DIM // 4
ROPE_THETA = 500000.0
EPS = 1e-6

LANES = 128
SUBLANES = 8
AB_PAD = LANES
PROJ_COLS = GDN_QKV + Z_COLS + MOBA_QKV + AB_PAD
VMEM_LIMIT = 56 * 1024 * 1024

F32 = jnp.float32
BF16 = jnp.bfloat16
HIGHEST = lax.Precision.HIGHEST
NEG_INF = float("-inf")


def _dot(a, b, precision=None):
    return jnp.dot(a, b, precision=precision, preferred_element_type=F32)


def _dot_nt(a, b, precision=None):
    return lax.dot_general(a, b, (((1,), (1,)), ((), ())), precision=precision,
                           preferred_element_type=F32)


def _rmsnorm(x, w):
    return x * lax.rsqrt(jnp.mean(x * x, axis=-1, keepdims=True) + EPS) * w


def _l2norm(x):
    return x * lax.rsqrt(jnp.sum(x * x, axis=-1, keepdims=True) + EPS)


def _silu(x):
    return x * jax.nn.sigmoid(x)


def _const_spec(shape):
    return pl.BlockSpec(shape, lambda *_: (0,) * len(shape), pipeline_mode=pl.Buffered(1))


def _params(semantics):
    return pltpu.CompilerParams(dimension_semantics=semantics, vmem_limit_bytes=VMEM_LIMIT)


def _in_proj_kernel(x_ref, nw_ref, w_ref, gpar_ref, cos_ref, sa_ref, sb_ref,
                    qkva_ref, z_ref, gb_ref, qb_ref, kb_ref, vb_ref, k5_ref, v5_ref):
    xn = _rmsnorm(x_ref[...], nw_ref[...]).astype(BF16)
    b0 = GDN_QKV + Z_COLS
    pb = _dot(xn, w_ref[:, b0:b0 + MOBA_QKV])
    pa = _dot(xn, w_ref[:, :GDN_QKV])
    cos, sa, sb = cos_ref[...], sa_ref[...], sb_ref[...]
    half = ROT_DIM // 2

    def rot(t):
        return (t * cos + pltpu.roll(t, HEAD_DIM - half, 1) * sa + pltpu.roll(t, half, 1) * sb)

    for h in range(H_B):
        lo, hi = h * HEAD_DIM, (h + 1) * HEAD_DIM
        qb_ref[:, lo:hi] = rot(pb[:, lo:hi])
        kh = rot(pb[:, H_B * HEAD_DIM + lo:H_B * HEAD_DIM + hi])
        vh = pb[:, 2 * H_B * HEAD_DIM + lo:2 * H_B * HEAD_DIM + hi]
        kb_ref[:, lo:hi] = kh
        vb_ref[:, lo:hi] = vh
        k5_ref[0, 0, :, h, :] = kh
        v5_ref[0, 0, :, h, :] = vh
    pz = _dot(xn, w_ref[:, GDN_QKV:b0])
    ab = _dot(xn, w_ref[:, b0 + MOBA_QKV:])
    qkva_ref[...] = pa
    z_ref[...] = pz
    lane = lax.broadcasted_iota(jnp.int32, ab.shape, 1)
    g = -jnp.exp(gpar_ref[0:1, :]) * jax.nn.softplus(ab + gpar_ref[1:2, :])
    gb_ref[...] = jnp.where(lane < H_A, g, jax.nn.sigmoid(ab))


def _in_proj(x, nw, w_all, gpar, cos, sa, sb, batch, seq, tm):
    t = batch * seq
    n_t = seq // tm
    row = lambda i: (i, 0)
    tab = lambda i: (i % n_t, 0)
    kv = lambda i: (0, i // n_t, i % n_t, 0, 0)
    widths = (GDN_QKV, Z_COLS, AB_PAD, H_B * HEAD_DIM, H_B * HEAD_DIM, H_B * HEAD_DIM)
    kv_shape = jax.ShapeDtypeStruct((1, batch, seq, H_B, HEAD_DIM), F32)
    kv_spec = pl.BlockSpec((1, 1, tm, H_B, HEAD_DIM), kv)
    return pl.pallas_call(
        _in_proj_kernel,
        grid=(t // tm,),
        in_specs=[pl.BlockSpec((tm, D_MODEL), row), _const_spec((1, D_MODEL)),
                  _const_spec((D_MODEL, PROJ_COLS)), _const_spec((SUBLANES, LANES)),
                  pl.BlockSpec((tm, LANES), tab), pl.BlockSpec((tm, LANES), tab),
                  pl.BlockSpec((tm, LANES), tab)],
        out_specs=[pl.BlockSpec((tm, w), row) for w in widths] + [kv_spec, kv_spec],
        out_shape=[jax.ShapeDtypeStruct((t, w), F32) for w in widths] + [kv_shape, kv_shape],
        compiler_params=_params(("parallel",)),
        name="in_proj",
    )(x, nw, w_all, gpar, cos, sa, sb)


def _gdn_head_norms(y):
    parts = []
    for h in range(H_A):
        parts.append(_l2norm(y[:, h * HEAD_DIM:(h + 1) * HEAD_DIM]) * (HEAD_DIM ** -0.5))
    for h in range(H_A, 2 * H_A):
        parts.append(_l2norm(y[:, h * HEAD_DIM:(h + 1) * HEAD_DIM]))
    return parts


def _gdn_out_norm(o, gnw, z):
    return _rmsnorm(o, gnw) * _silu(z)


def _sum_rows(x):
    while x.shape[0] > 1:
        half = x.shape[0] // 2
        x = x[:half] + x[half:]
    return x[0]


def _block_key_sum(pages_ref, r, out_ref):
    i = r * PAGES_PER_BLOCK
    s = _sum_rows(pages_ref[i])
    for j in range(1, PAGES_PER_BLOCK):
        s = s + _sum_rows(pages_ref[i + j])
    s = s[:H_B] + s[H_B:]
    for h in range(H_B):
        out_ref[0, r:r + 1, h * HEAD_DIM:(h + 1) * HEAD_DIM] = s[h:h + 1, :]


def _gdn_prompt_kernel(pt_ref, qkva_ref, gb_ref, z_ref, cw_ref, gnw_ref, cache_ref, *refs,
                       tl, n_t, n_tiles, n_pages, steps_per_row):
    step = pl.program_id(0)
    o_ref, sfin_ref, psum_ref, xbuf, u_s, wq_s, kdt_s, qk_s, egl_s, s_s, pbuf, psem = refs

    def page_copy(tile, i, half):
        page = pt_ref[tile // steps_per_row, (tile % steps_per_row) * n_pages + i]
        return pltpu.make_async_copy(cache_ref.at[0, page], pbuf.at[half, i], psem.at[half])

    @pl.when(step == 0)
    def _():
        xbuf[0:SUBLANES, :] = jnp.zeros((SUBLANES, GDN_QKV), F32)
        u_s[...] = jnp.zeros_like(u_s)
        wq_s[...] = jnp.zeros_like(wq_s)
        kdt_s[...] = jnp.zeros_like(kdt_s)
        qk_s[...] = jnp.zeros_like(qk_s)
        egl_s[...] = jnp.zeros_like(egl_s)
        s_s[...] = jnp.zeros_like(s_s)
        for i in range(n_pages):
            page_copy(0, i, 0).start()

    for parity in range(2):
        @pl.when(step % 2 == parity)
        def _():
            nxt = jnp.minimum(step + 1, n_tiles - 1)
            for i in range(n_pages):
                page_copy(nxt, i, 1 - parity).start()

            def wait_pages():
                for i in range(n_pages):
                    page_copy(0, i, parity).wait()

            _gdn_prompt_step(qkva_ref, gb_ref, z_ref, cw_ref, gnw_ref, o_ref, sfin_ref, psum_ref,
                             xbuf, u_s, wq_s, kdt_s, qk_s, egl_s, s_s, pbuf.at[parity], wait_pages,
                             tl=tl, n_t=n_t, n_pages=n_pages, wr=parity)

    @pl.when(step == n_tiles)
    def _():
        for i in range(n_pages):
            page_copy(0, i, (n_tiles + 1) % 2).wait()


def _gdn_prompt_step(qkva_ref, gb_ref, z_ref, cw_ref, gnw_ref, o_ref, sfin_ref, psum_ref,
                     xbuf, u_s, wq_s, kdt_s, qk_s, egl_s, s_s, pages, wait_pages, *, tl, n_t, n_pages, wr):
    wait_pages()
    step = pl.program_id(0)
    rd = 1 - wr
    a_starts_seq = (step % n_t) == 0
    b_starts_seq = ((step + n_t - 1) % n_t) == 0
    n_c = tl // CHUNK
    pending_blocks = list(range(n_pages // PAGES_PER_BLOCK))

    def reduce_cache_blocks(count):
        for _ in range(min(count, len(pending_blocks))):
            _block_key_sum(pages, pending_blocks.pop(0), psum_ref)

    tail = xbuf[0:SUBLANES, :]
    xbuf[0:SUBLANES, :] = jnp.where(a_starts_seq, 0.0, tail)
    xbuf[SUBLANES:SUBLANES + tl, :] = qkva_ref[...]
    cw = cw_ref[...]
    y = xbuf[pl.ds(SUBLANES - 3, tl), :] * cw[0:1]
    for j in range(1, CONV_W):
        y = y + xbuf[pl.ds(SUBLANES - 3 + j, tl), :] * cw[j:j + 1]
    y = _silu(y)
    xbuf[0:SUBLANES, :] = xbuf[tl:tl + SUBLANES, :]
    qk_parts = _gdn_head_norms(y)

    ri = lax.broadcasted_iota(jnp.int32, (tl, tl), 0)
    ci = lax.broadcasted_iota(jnp.int32, (tl, tl), 1)
    same = (ri // CHUNK) == (ci // CHUNK)
    gb = gb_ref[...]
    sums = _dot(jnp.concatenate([(same & (ri >= ci)).astype(F32), same.astype(F32)], axis=0), gb, HIGHEST)
    gc_all, gl_all = sums[:tl], sums[tl:]
    gc_t = gc_all.T

    rg = lax.broadcasted_iota(jnp.int32, (SOLVE_ROWS, SOLVE_ROWS), 0)
    cg = lax.broadcasted_iota(jnp.int32, (SOLVE_ROWS, SOLVE_ROWS), 1)
    same_g = (rg // CHUNK) == (cg // CHUNK)
    tri = same_g & (rg >= cg)
    strict = same_g & (rg > cg)
    eye = (rg == cg).astype(F32)
    chunks_per_group = SOLVE_ROWS // CHUNK
    groups = [(h, grp) for h in range(H_A) for grp in range(tl // SOLVE_ROWS)]
    ps, tinvs, rhs = [], [], []
    for h, grp in groups:
        lo, hi = h * HEAD_DIM, (h + 1) * HEAD_DIM
        g0, g1 = grp * SOLVE_ROWS, (grp + 1) * SOLVE_ROWS
        q, k = qk_parts[h][g0:g1], qk_parts[H_A + h][g0:g1]
        v = y[g0:g1, 2 * H_A * HEAD_DIM + lo:2 * H_A * HEAD_DIM + hi]
        beta = gb[g0:g1, H_A + h:H_A + h + 1]
        gcol = gc_all[g0:g1, h:h + 1]
        glcol = gl_all[g0:g1, h:h + 1]
        decay = jnp.where(tri, jnp.exp(gcol - gc_t[h:h + 1, g0:g1]), 0.0)
        egc = jnp.exp(gcol)
        kbeta = k * beta
        qkk = _dot_nt(jnp.concatenate([q, kbeta], axis=0).astype(BF16), k.astype(BF16))
        qk = qkk[:SOLVE_ROWS] * decay
        p = -jnp.where(strict, qkk[SOLVE_ROWS:] * decay, 0.0)
        ps.append(p)
        tinvs.append(eye + p)
        rhs.append(jnp.concatenate([v * beta, kbeta * egc], axis=1).astype(BF16))
        qd = (q * egc).astype(BF16)
        kd = k * jnp.exp(glcol - gcol)
        egl = jnp.exp(glcol)
        for cc in range(chunks_per_group):
            c = grp * chunks_per_group + cc
            r0, r1 = cc * CHUNK, (cc + 1) * CHUNK
            wq_s[wr, h, c, CHUNK:2 * CHUNK, :] = qd[r0:r1]
            kdt_s[wr, h, c] = kd[r0:r1].T.astype(BF16)
            qk_s[wr, h, c] = qk[r0:r1, r0:r1].astype(BF16)
            egl_s[wr, h, c] = jnp.broadcast_to(egl[r0:r0 + 1, :], (SUBLANES, HEAD_DIM))

    gnw = gnw_ref[...]

    def recurrence_unit(c, h):
        r0, r1 = c * CHUNK, (c + 1) * CHUNK
        lo, hi = h * HEAD_DIM, (h + 1) * HEAD_DIM
        s = s_s[h]
        if c == 0:
            s = jnp.where(b_starts_seq, 0.0, s)
        ws = _dot(wq_s[rd, h, c], s.astype(BF16))
        v_new = u_s[rd, r0:r1, lo:hi] - ws[:CHUNK]
        vnb = v_new.astype(BF16)
        o = ws[CHUNK:] + _dot(qk_s[rd, h, c], vnb)
        s_s[h] = s * egl_s[rd, h, c][0:1] + _dot(kdt_s[rd, h, c], vnb)
        o_ref[r0:r1, lo:hi] = _gdn_out_norm(o, gnw, z_ref[r0:r1, lo:hi])

    side_work = []
    for c in range(n_c):
        for h in range(H_A):
            side_work.append(functools.partial(recurrence_unit, c, h))
            side_work.extend([functools.partial(reduce_cache_blocks, 1)] * 2)
    n_iter = 5
    n_slots = 2 * n_iter * len(groups)
    done = [0]

    def side_slot(k):
        target = (k + 1) * len(side_work) // n_slots
        while done[0] < target:
            side_work[done[0]]()
            done[0] += 1

    slot = 0
    for it in range(n_iter):
        for i in range(len(groups)):
            pb = ps[i].astype(BF16)
            ps[i] = _dot(pb, pb)
            side_slot(slot)
            slot += 1
        for i in range(len(groups)):
            tinvs[i] = tinvs[i] + _dot(tinvs[i].astype(BF16), ps[i].astype(BF16))
            side_slot(slot)
            slot += 1
    for i, (h, grp) in enumerate(groups):
        lo, hi = h * HEAD_DIM, (h + 1) * HEAD_DIM
        g0, g1 = grp * SOLVE_ROWS, (grp + 1) * SOLVE_ROWS
        uw = _dot(tinvs[i].astype(BF16), rhs[i])
        u_s[wr, g0:g1, lo:hi] = uw[:, :HEAD_DIM]
        w = uw[:, HEAD_DIM:].astype(BF16)
        for cc in range(chunks_per_group):
            c = grp * chunks_per_group + cc
            wq_s[wr, h, c, 0:CHUNK, :] = w[cc * CHUNK:(cc + 1) * CHUNK]
    reduce_cache_blocks(len(pending_blocks))
    sfin_ref[0] = s_s[...]


def _gdn_prompt(qkva, gb, z, cw, gnw, cache_k, page_table, batch, seq, tl):
    n_t = seq // tl
    n_c = tl // CHUNK
    n_tiles = batch * n_t
    dec_batch, pages_per_row = page_table.shape
    n_pages = dec_batch * pages_per_row // n_tiles
    steps_per_row = pages_per_row // n_pages
    assert n_pages * n_tiles == dec_batch * pages_per_row and steps_per_row * n_pages == pages_per_row
    blocks_per_step = n_pages // PAGES_PER_BLOCK
    width = H_B * HEAD_DIM
    cache_pairs = cache_k.reshape(1, cache_k.shape[1], PAGE_SIZE // 2, 2 * H_B, HEAD_DIM)
    tile_a = lambda s: jnp.minimum(s, n_tiles - 1)
    tile_b = lambda s: jnp.maximum(s - 1, 0)
    row_a = lambda s, pt: (tile_a(s), 0)
    row_b = lambda s, pt: (tile_b(s), 0)
    const = lambda shape: pl.BlockSpec(shape, lambda s, pt: (0,) * len(shape), pipeline_mode=pl.Buffered(1))

    def psum_index(s, pt):
        a = tile_a(s)
        return (a // steps_per_row, a % steps_per_row, 0)

    return pl.pallas_call(
        functools.partial(_gdn_prompt_kernel, tl=tl, n_t=n_t, n_tiles=n_tiles, n_pages=n_pages,
                          steps_per_row=steps_per_row),
        grid_spec=pltpu.PrefetchScalarGridSpec(
            num_scalar_prefetch=1, grid=(n_tiles + 1,),
            in_specs=[pl.BlockSpec((tl, GDN_QKV), row_a), pl.BlockSpec((tl, AB_PAD), row_a),
                      pl.BlockSpec((tl, Z_COLS), row_b), const((CONV_W, GDN_QKV)), const((1, HEAD_DIM)),
                      pl.BlockSpec(memory_space=pl.ANY)],
            out_specs=[pl.BlockSpec((tl, Z_COLS), row_b),
                       pl.BlockSpec((1, H_A, HEAD_DIM, HEAD_DIM), lambda s, pt: (tile_b(s) // n_t, 0, 0, 0)),
                       pl.BlockSpec((1, blocks_per_step, width), psum_index)],
            scratch_shapes=[pltpu.VMEM((tl + SUBLANES, GDN_QKV), F32),
                            pltpu.VMEM((2, tl, Z_COLS), F32),
                            pltpu.VMEM((2, H_A, n_c, 2 * CHUNK, HEAD_DIM), BF16),
                            pltpu.VMEM((2, H_A, n_c, HEAD_DIM, CHUNK), BF16),
                            pltpu.VMEM((2, H_A, n_c, CHUNK, CHUNK), BF16),
                            pltpu.VMEM((2, H_A, n_c, SUBLANES, HEAD_DIM), F32),
                            pltpu.VMEM((H_A, HEAD_DIM, HEAD_DIM), F32),
                            pltpu.VMEM((2, n_pages, PAGE_SIZE // 2, 2 * H_B, HEAD_DIM), F32),
                            pltpu.SemaphoreType.DMA((2,))]),
        out_shape=[jax.ShapeDtypeStruct((batch * seq, Z_COLS), F32),
                   jax.ShapeDtypeStruct((batch, H_A, HEAD_DIM, HEAD_DIM), F32),
                   jax.ShapeDtypeStruct((dec_batch, pages_per_row // PAGES_PER_BLOCK, width), F32)],
        compiler_params=_params(("arbitrary",)),
        name="gdn_prompt",
    )(page_table, qkva, gb, z, cw, gnw, cache_pairs)


def _row0(x, rows=SUBLANES):
    r = lax.broadcasted_iota(jnp.int32, (rows, x.shape[1]), 0)
    return jnp.where(r == 0, jnp.broadcast_to(x, (rows, x.shape[1])), 0.0)


GDN_SAMPLE_ROWS = SUBLANES


def _gdn_sample_kernel(xn_ref, p0_ref, p1_ref, p2_ref, gb_ref, z_ref, cw_ref, gnw_ref, s0_ref,
                       o_ref, sout_ref):
    r0 = pl.multiple_of(pl.program_id(0) * GDN_SAMPLE_ROWS, GDN_SAMPLE_ROWS)
    rows = lambda r: r[pl.ds(r0, GDN_SAMPLE_ROWS), :]
    cw = cw_ref[...]
    y = rows(p0_ref) * cw[0:1]
    y = y + rows(p1_ref) * cw[1:2]
    y = y + rows(p2_ref) * cw[2:3]
    y = y + rows(xn_ref) * cw[3:4]
    y = _silu(y)
    qk = _gdn_head_norms(y)
    gbr = rows(gb_ref)
    zr = rows(z_ref)
    gnw = gnw_ref[...]
    ri = lax.broadcasted_iota(jnp.int32, (HEAD_DIM, HEAD_DIM), 0)
    ci = lax.broadcasted_iota(jnp.int32, (HEAD_DIM, HEAD_DIM), 1)
    eye = (ri == ci).astype(BF16)
    pad = 2 * SUBLANES
    for j in range(GDN_SAMPLE_ROWS):
        for h in range(H_A):
            lo, hi = h * HEAD_DIM, (h + 1) * HEAD_DIM
            q, k = qk[h][j:j + 1], qk[H_A + h][j:j + 1]
            v = y[j:j + 1, 2 * H_A * HEAD_DIM + lo:2 * H_A * HEAD_DIM + hi]
            eg = jnp.exp(gbr[j:j + 1, h:h + 1])
            beta = gbr[j:j + 1, H_A + h:H_A + h + 1]
            s0 = s0_ref[j, h]
            kp = _row0(k, pad).astype(BF16)
            v_new = beta * (v - eg * _dot(kp, s0.astype(BF16))[0:1])
            k_col = _dot_nt(eye, kp)[:, 0:1]
            s_new = s0 * eg + k_col * v_new
            o = _dot(_row0(q, pad).astype(BF16), s_new.astype(BF16))[0:1]
            o_ref[j, :, lo:hi] = _gdn_out_norm(o, gnw, zr[j:j + 1, lo:hi])
            sout_ref[j, h] = s_new


def _gdn_sample(xn, p0, p1, p2, gb, z, cw, gnw, s0):
    nb = xn.shape[0]
    full = lambda a: _const_spec(a.shape)
    st = pl.BlockSpec((GDN_SAMPLE_ROWS, H_A, HEAD_DIM, HEAD_DIM), lambda b: (b, 0, 0, 0))
    return pl.pallas_call(
        _gdn_sample_kernel,
        grid=(nb // GDN_SAMPLE_ROWS,),
        in_specs=[full(xn), full(p0), full(p1), full(p2), full(gb), full(z), full(cw), full(gnw), st],
        out_specs=[pl.BlockSpec((GDN_SAMPLE_ROWS, 1, Z_COLS), lambda b: (b, 0, 0)), st],
        out_shape=[jax.ShapeDtypeStruct((nb, 1, Z_COLS), F32),
                   jax.ShapeDtypeStruct(s0.shape, F32)],
        compiler_params=_params(("parallel",)),
        name="gdn_sample",
    )(xn, p0, p1, p2, gb, z, cw, gnw, s0)


def _moba_prompt_kernel(q_ref, k_ref, v_ref, o_ref, *, seq):
    nb = seq // MOBA_BLOCK
    k = k_ref[...]
    kb = k.astype(BF16)
    v_t = v_ref[...].T.astype(BF16)
    km = jnp.concatenate(
        [jnp.sum(k[j * MOBA_BLOCK:(j + 1) * MOBA_BLOCK], axis=0, keepdims=True) * (1.0 / MOBA_BLOCK)
         for j in range(nb)], axis=0)
    blk = lax.broadcasted_iota(jnp.int32, (nb, MOBA_BLOCK), 0)
    ki = lax.broadcasted_iota(jnp.int32, (MOBA_BLOCK, MOBA_BLOCK), 0)
    qi = lax.broadcasted_iota(jnp.int32, (MOBA_BLOCK, MOBA_BLOCK), 1)
    causal_bias = jnp.where(ki <= qi, 0.0, NEG_INF)
    scale = HEAD_DIM ** -0.5

    def logits(n):
        q = q_ref[n * MOBA_BLOCK:(n + 1) * MOBA_BLOCK, :]
        gate = _dot_nt(km, q, HIGHEST)
        qs = (q * scale).astype(BF16)
        cnt = jnp.zeros((nb, MOBA_BLOCK), jnp.int32)
        for i in range(n):
            gi = gate[i:i + 1, :]
            beats = (gi > gate) | ((gi == gate) & (i < blk))
            cnt = cnt + beats.astype(jnp.int32)
        sel_bias = jnp.where((cnt < MOBA_TOPK) & (blk < n), 0.0, NEG_INF)
        tiles = [_dot_nt(kb[j * MOBA_BLOCK:(j + 1) * MOBA_BLOCK], qs) + sel_bias[j:j + 1, :] for j in range(n)]
        tiles.append(_dot_nt(kb[n * MOBA_BLOCK:(n + 1) * MOBA_BLOCK], qs) + causal_bias)
        return tiles

    def attend(n, tiles):
        m = functools.reduce(jnp.maximum, [jnp.max(t, axis=0, keepdims=True) for t in tiles])
        ps = [jnp.exp(t - m) for t in tiles]
        l = functools.reduce(jnp.add, [jnp.sum(p, axis=0, keepdims=True) for p in ps])
        p = jnp.concatenate([x.astype(BF16) for x in ps], axis=0)
        o_t = _dot(v_t[:, :(n + 1) * MOBA_BLOCK], p) / l
        o_ref[n * MOBA_BLOCK:(n + 1) * MOBA_BLOCK, :] = o_t.T

    s_next = logits(0)
    for n in range(nb):
        s_cur = s_next
        if n + 1 < nb:
            s_next = logits(n + 1)
        attend(n, s_cur)


def _moba_prompt(qb, kb, vb, batch, seq):
    spec = pl.BlockSpec((seq, HEAD_DIM), lambda b, h: (b, h))
    return pl.pallas_call(
        functools.partial(_moba_prompt_kernel, seq=seq),
        grid=(batch, H_B),
        in_specs=[spec, spec, spec],
        out_specs=spec,
        out_shape=jax.ShapeDtypeStruct((batch * seq, H_B * HEAD_DIM), F32),
        compiler_params=_params(("parallel", "parallel")),
        name="moba_prompt",
    )(qb, kb, vb)


def _moba_select_kernel(q_ref, ps_ref, idx_ref):
    b = pl.program_id(0)
    q = q_ref[pl.ds(b, 1), :]
    km = ps_ref[0] * (1.0 / MOBA_BLOCK)
    n_blocks = km.shape[0]
    row = lax.broadcasted_iota(jnp.int32, (SUBLANES, LANES), 0)
    lane = lax.broadcasted_iota(jnp.int32, (SUBLANES, LANES), 1)
    blk = lax.broadcasted_iota(jnp.int32, (1, n_blocks), 1).astype(F32)
    out = jnp.zeros((SUBLANES, LANES), F32)
    for h in range(H_B):
        lo, hi = h * HEAD_DIM, (h + 1) * HEAD_DIM
        g = _dot_nt(_row0(q[:, lo:hi]), km[:, lo:hi], HIGHEST)[0:1]
        for s in range(MOBA_TOPK):
            m = jnp.max(g, axis=1, keepdims=True)
            idx = jnp.min(jnp.where(g == m, blk, float(n_blocks)), axis=1, keepdims=True)
            out = jnp.where((row == h) & (lane == s), idx, out)
            g = jnp.where(blk == idx, NEG_INF, g)
    idx_ref[0] = out.astype(jnp.int32)


def _moba_select(qb, psums):
    nb, n_blocks, width = psums.shape
    return pl.pallas_call(
        _moba_select_kernel,
        grid=(nb,),
        in_specs=[_const_spec(qb.shape), pl.BlockSpec((1, n_blocks, width), lambda b: (b, 0, 0))],
        out_specs=pl.BlockSpec((1, SUBLANES, LANES), lambda b: (b, 0, 0)),
        out_shape=jax.ShapeDtypeStruct((nb, SUBLANES, LANES), jnp.int32),
        compiler_params=_params(("parallel",)),
        name="moba_select",
    )(qb, psums)


N_SEL_PAGES = MOBA_TOPK * PAGES_PER_BLOCK


def _moba_attend_kernel(idx_ref, pt_ref, *refs):
    n_pages = H_B * N_SEL_PAGES
    kp, vp = refs[:n_pages], refs[n_pages:2 * n_pages]
    q_ref, k_ref, v_ref, o_ref = refs[2 * n_pages:]
    b = pl.program_id(0)
    q = q_ref[pl.ds(b, 1), :]
    k_new = k_ref[pl.ds(b, 1), :]
    v_new = v_ref[pl.ds(b, 1), :]
    scale = HEAD_DIM ** -0.5
    for h in range(H_B):
        lo, hi = h * HEAD_DIM, (h + 1) * HEAD_DIM
        pages = range(h * N_SEL_PAGES, (h + 1) * N_SEL_PAGES)
        keys = jnp.concatenate([kp[i][0, 0, :, h, :] for i in pages], axis=0)
        vals = jnp.concatenate([vp[i][0, 0, :, h, :] for i in pages], axis=0)
        qh = q[:, lo:hi]
        s = _dot_nt(_row0(qh, 2 * SUBLANES).astype(BF16), keys.astype(BF16))[0:1] * scale
        s_new = jnp.sum(qh * k_new[:, lo:hi], axis=1, keepdims=True) * scale
        m = jnp.maximum(jnp.max(s, axis=1, keepdims=True), s_new)
        p = jnp.exp(s - m)
        p_new = jnp.exp(s_new - m)
        l = jnp.sum(p, axis=1, keepdims=True) + p_new
        pv = _dot(_row0(p, 2 * SUBLANES).astype(BF16), vals.astype(BF16))[0:1]
        o_ref[0, :, lo:hi] = (pv + p_new * v_new[:, lo:hi]) / l


def _moba_attend(idx_flat, page_table, cache_k, cache_v, qb, kb, vb):
    nb = qb.shape[0]

    def page_spec(h, s, p):
        def index(b, idx, pt):
            blk = idx[(b * H_B + h) * MOBA_TOPK + s]
            return (0, pt[b, blk * PAGES_PER_BLOCK + p], 0, 0, 0)
        return pl.BlockSpec((1, 1, PAGE_SIZE, H_B, HEAD_DIM), index)

    page_specs = [page_spec(h, s, p) for h in range(H_B) for s in range(MOBA_TOPK)
                  for p in range(PAGES_PER_BLOCK)]
    n_pages = len(page_specs)
    rows = pl.BlockSpec((nb, H_B * HEAD_DIM), lambda b, idx, pt: (0, 0))
    out = pl.pallas_call(
        _moba_attend_kernel,
        grid_spec=pltpu.PrefetchScalarGridSpec(
            num_scalar_prefetch=2, grid=(nb,),
            in_specs=page_specs + page_specs + [rows, rows, rows],
            out_specs=pl.BlockSpec((1, 1, H_B * HEAD_DIM), lambda b, idx, pt: (b, 0, 0))),
        out_shape=jax.ShapeDtypeStruct((nb, 1, H_B * HEAD_DIM), F32),
        compiler_params=_params(("parallel",)),
        name="moba_attend",
    )(idx_flat, page_table, *([cache_k] * n_pages), *([cache_v] * n_pages), qb, kb, vb)
    return out.reshape(nb, H_B * HEAD_DIM)


def _out_ffn_kernel(x_ref, oa_ref, ob_ref, woa_ref, wob_ref, n2_ref, wg_ref, wu_ref, wd_ref, fn_ref, y_ref):
    x1 = (x_ref[...] + _dot(oa_ref[...].astype(BF16), woa_ref[...])
          + _dot(ob_ref[...].astype(BF16), wob_ref[...]))
    h = _rmsnorm(x1, n2_ref[...]).astype(BF16)
    act = _silu(_dot(h, wg_ref[...])) * _dot(h, wu_ref[...])
    x2 = x1 + _dot(act.astype(BF16), wd_ref[...])
    y_ref[...] = _rmsnorm(x2, fn_ref[...])


def _out_ffn(x, oa, ob, woa, wob, n2, wg, wu, wd, fn, tm):
    t = x.shape[0]
    row = lambda i: (i, 0)
    full = lambda a: _const_spec(a.shape)
    return pl.pallas_call(
        _out_ffn_kernel,
        grid=(t // tm,),
        in_specs=[pl.BlockSpec((tm, D_MODEL), row), pl.BlockSpec((tm, Z_COLS), row),
                  pl.BlockSpec((tm, H_B * HEAD_DIM), row), full(woa), full(wob), full(n2),
                  full(wg), full(wu), full(wd), full(fn)],
        out_specs=pl.BlockSpec((tm, D_MODEL), row),
        out_shape=jax.ShapeDtypeStruct((t, D_MODEL), F32),
        compiler_params=_params(("parallel",)),
        name="out_ffn",
    )(x, oa, ob, woa, wob, n2, wg, wu, wd, fn)


def _rotary_tables(pos):
    half = ROT_DIM // 2
    inv_freq = ROPE_THETA ** (-jnp.arange(half, dtype=F32) / half)
    ang = pos.astype(F32)[:, None] * inv_freq[None, :]
    cos, sin = jnp.cos(ang), jnp.sin(ang)
    n = pos.shape[0]
    rest = HEAD_DIM - ROT_DIM
    cos_t = jnp.concatenate([cos, cos, jnp.ones((n, rest), F32)], axis=1)
    zeros_h = jnp.zeros((n, half), F32)
    sin_a = jnp.concatenate([-sin, zeros_h, jnp.zeros((n, rest), F32)], axis=1)
    sin_b = jnp.concatenate([zeros_h, sin, jnp.zeros((n, rest), F32)], axis=1)
    return cos_t, sin_a, sin_b


def kernel(x_prompt, x_sample, cache_k, cache_v, page_table, state_gdn, state_conv, norm1_w, w_in, conv_w,
           a_log, dt_bias, gdn_norm_w, w_out, norm2_w, w_gate, w_up, w_down, final_norm_w):
    depth = w_in.shape[0]
    assert depth == 1, "single-layer trunk"
    batch, seq, _ = x_prompt.shape
    dec_batch, dec_seq, _ = x_sample.shape
    assert dec_seq == 1
    n_phys = cache_k.shape[1]
    past_len = page_table.shape[1] * PAGE_SIZE

    wi = w_in[0]
    ab0 = GDN_QKV + Z_COLS
    w_all = jnp.concatenate(
        [wi[:, :ab0], wi[:, ab0 + 2 * H_A:], wi[:, ab0:ab0 + 2 * H_A],
         jnp.zeros((D_MODEL, AB_PAD - 2 * H_A), wi.dtype)], axis=1).astype(BF16)
    gpar = jnp.zeros((SUBLANES, LANES), F32)
    gpar = gpar.at[0, :H_A].set(a_log[0].astype(F32)).at[1, :H_A].set(dt_bias[0].astype(F32))
    nw1 = norm1_w[0].reshape(1, D_MODEL)
    cw = conv_w[0]
    gnw = gdn_norm_w[0].reshape(1, HEAD_DIM)
    woa = w_out[0][:Z_COLS].astype(BF16)
    wob = w_out[0][Z_COLS:].astype(BF16)
    n2 = norm2_w[0].reshape(1, D_MODEL)
    wg, wu, wd = w_gate[0].astype(BF16), w_up[0].astype(BF16), w_down[0].astype(BF16)
    fn = final_norm_w.reshape(1, D_MODEL)

    xp = x_prompt.reshape(batch * seq, D_MODEL)
    tabs_p = _rotary_tables(jnp.arange(seq, dtype=jnp.int32))
    qkva_p, z_p, gb_p, qb_p, kb_p, vb_p, k5_p, v5_p = _in_proj(xp, nw1, w_all, gpar, *tabs_p,
                                                               batch=batch, seq=seq, tm=512)
    oa_p, gdn_p, psums = _gdn_prompt(qkva_p, gb_p, z_p, cw, gnw, cache_k, page_table, batch, seq, tl=256)
    ob_p = _moba_prompt(qb_p, kb_p, vb_p, batch, seq)
    y_p = _out_ffn(xp, oa_p, ob_p, woa, wob, n2, wg, wu, wd, fn, tm=512)

    xs = x_sample.reshape(dec_batch, D_MODEL)
    tabs_s = _rotary_tables(jnp.full((dec_batch,), past_len, jnp.int32))
    qkva_s, z_s, gb_s, qb_s, kb_s, vb_s, k5_s, v5_s = _in_proj(xs, nw1, w_all, gpar, *tabs_s,
                                                               batch=1, seq=dec_batch, tm=dec_batch)
    sc = state_conv[0]
    oa_s, gdn_s = _gdn_sample(qkva_s, sc[:, 0], sc[:, 1], sc[:, 2], gb_s, z_s, cw, gnw, state_gdn[0])
    idx = _moba_select(qb_s, psums)
    idx_flat = idx[:, :H_B, :MOBA_TOPK].reshape(-1)
    ob_s = _moba_attend(idx_flat, page_table, cache_k, cache_v, qb_s, kb_s, vb_s)
    y_s = _out_ffn(xs, oa_s.reshape(dec_batch, Z_COLS), ob_s, woa, wob, n2, wg, wu, wd, fn, tm=dec_batch)

    conv_p = qkva_p.reshape(batch, seq, GDN_QKV)[:, seq - (CONV_W - 1):]
    conv_s = jnp.concatenate([sc[:, 1:], qkva_s[:, None, :]], axis=1)
    return (y_p.reshape(batch, seq, D_MODEL), y_s.reshape(dec_batch, 1, D_MODEL),
            k5_p, v5_p,
            k5_s.reshape(1, dec_batch, 1, H_B, HEAD_DIM), v5_s.reshape(1, dec_batch, 1, H_B, HEAD_DIM),
            gdn_p[None], gdn_s[None], conv_p[None], conv_s[None])
```

```python
import functools

import jax
import jax.numpy as jnp
from jax import lax
from jax.experimental import pallas as pl
from jax.experimental.pallas import tpu as pltpu

D_MODEL = 1024
HEAD_DIM = 128
H_A = 4
H_B = 4
GDN_QKV = 3 * H_A * HEAD_DIM
MOBA_QKV = 3 * H_B * HEAD_DIM
Z_COLS = H_A * HEAD_DIM
CONV_W = 4
CHUNK = 64
SOLVE_ROWS = 128
MOBA_BLOCK = 256
MOBA_TOPK = 3
PAGE_SIZE = 128
PAGES_PER_BLOCK = MOBA_BLOCK // PAGE_SIZE
ROT_DIM = HEAD_DIM // 4
ROPE_THETA = 500000.0
EPS = 1e-6

LANES = 128
SUBLANES = 8
AB_PAD = LANES
PROJ_COLS = GDN_QKV + Z_COLS + MOBA_QKV + AB_PAD
VMEM_LIMIT = 56 * 1024 * 1024

F32 = jnp.float32
BF16 = jnp.bfloat16
HIGHEST = lax.Precision.HIGHEST
NEG_INF = float("-inf")


def _dot(a, b, precision=None):
    return jnp.dot(a, b, precision=precision, preferred_element_type=F32)


def _dot_nt(a, b, precision=None):
    return lax.dot_general(a, b, (((1,), (1,)), ((), ())), precision=precision,
                           preferred_element_type=F32)


def _rmsnorm(x, w):
    return x * lax.rsqrt(jnp.mean(x * x, axis=-1, keepdims=True) + EPS) * w


def _l2norm(x):
    return x * lax.rsqrt(jnp.sum(x * x, axis=-1, keepdims=True) + EPS)


def _silu(x):
    return x * jax.nn.sigmoid(x)


def _const_spec(shape):
    return pl.BlockSpec(shape, lambda *_: (0,) * len(shape), pipeline_mode=pl.Buffered(1))


def _params(semantics):
    return pltpu.CompilerParams(dimension_semantics=semantics, vmem_limit_bytes=VMEM_LIMIT)


def _in_proj_kernel(x_ref, nw_ref, w_ref, gpar_ref, cos_ref, sa_ref, sb_ref,
                    qkva_ref, z_ref, gb_ref, qb_ref, kb_ref, vb_ref, k5_ref, v5_ref):
    xn = _rmsnorm(x_ref[...], nw_ref[...]).astype(BF16)
    b0 = GDN_QKV + Z_COLS
    pb = _dot(xn, w_ref[:, b0:b0 + MOBA_QKV])
    pa = _dot(xn, w_ref[:, :GDN_QKV])
    cos, sa, sb = cos_ref[...], sa_ref[...], sb_ref[...]
    half = ROT_DIM // 2

    def rot(t):
        return (t * cos + pltpu.roll(t, HEAD_DIM - half, 1) * sa + pltpu.roll(t, half, 1) * sb)

    for h in range(H_B):
        lo, hi = h * HEAD_DIM, (h + 1) * HEAD_DIM
        qb_ref[:, lo:hi] = rot(pb[:, lo:hi])
        kh = rot(pb[:, H_B * HEAD_DIM + lo:H_B * HEAD_DIM + hi])
        vh = pb[:, 2 * H_B * HEAD_DIM + lo:2 * H_B * HEAD_DIM + hi]
        kb_ref[:, lo:hi] = kh
        vb_ref[:, lo:hi] = vh
        k5_ref[0, 0, :, h, :] = kh
        v5_ref[0, 0, :, h, :] = vh
    pz = _dot(xn, w_ref[:, GDN_QKV:b0])
    ab = _dot(xn, w_ref[:, b0 + MOBA_QKV:])
    qkva_ref[...] = pa
    z_ref[...] = pz
    lane = lax.broadcasted_iota(jnp.int32, ab.shape, 1)
    g = -jnp.exp(gpar_ref[0:1, :]) * jax.nn.softplus(ab + gpar_ref[1:2, :])
    gb_ref[...] = jnp.where(lane < H_A, g, jax.nn.sigmoid(ab))


def _in_proj(x, nw, w_all, gpar, cos, sa, sb, batch, seq, tm):
    t = batch * seq
    n_t = seq // tm
    row = lambda i: (i, 0)
    tab = lambda i: (i % n_t, 0)
    kv = lambda i: (0, i // n_t, i % n_t, 0, 0)
    widths = (GDN_QKV, Z_COLS, AB_PAD, H_B * HEAD_DIM, H_B * HEAD_DIM, H_B * HEAD_DIM)
    kv_shape = jax.ShapeDtypeStruct((1, batch, seq, H_B, HEAD_DIM), F32)
    kv_spec = pl.BlockSpec((1, 1, tm, H_B, HEAD_DIM), kv)
    return pl.pallas_call(
        _in_proj_kernel,
        grid=(t // tm,),
        in_specs=[pl.BlockSpec((tm, D_MODEL), row), _const_spec((1, D_MODEL)),
                  _const_spec((D_MODEL, PROJ_COLS)), _const_spec((SUBLANES, LANES)),
                  pl.BlockSpec((tm, LANES), tab), pl.BlockSpec((tm, LANES), tab),
                  pl.BlockSpec((tm, LANES), tab)],
        out_specs=[pl.BlockSpec((tm, w), row) for w in widths] + [kv_spec, kv_spec],
        out_shape=[jax.ShapeDtypeStruct((t, w), F32) for w in widths] + [kv_shape, kv_shape],
        compiler_params=_params(("parallel",)),
        name="in_proj",
    )(x, nw, w_all, gpar, cos, sa, sb)


def _gdn_head_norms(y):
    parts = []
    for h in range(H_A):
        parts.append(_l2norm(y[:, h * HEAD_DIM:(h + 1) * HEAD_DIM]) * (HEAD_DIM ** -0.5))
    for h in range(H_A, 2 * H_A):
        parts.append(_l2norm(y[:, h * HEAD_DIM:(h + 1) * HEAD_DIM]))
    return parts


def _gdn_out_norm(o, gnw, z):
    return _rmsnorm(o, gnw) * _silu(z)


def _sum_rows(x):
    while x.shape[0] > 1:
        half = x.shape[0] // 2
        x = x[:half] + x[half:]
    return x[0]


def _block_key_sum(pages_ref, r, out_ref):
    i = r * PAGES_PER_BLOCK
    s = _sum_rows(pages_ref[i])
    for j in range(1, PAGES_PER_BLOCK):
        s = s + _sum_rows(pages_ref[i + j])
    s = s[:H_B] + s[H_B:]
    for h in range(H_B):
        out_ref[0, r:r + 1, h * HEAD_DIM:(h + 1) * HEAD_DIM] = s[h:h + 1, :]


def _gdn_prompt_kernel(pt_ref, qkva_ref, gb_ref, z_ref, cw_ref, gnw_ref, cache_ref, *refs,
                       tl, n_t, n_tiles, n_pages, steps_per_row):
    step = pl.program_id(0)
    o_ref, sfin_ref, psum_ref, xbuf, u_s, wq_s, kdt_s, qk_s, egl_s, s_s, pbuf, psem = refs

    def page_copy(tile, i, half):
        page = pt_ref[tile // steps_per_row, (tile % steps_per_row) * n_pages + i]
        return pltpu.make_async_copy(cache_ref.at[0, page], pbuf.at[half, i], psem.at[half])

    @pl.when(step == 0)
    def _():
        xbuf[0:SUBLANES, :] = jnp.zeros((SUBLANES, GDN_QKV), F32)
        u_s[...] = jnp.zeros_like(u_s)
        wq_s[...] = jnp.zeros_like(wq_s)
        kdt_s[...] = jnp.zeros_like(kdt_s)
        qk_s[...] = jnp.zeros_like(qk_s)
        egl_s[...] = jnp.zeros_like(egl_s)
        s_s[...] = jnp.zeros_like(s_s)
        for i in range(n_pages):
            page_copy(0, i, 0).start()

    for parity in range(2):
        @pl.when(step % 2 == parity)
        def _():
            nxt = jnp.minimum(step + 1, n_tiles - 1)
            for i in range(n_pages):
                page_copy(nxt, i, 1 - parity).start()

            def wait_pages():
                for i in range(n_pages):
                    page_copy(0, i, parity).wait()

            _gdn_prompt_step(qkva_ref, gb_ref, z_ref, cw_ref, gnw_ref, o_ref, sfin_ref, psum_ref,
                             xbuf, u_s, wq_s, kdt_s, qk_s, egl_s, s_s, pbuf.at[parity], wait_pages,
                             tl=tl, n_t=n_t, n_pages=n_pages, wr=parity)

    @pl.when(step == n_tiles)
    def _():
        for i in range(n_pages):
            page_copy(0, i, (n_tiles + 1) % 2).wait()


def _gdn_prompt_step(qkva_ref, gb_ref, z_ref, cw_ref, gnw_ref, o_ref, sfin_ref, psum_ref,
                     xbuf, u_s, wq_s, kdt_s, qk_s, egl_s, s_s, pages, wait_pages, *, tl, n_t, n_pages, wr):
    wait_pages()
    step = pl.program_id(0)
    rd = 1 - wr
    a_starts_seq = (step % n_t) == 0
    b_starts_seq = ((step + n_t - 1) % n_t) == 0
    n_c = tl // CHUNK
    pending_blocks = list(range(n_pages // PAGES_PER_BLOCK))

    def reduce_cache_blocks(count):
        for _ in range(min(count, len(pending_blocks))):
            _block_key_sum(pages, pending_blocks.pop(0), psum_ref)

    tail = xbuf[0:SUBLANES, :]
    xbuf[0:SUBLANES, :] = jnp.where(a_starts_seq, 0.0, tail)
    xbuf[SUBLANES:SUBLANES + tl, :] = qkva_ref[...]
    cw = cw_ref[...]
    xs = xbuf[...]
    y = pltpu.roll(xs, CONV_W - 1, 0)[SUBLANES:] * cw[0:1]
    for j in range(1, CONV_W - 1):
        y = y + pltpu.roll(xs, CONV_W - 1 - j, 0)[SUBLANES:] * cw[j:j + 1]
    y = _silu(y + xs[SUBLANES:] * cw[CONV_W - 1:CONV_W])
    xbuf[0:SUBLANES, :] = xbuf[tl:tl + SUBLANES, :]
    qk_parts = _gdn_head_norms(y)

    ri = lax.broadcasted_iota(jnp.int32, (tl, tl), 0)
    ci = lax.broadcasted_iota(jnp.int32, (tl, tl), 1)
    same = (ri // CHUNK) == (ci // CHUNK)
    gb = gb_ref[...]
    sums = _dot(jnp.concatenate([(same & (ri >= ci)).astype(F32), same.astype(F32)], axis=0), gb, HIGHEST)
    gc_all, gl_all = sums[:tl], sums[tl:]
    gc_t = gc_all.T

    rg = lax.broadcasted_iota(jnp.int32, (SOLVE_ROWS, SOLVE_ROWS), 0)
    cg = lax.broadcasted_iota(jnp.int32, (SOLVE_ROWS, SOLVE_ROWS), 1)
    same_g = (rg // CHUNK) == (cg // CHUNK)
    tri = same_g & (rg >= cg)
    strict = same_g & (rg > cg)
    eye = (rg == cg).astype(F32)
    chunks_per_group = SOLVE_ROWS // CHUNK
    groups = [(h, grp) for h in range(H_A) for grp in range(tl // SOLVE_ROWS)]
    ps, tinvs, rhs = [], [], []
    for h, grp in groups:
        lo, hi = h * HEAD_DIM, (h + 1) * HEAD_DIM
        g0, g1 = grp * SOLVE_ROWS, (grp + 1) * SOLVE_ROWS
        q, k = qk_parts[h][g0:g1], qk_parts[H_A + h][g0:g1]
        v = y[g0:g1, 2 * H_A * HEAD_DIM + lo:2 * H_A * HEAD_DIM + hi]
        beta = gb[g0:g1, H_A + h:H_A + h + 1]
        gcol = gc_all[g0:g1, h:h + 1]
        glcol = gl_all[g0:g1, h:h + 1]
        decay = jnp.where(tri, jnp.exp(gcol - gc_t[h:h + 1, g0:g1]), 0.0)
        egc = jnp.exp(gcol)
        kbeta = k * beta
        qkk = _dot_nt(jnp.concatenate([q, kbeta], axis=0).astype(BF16), k.astype(BF16))
        qk = qkk[:SOLVE_ROWS] * decay
        p = -jnp.where(strict, qkk[SOLVE_ROWS:] * decay, 0.0)
        ps.append(p)
        tinvs.append(eye + p)
        rhs.append(jnp.concatenate([v * beta, kbeta * egc], axis=1).astype(BF16))
        qd = (q * egc).astype(BF16)
        kd = k * jnp.exp(glcol - gcol)
        egl = jnp.exp(glcol)
        for cc in range(chunks_per_group):
            c = grp * chunks_per_group + cc
            r0, r1 = cc * CHUNK, (cc + 1) * CHUNK
            wq_s[wr, h, c, CHUNK:2 * CHUNK, :] = qd[r0:r1]
            kdt_s[wr, h, c] = kd[r0:r1].T.astype(BF16)
            qk_s[wr, h, c] = qk[r0:r1, r0:r1].astype(BF16)
            egl_s[wr, h, c] = jnp.broadcast_to(egl[r0:r0 + 1, :], (SUBLANES, HEAD_DIM))

    gnw = gnw_ref[...]

    def recurrence_unit(c, h):
        r0, r1 = c * CHUNK, (c + 1) * CHUNK
        lo, hi = h * HEAD_DIM, (h + 1) * HEAD_DIM
        s = s_s[h]
        if c == 0:
            s = jnp.where(b_starts_seq, 0.0, s)
        ws = _dot(wq_s[rd, h, c], s.astype(BF16))
        v_new = u_s[rd, r0:r1, lo:hi] - ws[:CHUNK]
        vnb = v_new.astype(BF16)
        o = ws[CHUNK:] + _dot(qk_s[rd, h, c], vnb)
        s_s[h] = s * egl_s[rd, h, c][0:1] + _dot(kdt_s[rd, h, c], vnb)
        o_ref[r0:r1, lo:hi] = _gdn_out_norm(o, gnw, z_ref[r0:r1, lo:hi])

    side_work = []
    for c in range(n_c):
        for h in range(H_A):
            side_work.append(functools.partial(recurrence_unit, c, h))
            side_work.extend([functools.partial(reduce_cache_blocks, 1)] * 2)
    n_iter = 5
    n_slots = 2 * n_iter * len(groups)
    done = [0]

    def side_slot(k):
        target = (k + 1) * len(side_work) // n_slots
        while done[0] < target:
            side_work[done[0]]()
            done[0] += 1

    slot = 0
    for it in range(n_iter):
        for i in range(len(groups)):
            pb = ps[i].astype(BF16)
            ps[i] = _dot(pb, pb)
            side_slot(slot)
            slot += 1
        for i in range(len(groups)):
            tinvs[i] = tinvs[i] + _dot(tinvs[i].astype(BF16), ps[i].astype(BF16))
            side_slot(slot)
            slot += 1
    for i, (h, grp) in enumerate(groups):
        lo, hi = h * HEAD_DIM, (h + 1) * HEAD_DIM
        g0, g1 = grp * SOLVE_ROWS, (grp + 1) * SOLVE_ROWS
        uw = _dot(tinvs[i].astype(BF16), rhs[i])
        u_s[wr, g0:g1, lo:hi] = uw[:, :HEAD_DIM]
        w = uw[:, HEAD_DIM:].astype(BF16)
        for cc in range(chunks_per_group):
            c = grp * chunks_per_group + cc
            wq_s[wr, h, c, 0:CHUNK, :] = w[cc * CHUNK:(cc + 1) * CHUNK]
    reduce_cache_blocks(len(pending_blocks))
    sfin_ref[0] = s_s[...]


def _gdn_prompt(qkva, gb, z, cw, gnw, cache_k, page_table, batch, seq, tl):
    n_t = seq // tl
    n_c = tl // CHUNK
    n_tiles = batch * n_t
    dec_batch, pages_per_row = page_table.shape
    n_pages = dec_batch * pages_per_row // n_tiles
    steps_per_row = pages_per_row // n_pages
    assert n_pages * n_tiles == dec_batch * pages_per_row and steps_per_row * n_pages == pages_per_row
    blocks_per_step = n_pages // PAGES_PER_BLOCK
    width = H_B * HEAD_DIM
    cache_pairs = cache_k.reshape(1, cache_k.shape[1], PAGE_SIZE // 2, 2 * H_B, HEAD_DIM)
    tile_a = lambda s: jnp.minimum(s, n_tiles - 1)
    tile_b = lambda s: jnp.maximum(s - 1, 0)
    row_a = lambda s, pt: (tile_a(s), 0)
    row_b = lambda s, pt: (tile_b(s), 0)
    const = lambda shape: pl.BlockSpec(shape, lambda s, pt: (0,) * len(shape), pipeline_mode=pl.Buffered(1))

    def psum_index(s, pt):
        a = tile_a(s)
        return (a // steps_per_row, a % steps_per_row, 0)

    return pl.pallas_call(
        functools.partial(_gdn_prompt_kernel, tl=tl, n_t=n_t, n_tiles=n_tiles, n_pages=n_pages,
                          steps_per_row=steps_per_row),
        grid_spec=pltpu.PrefetchScalarGridSpec(
            num_scalar_prefetch=1, grid=(n_tiles + 1,),
            in_specs=[pl.BlockSpec((tl, GDN_QKV), row_a), pl.BlockSpec((tl, AB_PAD), row_a),
                      pl.BlockSpec((tl, Z_COLS), row_b), const((CONV_W, GDN_QKV)), const((1, HEAD_DIM)),
                      pl.BlockSpec(memory_space=pl.ANY)],
            out_specs=[pl.BlockSpec((tl, Z_COLS), row_b),
                       pl.BlockSpec((1, H_A, HEAD_DIM, HEAD_DIM), lambda s, pt: (tile_b(s) // n_t, 0, 0, 0)),
                       pl.BlockSpec((1, blocks_per_step, width), psum_index)],
            scratch_shapes=[pltpu.VMEM((tl + SUBLANES, GDN_QKV), F32),
                            pltpu.VMEM((2, tl, Z_COLS), F32),
                            pltpu.VMEM((2, H_A, n_c, 2 * CHUNK, HEAD_DIM), BF16),
                            pltpu.VMEM((2, H_A, n_c, HEAD_DIM, CHUNK), BF16),
                            pltpu.VMEM((2, H_A, n_c, CHUNK, CHUNK), BF16),
                            pltpu.VMEM((2, H_A, n_c, SUBLANES, HEAD_DIM), F32),
                            pltpu.VMEM((H_A, HEAD_DIM, HEAD_DIM), F32),
                            pltpu.VMEM((2, n_pages, PAGE_SIZE // 2, 2 * H_B, HEAD_DIM), F32),
                            pltpu.SemaphoreType.DMA((2,))]),
        out_shape=[jax.ShapeDtypeStruct((batch * seq, Z_COLS), F32),
                   jax.ShapeDtypeStruct((batch, H_A, HEAD_DIM, HEAD_DIM), F32),
                   jax.ShapeDtypeStruct((dec_batch, pages_per_row // PAGES_PER_BLOCK, width), F32)],
        compiler_params=_params(("arbitrary",)),
        name="gdn_prompt",
    )(page_table, qkva, gb, z, cw, gnw, cache_pairs)


def _row0(x, rows=SUBLANES):
    r = lax.broadcasted_iota(jnp.int32, (rows, x.shape[1]), 0)
    return jnp.where(r == 0, jnp.broadcast_to(x, (rows, x.shape[1])), 0.0)


GDN_SAMPLE_ROWS = SUBLANES


def _gdn_sample_kernel(xn_ref, p0_ref, p1_ref, p2_ref, gb_ref, z_ref, cw_ref, gnw_ref, s0_ref,
                       o_ref, sout_ref):
    r0 = pl.multiple_of(pl.program_id(0) * GDN_SAMPLE_ROWS, GDN_SAMPLE_ROWS)
    rows = lambda r: r[pl.ds(r0, GDN_SAMPLE_ROWS), :]
    cw = cw_ref[...]
    y = rows(p0_ref) * cw[0:1]
    y = y + rows(p1_ref) * cw[1:2]
    y = y + rows(p2_ref) * cw[2:3]
    y = y + rows(xn_ref) * cw[3:4]
    y = _silu(y)
    qk = _gdn_head_norms(y)
    gbr = rows(gb_ref)
    zr = rows(z_ref)
    gnw = gnw_ref[...]
    ri = lax.broadcasted_iota(jnp.int32, (HEAD_DIM, HEAD_DIM), 0)
    ci = lax.broadcasted_iota(jnp.int32, (HEAD_DIM, HEAD_DIM), 1)
    eye = (ri == ci).astype(BF16)
    pad = 2 * SUBLANES
    for j in range(GDN_SAMPLE_ROWS):
        for h in range(H_A):
            lo, hi = h * HEAD_DIM, (h + 1) * HEAD_DIM
            q, k = qk[h][j:j + 1], qk[H_A + h][j:j + 1]
            v = y[j:j + 1, 2 * H_A * HEAD_DIM + lo:2 * H_A * HEAD_DIM + hi]
            eg = jnp.exp(gbr[j:j + 1, h:h + 1])
            beta = gbr[j:j + 1, H_A + h:H_A + h + 1]
            s0 = s0_ref[j, h]
            kp = _row0(k, pad).astype(BF16)
            v_new = beta * (v - eg * _dot(kp, s0.astype(BF16))[0:1])
            k_col = _dot_nt(eye, kp)[:, 0:1]
            s_new = s0 * eg + k_col * v_new
            o = _dot(_row0(q, pad).astype(BF16), s_new.astype(BF16))[0:1]
            o_ref[j, :, lo:hi] = _gdn_out_norm(o, gnw, zr[j:j + 1, lo:hi])
            sout_ref[j, h] = s_new


def _gdn_sample(xn, p0, p1, p2, gb, z, cw, gnw, s0):
    nb = xn.shape[0]
    full = lambda a: _const_spec(a.shape)
    st = pl.BlockSpec((GDN_SAMPLE_ROWS, H_A, HEAD_DIM, HEAD_DIM), lambda b: (b, 0, 0, 0))
    return pl.pallas_call(
        _gdn_sample_kernel,
        grid=(nb // GDN_SAMPLE_ROWS,),
        in_specs=[full(xn), full(p0), full(p1), full(p2), full(gb), full(z), full(cw), full(gnw), st],
        out_specs=[pl.BlockSpec((GDN_SAMPLE_ROWS, 1, Z_COLS), lambda b: (b, 0, 0)), st],
        out_shape=[jax.ShapeDtypeStruct((nb, 1, Z_COLS), F32),
                   jax.ShapeDtypeStruct(s0.shape, F32)],
        compiler_params=_params(("parallel",)),
        name="gdn_sample",
    )(xn, p0, p1, p2, gb, z, cw, gnw, s0)


def _moba_prompt_kernel(q_ref, k_ref, v_ref, o_ref, *, seq):
    nb = seq // MOBA_BLOCK
    k = k_ref[...]
    kb = k.astype(BF16)
    v_t = v_ref[...].T.astype(BF16)
    km = jnp.concatenate(
        [jnp.sum(k[j * MOBA_BLOCK:(j + 1) * MOBA_BLOCK], axis=0, keepdims=True) * (1.0 / MOBA_BLOCK)
         for j in range(nb)], axis=0)
    blk = lax.broadcasted_iota(jnp.int32, (nb, MOBA_BLOCK), 0)
    ki = lax.broadcasted_iota(jnp.int32, (MOBA_BLOCK, MOBA_BLOCK), 0)
    qi = lax.broadcasted_iota(jnp.int32, (MOBA_BLOCK, MOBA_BLOCK), 1)
    causal_bias = jnp.where(ki <= qi, 0.0, NEG_INF)
    scale = HEAD_DIM ** -0.5

    def logits(n):
        q = q_ref[n * MOBA_BLOCK:(n + 1) * MOBA_BLOCK, :]
        gate = _dot_nt(km, q, HIGHEST)
        qs = (q * scale).astype(BF16)
        cnt = jnp.zeros((nb, MOBA_BLOCK), jnp.int32)
        for i in range(n):
            gi = gate[i:i + 1, :]
            beats = (gi > gate) | ((gi == gate) & (i < blk))
            cnt = cnt + beats.astype(jnp.int32)
        sel_bias = jnp.where((cnt < MOBA_TOPK) & (blk < n), 0.0, NEG_INF)
        tiles = [_dot_nt(kb[j * MOBA_BLOCK:(j + 1) * MOBA_BLOCK], qs) + sel_bias[j:j + 1, :] for j in range(n)]
        tiles.append(_dot_nt(kb[n * MOBA_BLOCK:(n + 1) * MOBA_BLOCK], qs) + causal_bias)
        return tiles

    def attend(n, tiles):
        m = functools.reduce(jnp.maximum, [jnp.max(t, axis=0, keepdims=True) for t in tiles])
        ps = [jnp.exp(t - m) for t in tiles]
        l = functools.reduce(jnp.add, [jnp.sum(p, axis=0, keepdims=True) for p in ps])
        p = jnp.concatenate([x.astype(BF16) for x in ps], axis=0)
        o_t = _dot(v_t[:, :(n + 1) * MOBA_BLOCK], p) / l
        o_ref[n * MOBA_BLOCK:(n + 1) * MOBA_BLOCK, :] = o_t.T

    s_next = logits(0)
    for n in range(nb):
        s_cur = s_next
        if n + 1 < nb:
            s_next = logits(n + 1)
        attend(n, s_cur)


def _moba_prompt(qb, kb, vb, batch, seq):
    spec = pl.BlockSpec((seq, HEAD_DIM), lambda b, h: (b, h))
    return pl.pallas_call(
        functools.partial(_moba_prompt_kernel, seq=seq),
        grid=(batch, H_B),
        in_specs=[spec, spec, spec],
        out_specs=spec,
        out_shape=jax.ShapeDtypeStruct((batch * seq, H_B * HEAD_DIM), F32),
        compiler_params=_params(("parallel", "parallel")),
        name="moba_prompt",
    )(qb, kb, vb)


SELECT_ROWS = SUBLANES


def _moba_select_kernel(q_ref, ps_ref, idx_ref):
    r0 = pl.multiple_of(pl.program_id(0) * SELECT_ROWS, SELECT_ROWS)
    qs = q_ref[pl.ds(r0, SELECT_ROWS), :]
    n_blocks = ps_ref.shape[1]
    row = lax.broadcasted_iota(jnp.int32, (SUBLANES, LANES), 0)
    lane = lax.broadcasted_iota(jnp.int32, (SUBLANES, LANES), 1)
    blk = lax.broadcasted_iota(jnp.int32, (1, n_blocks), 1).astype(F32)
    for j in range(SELECT_ROWS):
        km = ps_ref[j] * (1.0 / MOBA_BLOCK)
        out = jnp.zeros((SUBLANES, LANES), F32)
        for h in range(H_B):
            lo, hi = h * HEAD_DIM, (h + 1) * HEAD_DIM
            g = _dot_nt(_row0(qs[j:j + 1, lo:hi]), km[:, lo:hi], HIGHEST)[0:1]
            for s in range(MOBA_TOPK):
                m = jnp.max(g, axis=1, keepdims=True)
                idx = jnp.min(jnp.where(g == m, blk, float(n_blocks)), axis=1, keepdims=True)
                out = jnp.where((row == h) & (lane == s), idx, out)
                g = jnp.where(blk == idx, NEG_INF, g)
        idx_ref[j] = out.astype(jnp.int32)


def _moba_select(qb, psums):
    nb, n_blocks, width = psums.shape
    return pl.pallas_call(
        _moba_select_kernel,
        grid=(nb // SELECT_ROWS,),
        in_specs=[_const_spec(qb.shape), pl.BlockSpec((SELECT_ROWS, n_blocks, width), lambda b: (b, 0, 0))],
        out_specs=pl.BlockSpec((SELECT_ROWS, SUBLANES, LANES), lambda b: (b, 0, 0)),
        out_shape=jax.ShapeDtypeStruct((nb, SUBLANES, LANES), jnp.int32),
        compiler_params=_params(("parallel",)),
        name="moba_select",
    )(qb, psums)


N_SEL_PAGES = MOBA_TOPK * PAGES_PER_BLOCK


def _moba_attend_kernel(idx_ref, pt_ref, q_ref, k_ref, v_ref, ck_ref, cv_ref, o_ref, kbuf, vbuf, sem):
    b = pl.program_id(0)
    n_rows = pl.num_programs(0)

    def copies(row, half):
        out = []
        for h in range(H_B):
            for s in range(MOBA_TOPK):
                blk = idx_ref[(row * H_B + h) * MOBA_TOPK + s]
                for p in range(PAGES_PER_BLOCK):
                    page = pt_ref[row, blk * PAGES_PER_BLOCK + p]
                    j = (h * MOBA_TOPK + s) * PAGES_PER_BLOCK + p
                    out.append(pltpu.make_async_copy(ck_ref.at[0, page, :, h, :], kbuf.at[half, j], sem.at[half]))
                    out.append(pltpu.make_async_copy(cv_ref.at[0, page, :, h, :], vbuf.at[half, j], sem.at[half]))
        return out

    @pl.when(b == 0)
    def _():
        for c in copies(0, 0):
            c.start()

    scale = HEAD_DIM ** -0.5
    for parity in range(2):
        @pl.when(b % 2 == parity)
        def _():
            for c in copies(jnp.minimum(b + 1, n_rows - 1), 1 - parity):
                c.start()
            for c in copies(0, parity):
                c.wait()
            q = q_ref[pl.ds(b, 1), :]
            k_new = k_ref[pl.ds(b, 1), :]
            v_new = v_ref[pl.ds(b, 1), :]
            for h in range(H_B):
                lo, hi = h * HEAD_DIM, (h + 1) * HEAD_DIM
                j0 = h * N_SEL_PAGES
                keys = kbuf[parity, j0:j0 + N_SEL_PAGES].reshape(N_SEL_PAGES * PAGE_SIZE, HEAD_DIM)
                vals = vbuf[parity, j0:j0 + N_SEL_PAGES].reshape(N_SEL_PAGES * PAGE_SIZE, HEAD_DIM)
                qh = q[:, lo:hi]
                s = _dot_nt(_row0(qh, 2 * SUBLANES).astype(BF16), keys.astype(BF16))[0:1] * scale
                s_new = jnp.sum(qh * k_new[:, lo:hi], axis=1, keepdims=True) * scale
                m = jnp.maximum(jnp.max(s, axis=1, keepdims=True), s_new)
                p = jnp.exp(s - m)
                p_new = jnp.exp(s_new - m)
                l = jnp.sum(p, axis=1, keepdims=True) + p_new
                pv = _dot(_row0(p, 2 * SUBLANES).astype(BF16), vals.astype(BF16))[0:1]
                o_ref[0, :, lo:hi] = (pv + p_new * v_new[:, lo:hi]) / l

    @pl.when(b == n_rows - 1)
    def _():
        for c in copies(0, n_rows % 2):
            c.wait()


def _moba_attend(idx_flat, page_table, cache_k, cache_v, qb, kb, vb):
    nb = qb.shape[0]
    n_slices = H_B * N_SEL_PAGES
    rows = pl.BlockSpec((nb, H_B * HEAD_DIM), lambda b, idx, pt: (0, 0))
    hbm = pl.BlockSpec(memory_space=pl.ANY)
    out = pl.pallas_call(
        _moba_attend_kernel,
        grid_spec=pltpu.PrefetchScalarGridSpec(
            num_scalar_prefetch=2, grid=(nb,),
            in_specs=[rows, rows, rows, hbm, hbm],
            out_specs=pl.BlockSpec((1, 1, H_B * HEAD_DIM), lambda b, idx, pt: (b, 0, 0)),
            scratch_shapes=[pltpu.VMEM((2, n_slices, PAGE_SIZE, HEAD_DIM), F32),
                            pltpu.VMEM((2, n_slices, PAGE_SIZE, HEAD_DIM), F32),
                            pltpu.SemaphoreType.DMA((2,))]),
        out_shape=jax.ShapeDtypeStruct((nb, 1, H_B * HEAD_DIM), F32),
        compiler_params=_params(("arbitrary",)),
        name="moba_attend",
    )(idx_flat, page_table, qb, kb, vb, cache_k, cache_v)
    return out.reshape(nb, H_B * HEAD_DIM)


def _out_ffn_kernel(x_ref, oa_ref, ob_ref, woa_ref, wob_ref, n2_ref, wg_ref, wu_ref, wd_ref, fn_ref, y_ref):
    x1 = (x_ref[...] + _dot(oa_ref[...].astype(BF16), woa_ref[...])
          + _dot(ob_ref[...].astype(BF16), wob_ref[...]))
    h = _rmsnorm(x1, n2_ref[...]).astype(BF16)
    act = _silu(_dot(h, wg_ref[...])) * _dot(h, wu_ref[...])
    x2 = x1 + _dot(act.astype(BF16), wd_ref[...])
    y_ref[...] = _rmsnorm(x2, fn_ref[...])


def _out_ffn(x, oa, ob, woa, wob, n2, wg, wu, wd, fn, tm):
    t = x.shape[0]
    row = lambda i: (i, 0)
    full = lambda a: _const_spec(a.shape)
    return pl.pallas_call(
        _out_ffn_kernel,
        grid=(t // tm,),
        in_specs=[pl.BlockSpec((tm, D_MODEL), row), pl.BlockSpec((tm, Z_COLS), row),
                  pl.BlockSpec((tm, H_B * HEAD_DIM), row), full(woa), full(wob), full(n2),
                  full(wg), full(wu), full(wd), full(fn)],
        out_specs=pl.BlockSpec((tm, D_MODEL), row),
        out_shape=jax.ShapeDtypeStruct((t, D_MODEL), F32),
        compiler_params=_params(("parallel",)),
        name="out_ffn",
    )(x, oa, ob, woa, wob, n2, wg, wu, wd, fn)


def _rotary_tables(pos):
    half = ROT_DIM // 2
    inv_freq = ROPE_THETA ** (-jnp.arange(half, dtype=F32) / half)
    ang = pos.astype(F32)[:, None] * inv_freq[None, :]
    cos, sin = jnp.cos(ang), jnp.sin(ang)
    n = pos.shape[0]
    rest = HEAD_DIM - ROT_DIM
    cos_t = jnp.concatenate([cos, cos, jnp.ones((n, rest), F32)], axis=1)
    zeros_h = jnp.zeros((n, half), F32)
    sin_a = jnp.concatenate([-sin, zeros_h, jnp.zeros((n, rest), F32)], axis=1)
    sin_b = jnp.concatenate([zeros_h, sin, jnp.zeros((n, rest), F32)], axis=1)
    return cos_t, sin_a, sin_b


def kernel(x_prompt, x_sample, cache_k, cache_v, page_table, state_gdn, state_conv, norm1_w, w_in, conv_w,
           a_log, dt_bias, gdn_norm_w, w_out, norm2_w, w_gate, w_up, w_down, final_norm_w):
    depth = w_in.shape[0]
    assert depth == 1, "single-layer trunk"
    batch, seq, _ = x_prompt.shape
    dec_batch, dec_seq, _ = x_sample.shape
    assert dec_seq == 1
    past_len = page_table.shape[1] * PAGE_SIZE

    wi = w_in[0]
    ab0 = GDN_QKV + Z_COLS
    w_all = jnp.concatenate(
        [wi[:, :ab0], wi[:, ab0 + 2 * H_A:], wi[:, ab0:ab0 + 2 * H_A],
         jnp.zeros((D_MODEL, AB_PAD - 2 * H_A), wi.dtype)], axis=1).astype(BF16)
    gpar = jnp.zeros((SUBLANES, LANES), F32)
    gpar = gpar.at[0, :H_A].set(a_log[0].astype(F32)).at[1, :H_A].set(dt_bias[0].astype(F32))
    nw1 = norm1_w[0].reshape(1, D_MODEL)
    cw = conv_w[0]
    gnw = gdn_norm_w[0].reshape(1, HEAD_DIM)
    woa = w_out[0][:Z_COLS].astype(BF16)
    wob = w_out[0][Z_COLS:].astype(BF16)
    n2 = norm2_w[0].reshape(1, D_MODEL)
    wg, wu, wd = w_gate[0].astype(BF16), w_up[0].astype(BF16), w_down[0].astype(BF16)
    fn = final_norm_w.reshape(1, D_MODEL)

    xp = x_prompt.reshape(batch * seq, D_MODEL)
    tabs_p = _rotary_tables(jnp.arange(seq, dtype=jnp.int32))
    qkva_p, z_p, gb_p, qb_p, kb_p, vb_p, k5_p, v5_p = _in_proj(xp, nw1, w_all, gpar, *tabs_p,
                                                               batch=batch, seq=seq, tm=512)
    oa_p, gdn_p, psums = _gdn_prompt(qkva_p, gb_p, z_p, cw, gnw, cache_k, page_table, batch, seq, tl=256)
    ob_p = _moba_prompt(qb_p, kb_p, vb_p, batch, seq)
    y_p = _out_ffn(xp, oa_p, ob_p, woa, wob, n2, wg, wu, wd, fn, tm=512)

    xs = x_sample.reshape(dec_batch, D_MODEL)
    tabs_s = _rotary_tables(jnp.full((dec_batch,), past_len, jnp.int32))
    qkva_s, z_s, gb_s, qb_s, kb_s, vb_s, k5_s, v5_s = _in_proj(xs, nw1, w_all, gpar, *tabs_s,
                                                               batch=1, seq=dec_batch, tm=dec_batch)
    sc = state_conv[0]
    oa_s, gdn_s = _gdn_sample(qkva_s, sc[:, 0], sc[:, 1], sc[:, 2], gb_s, z_s, cw, gnw, state_gdn[0])
    idx = _moba_select(qb_s, psums)
    idx_flat = idx[:, :H_B, :MOBA_TOPK].reshape(-1)
    ob_s = _moba_attend(idx_flat, page_table, cache_k, cache_v, qb_s, kb_s, vb_s)
    y_s = _out_ffn(xs, oa_s.reshape(dec_batch, Z_COLS), ob_s, woa, wob, n2, wg, wu, wd, fn, tm=dec_batch)

    conv_p = qkva_p.reshape(batch, seq, GDN_QKV)[:, seq - (CONV_W - 1):]
    conv_s = jnp.concatenate([sc[:, 1:], qkva_s[:, None, :]], axis=1)
    return (y_p.reshape(batch, seq, D_MODEL), y_s.reshape(dec_batch, 1, D_MODEL),
            k5_p, v5_p,
            k5_s.reshape(1, dec_batch, 1, H_B, HEAD_DIM), v5_s.reshape(1, dec_batch, 1, H_B, HEAD_DIM),
            gdn_p[None], gdn_s[None], conv_p[None], conv_s[None])
```

```python
import functools

import jax
import jax.numpy as jnp
from jax import lax
from jax.experimental import pallas as pl
from jax.experimental.pallas import tpu as pltpu

D_MODEL = 1024
HEAD_DIM = 128
H_A = 4
H_B = 4
GDN_QKV = 3 * H_A * HEAD_DIM
MOBA_QKV = 3 * H_B * HEAD_DIM
Z_COLS = H_A * HEAD_DIM
CONV_W = 4
CHUNK = 64
SOLVE_ROWS = 128
MOBA_BLOCK = 256
MOBA_TOPK = 3
PAGE_SIZE = 128
PAGES_PER_BLOCK = MOBA_BLOCK // PAGE_SIZE
ROT_DIM = HEAD_DIM // 4
ROPE_THETA = 500000.0
EPS = 1e-6

LANES = 128
SUBLANES = 8
AB_PAD = LANES
PROJ_COLS = GDN_QKV + Z_COLS + MOBA_QKV + AB_PAD
VMEM_LIMIT = 56 * 1024 * 1024

F32 = jnp.float32
BF16 = jnp.bfloat16
HIGHEST = lax.Precision.HIGHEST
NEG_INF = float("-inf")
LOG2_E = 1.4426950408889634


def _dot(a, b, precision=None):
    return jnp.dot(a, b, precision=precision, preferred_element_type=F32)


def _dot_nt(a, b, precision=None):
    return lax.dot_general(a, b, (((1,), (1,)), ((), ())), precision=precision,
                           preferred_element_type=F32)


def _rmsnorm(x, w):
    return x * lax.rsqrt(jnp.mean(x * x, axis=-1, keepdims=True) + EPS) * w


def _l2norm(x):
    return x * lax.rsqrt(jnp.sum(x * x, axis=-1, keepdims=True) + EPS)


def _silu(x):
    return x * jax.nn.sigmoid(x)


def _const_spec(shape):
    return pl.BlockSpec(shape, lambda *_: (0,) * len(shape), pipeline_mode=pl.Buffered(1))


def _params(semantics):
    return pltpu.CompilerParams(dimension_semantics=semantics, vmem_limit_bytes=VMEM_LIMIT)


def _in_proj_kernel(x_ref, nw_ref, w_ref, gpar_ref, cos_ref, sa_ref, sb_ref,
                    qkva_ref, z_ref, gb_ref, qb_ref, kb_ref, vb_ref, k5_ref, v5_ref):
    xn = _rmsnorm(x_ref[...], nw_ref[...]).astype(BF16)
    b0 = GDN_QKV + Z_COLS
    pb = _dot(xn, w_ref[:, b0:b0 + MOBA_QKV])
    pa = _dot(xn, w_ref[:, :GDN_QKV])
    cos, sa, sb = cos_ref[...], sa_ref[...], sb_ref[...]
    half = ROT_DIM // 2

    def rot(t):
        return (t * cos + pltpu.roll(t, HEAD_DIM - half, 1) * sa + pltpu.roll(t, half, 1) * sb)

    for h in range(H_B):
        lo, hi = h * HEAD_DIM, (h + 1) * HEAD_DIM
        qb_ref[:, lo:hi] = rot(pb[:, lo:hi])
        kh = rot(pb[:, H_B * HEAD_DIM + lo:H_B * HEAD_DIM + hi])
        vh = pb[:, 2 * H_B * HEAD_DIM + lo:2 * H_B * HEAD_DIM + hi]
        kb_ref[:, lo:hi] = kh
        vb_ref[:, lo:hi] = vh
        k5_ref[0, 0, :, h, :] = kh
        v5_ref[0, 0, :, h, :] = vh
    pz = _dot(xn, w_ref[:, GDN_QKV:b0])
    ab = _dot(xn, w_ref[:, b0 + MOBA_QKV:])
    qkva_ref[...] = pa
    z_ref[...] = pz
    lane = lax.broadcasted_iota(jnp.int32, ab.shape, 1)
    g = -jnp.exp(gpar_ref[0:1, :]) * jax.nn.softplus(ab + gpar_ref[1:2, :])
    gb_ref[...] = jnp.where(lane < H_A, g, jax.nn.sigmoid(ab))


def _in_proj(x, nw, w_all, gpar, cos, sa, sb, batch, seq, tm):
    t = batch * seq
    n_t = seq // tm
    row = lambda i: (i, 0)
    tab = lambda i: (i % n_t, 0)
    kv = lambda i: (0, i // n_t, i % n_t, 0, 0)
    widths = (GDN_QKV, Z_COLS, AB_PAD, H_B * HEAD_DIM, H_B * HEAD_DIM, H_B * HEAD_DIM)
    kv_shape = jax.ShapeDtypeStruct((1, batch, seq, H_B, HEAD_DIM), F32)
    kv_spec = pl.BlockSpec((1, 1, tm, H_B, HEAD_DIM), kv)
    return pl.pallas_call(
        _in_proj_kernel,
        grid=(t // tm,),
        in_specs=[pl.BlockSpec((tm, D_MODEL), row), _const_spec((1, D_MODEL)),
                  _const_spec((D_MODEL, PROJ_COLS)), _const_spec((SUBLANES, LANES)),
                  pl.BlockSpec((tm, LANES), tab), pl.BlockSpec((tm, LANES), tab),
                  pl.BlockSpec((tm, LANES), tab)],
        out_specs=[pl.BlockSpec((tm, w), row) for w in widths] + [kv_spec, kv_spec],
        out_shape=[jax.ShapeDtypeStruct((t, w), F32) for w in widths] + [kv_shape, kv_shape],
        compiler_params=_params(("parallel",)),
        name="in_proj",
    )(x, nw, w_all, gpar, cos, sa, sb)


def _gdn_head_norms(y):
    parts = []
    for h in range(H_A):
        parts.append(_l2norm(y[:, h * HEAD_DIM:(h + 1) * HEAD_DIM]) * (HEAD_DIM ** -0.5))
    for h in range(H_A, 2 * H_A):
        parts.append(_l2norm(y[:, h * HEAD_DIM:(h + 1) * HEAD_DIM]))
    return parts


def _gdn_out_norm(o, gnw, z):
    return _rmsnorm(o, gnw) * _silu(z)


def _sum_rows(x):
    while x.shape[0] > 1:
        half = x.shape[0] // 2
        x = x[:half] + x[half:]
    return x[0]


def _block_key_sum(pages_ref, r, out_ref):
    i = r * PAGES_PER_BLOCK
    s = _sum_rows(pages_ref[i])
    for j in range(1, PAGES_PER_BLOCK):
        s = s + _sum_rows(pages_ref[i + j])
    s = s[:H_B] + s[H_B:]
    for h in range(H_B):
        out_ref[0, r:r + 1, h * HEAD_DIM:(h + 1) * HEAD_DIM] = s[h:h + 1, :]


def _gdn_prompt_kernel(pt_ref, qkva_ref, gb_ref, z_ref, cw_ref, gnw_ref, cache_ref, *refs,
                       tl, n_t, n_tiles, n_pages, steps_per_row):
    step = pl.program_id(0)
    o_ref, sfin_ref, psum_ref, xbuf, u_s, wq_s, kdt_s, qk_s, egl_s, s_s, pbuf, psem = refs

    def page_copy(tile, i, half):
        page = pt_ref[tile // steps_per_row, (tile % steps_per_row) * n_pages + i]
        return pltpu.make_async_copy(cache_ref.at[0, page], pbuf.at[half, i], psem.at[half])

    @pl.when(step == 0)
    def _():
        xbuf[0:SUBLANES, :] = jnp.zeros((SUBLANES, GDN_QKV), F32)
        u_s[...] = jnp.zeros_like(u_s)
        wq_s[...] = jnp.zeros_like(wq_s)
        kdt_s[...] = jnp.zeros_like(kdt_s)
        qk_s[...] = jnp.zeros_like(qk_s)
        egl_s[...] = jnp.zeros_like(egl_s)
        s_s[...] = jnp.zeros_like(s_s)
        for i in range(n_pages):
            page_copy(0, i, 0).start()

    for parity in range(2):
        @pl.when(step % 2 == parity)
        def _():
            nxt = jnp.minimum(step + 1, n_tiles - 1)
            for i in range(n_pages):
                page_copy(nxt, i, 1 - parity).start()

            def wait_pages():
                for i in range(n_pages):
                    page_copy(0, i, parity).wait()

            _gdn_prompt_step(qkva_ref, gb_ref, z_ref, cw_ref, gnw_ref, o_ref, sfin_ref, psum_ref,
                             xbuf, u_s, wq_s, kdt_s, qk_s, egl_s, s_s, pbuf.at[parity], wait_pages,
                             tl=tl, n_t=n_t, n_pages=n_pages, wr=parity)

    @pl.when(step == n_tiles)
    def _():
        for i in range(n_pages):
            page_copy(0, i, (n_tiles + 1) % 2).wait()


def _gdn_prompt_step(qkva_ref, gb_ref, z_ref, cw_ref, gnw_ref, o_ref, sfin_ref, psum_ref,
                     xbuf, u_s, wq_s, kdt_s, qk_s, egl_s, s_s, pages, wait_pages, *, tl, n_t, n_pages, wr):
    wait_pages()
    step = pl.program_id(0)
    rd = 1 - wr
    a_starts_seq = (step % n_t) == 0
    b_starts_seq = ((step + n_t - 1) % n_t) == 0
    n_c = tl // CHUNK
    pending_blocks = list(range(n_pages // PAGES_PER_BLOCK))

    def reduce_cache_blocks(count):
        for _ in range(min(count, len(pending_blocks))):
            _block_key_sum(pages, pending_blocks.pop(0), psum_ref)

    tail = xbuf[0:SUBLANES, :]
    xbuf[0:SUBLANES, :] = jnp.where(a_starts_seq, 0.0, tail)
    xbuf[SUBLANES:SUBLANES + tl, :] = qkva_ref[...]
    cw = cw_ref[...]
    xs = xbuf[...]
    y = pltpu.roll(xs, CONV_W - 1, 0)[SUBLANES:] * cw[0:1]
    for j in range(1, CONV_W - 1):
        y = y + pltpu.roll(xs, CONV_W - 1 - j, 0)[SUBLANES:] * cw[j:j + 1]
    y = _silu(y + xs[SUBLANES:] * cw[CONV_W - 1:CONV_W])
    xbuf[0:SUBLANES, :] = xbuf[tl:tl + SUBLANES, :]
    qk_parts = _gdn_head_norms(y)

    ri = lax.broadcasted_iota(jnp.int32, (tl, tl), 0)
    ci = lax.broadcasted_iota(jnp.int32, (tl, tl), 1)
    same = (ri // CHUNK) == (ci // CHUNK)
    gb = gb_ref[...]
    sums = _dot(jnp.concatenate([(same & (ri >= ci)).astype(F32), same.astype(F32)], axis=0), gb, HIGHEST)
    gc_all, gl_all = sums[:tl], sums[tl:]
    gc_t = gc_all.T

    rg = lax.broadcasted_iota(jnp.int32, (SOLVE_ROWS, SOLVE_ROWS), 0)
    cg = lax.broadcasted_iota(jnp.int32, (SOLVE_ROWS, SOLVE_ROWS), 1)
    same_g = (rg // CHUNK) == (cg // CHUNK)
    tri = same_g & (rg >= cg)
    strict = same_g & (rg > cg)
    eye = (rg == cg).astype(F32)
    chunks_per_group = SOLVE_ROWS // CHUNK
    groups = [(h, grp) for h in range(H_A) for grp in range(tl // SOLVE_ROWS)]
    ps, tinvs, rhs = [], [], []
    for h, grp in groups:
        lo, hi = h * HEAD_DIM, (h + 1) * HEAD_DIM
        g0, g1 = grp * SOLVE_ROWS, (grp + 1) * SOLVE_ROWS
        q, k = qk_parts[h][g0:g1], qk_parts[H_A + h][g0:g1]
        v = y[g0:g1, 2 * H_A * HEAD_DIM + lo:2 * H_A * HEAD_DIM + hi]
        beta = gb[g0:g1, H_A + h:H_A + h + 1]
        gcol = gc_all[g0:g1, h:h + 1]
        glcol = gl_all[g0:g1, h:h + 1]
        decay = jnp.where(tri, jnp.exp(gcol - gc_t[h:h + 1, g0:g1]), 0.0)
        egc = jnp.exp(gcol)
        kbeta = k * beta
        qkk = _dot_nt(jnp.concatenate([q, kbeta], axis=0).astype(BF16), k.astype(BF16))
        qk = qkk[:SOLVE_ROWS] * decay
        p = -jnp.where(strict, qkk[SOLVE_ROWS:] * decay, 0.0)
        ps.append(p)
        tinvs.append(eye + p)
        rhs.append(jnp.concatenate([v * beta, kbeta * egc], axis=1).astype(BF16))
        qd = (q * egc).astype(BF16)
        kd = k * jnp.exp(glcol - gcol)
        egl = jnp.exp(glcol)
        for cc in range(chunks_per_group):
            c = grp * chunks_per_group + cc
            r0, r1 = cc * CHUNK, (cc + 1) * CHUNK
            wq_s[wr, h, c, CHUNK:2 * CHUNK, :] = qd[r0:r1]
            kdt_s[wr, h, c] = kd[r0:r1].T.astype(BF16)
            qk_s[wr, h, c] = qk[r0:r1, r0:r1].astype(BF16)
            egl_s[wr, h, c] = jnp.broadcast_to(egl[r0:r0 + 1, :], (SUBLANES, HEAD_DIM))

    gnw = gnw_ref[...]

    state, held = {}, {}

    def recurrence_first(c, h):
        s = state.get(h)
        if s is None:
            s = jnp.where(b_starts_seq, 0.0, s_s[h])
        held[h] = (s, _dot(wq_s[rd, h, c], s.astype(BF16)))

    def recurrence_second(c, h):
        r0, r1 = c * CHUNK, (c + 1) * CHUNK
        lo, hi = h * HEAD_DIM, (h + 1) * HEAD_DIM
        s, ws = held.pop(h)
        v_new = u_s[rd, r0:r1, lo:hi] - ws[:CHUNK]
        vnb = v_new.astype(BF16)
        o = ws[CHUNK:] + _dot(qk_s[rd, h, c], vnb)
        state[h] = s * egl_s[rd, h, c][0:1] + _dot(kdt_s[rd, h, c], vnb)
        o_ref[r0:r1, lo:hi] = _gdn_out_norm(o, gnw, z_ref[r0:r1, lo:hi])
        if c == n_c - 1:
            s_s[h] = state[h]

    side_work = []
    for c in range(n_c):
        for piece in (recurrence_first, recurrence_second):
            for h in range(H_A):
                side_work.append(functools.partial(piece, c, h))
                side_work.append(functools.partial(reduce_cache_blocks, 1))
    n_iter = 5
    n_slots = 2 * n_iter * len(groups)
    done = [0]

    def side_slot(k):
        target = (k + 1) * len(side_work) // n_slots
        while done[0] < target:
            side_work[done[0]]()
            done[0] += 1

    slot = 0
    for it in range(n_iter):
        for i in range(len(groups)):
            pb = ps[i].astype(BF16)
            ps[i] = _dot(pb, pb)
            side_slot(slot)
            slot += 1
        for i in range(len(groups)):
            tinvs[i] = tinvs[i] + _dot(tinvs[i].astype(BF16), ps[i].astype(BF16))
            side_slot(slot)
            slot += 1
    for i, (h, grp) in enumerate(groups):
        lo, hi = h * HEAD_DIM, (h + 1) * HEAD_DIM
        g0, g1 = grp * SOLVE_ROWS, (grp + 1) * SOLVE_ROWS
        uw = _dot(tinvs[i].astype(BF16), rhs[i])
        u_s[wr, g0:g1, lo:hi] = uw[:, :HEAD_DIM]
        w = uw[:, HEAD_DIM:].astype(BF16)
        for cc in range(chunks_per_group):
            c = grp * chunks_per_group + cc
            wq_s[wr, h, c, 0:CHUNK, :] = w[cc * CHUNK:(cc + 1) * CHUNK]
    reduce_cache_blocks(len(pending_blocks))
    sfin_ref[0] = s_s[...]


def _gdn_prompt(qkva, gb, z, cw, gnw, cache_k, page_table, batch, seq, tl):
    n_t = seq // tl
    n_c = tl // CHUNK
    n_tiles = batch * n_t
    dec_batch, pages_per_row = page_table.shape
    n_pages = dec_batch * pages_per_row // n_tiles
    steps_per_row = pages_per_row // n_pages
    assert n_pages * n_tiles == dec_batch * pages_per_row and steps_per_row * n_pages == pages_per_row
    blocks_per_step = n_pages // PAGES_PER_BLOCK
    width = H_B * HEAD_DIM
    cache_pairs = cache_k.reshape(1, cache_k.shape[1], PAGE_SIZE // 2, 2 * H_B, HEAD_DIM)
    tile_a = lambda s: jnp.minimum(s, n_tiles - 1)
    tile_b = lambda s: jnp.maximum(s - 1, 0)
    row_a = lambda s, pt: (tile_a(s), 0)
    row_b = lambda s, pt: (tile_b(s), 0)
    const = lambda shape: pl.BlockSpec(shape, lambda s, pt: (0,) * len(shape), pipeline_mode=pl.Buffered(1))

    def psum_index(s, pt):
        a = tile_a(s)
        return (a // steps_per_row, a % steps_per_row, 0)

    return pl.pallas_call(
        functools.partial(_gdn_prompt_kernel, tl=tl, n_t=n_t, n_tiles=n_tiles, n_pages=n_pages,
                          steps_per_row=steps_per_row),
        grid_spec=pltpu.PrefetchScalarGridSpec(
            num_scalar_prefetch=1, grid=(n_tiles + 1,),
            in_specs=[pl.BlockSpec((tl, GDN_QKV), row_a), pl.BlockSpec((tl, AB_PAD), row_a),
                      pl.BlockSpec((tl, Z_COLS), row_b), const((CONV_W, GDN_QKV)), const((1, HEAD_DIM)),
                      pl.BlockSpec(memory_space=pl.ANY)],
            out_specs=[pl.BlockSpec((tl, Z_COLS), row_b),
                       pl.BlockSpec((1, H_A, HEAD_DIM, HEAD_DIM), lambda s, pt: (tile_b(s) // n_t, 0, 0, 0)),
                       pl.BlockSpec((1, blocks_per_step, width), psum_index)],
            scratch_shapes=[pltpu.VMEM((tl + SUBLANES, GDN_QKV), F32),
                            pltpu.VMEM((2, tl, Z_COLS), F32),
                            pltpu.VMEM((2, H_A, n_c, 2 * CHUNK, HEAD_DIM), BF16),
                            pltpu.VMEM((2, H_A, n_c, HEAD_DIM, CHUNK), BF16),
                            pltpu.VMEM((2, H_A, n_c, CHUNK, CHUNK), BF16),
                            pltpu.VMEM((2, H_A, n_c, SUBLANES, HEAD_DIM), F32),
                            pltpu.VMEM((H_A, HEAD_DIM, HEAD_DIM), F32),
                            pltpu.VMEM((2, n_pages, PAGE_SIZE // 2, 2 * H_B, HEAD_DIM), F32),
                            pltpu.SemaphoreType.DMA((2,))]),
        out_shape=[jax.ShapeDtypeStruct((batch * seq, Z_COLS), F32),
                   jax.ShapeDtypeStruct((batch, H_A, HEAD_DIM, HEAD_DIM), F32),
                   jax.ShapeDtypeStruct((dec_batch, pages_per_row // PAGES_PER_BLOCK, width), F32)],
        compiler_params=_params(("arbitrary",)),
        name="gdn_prompt",
    )(page_table, qkva, gb, z, cw, gnw, cache_pairs)


def _row0(x, rows=SUBLANES):
    r = lax.broadcasted_iota(jnp.int32, (rows, x.shape[1]), 0)
    return jnp.where(r == 0, jnp.broadcast_to(x, (rows, x.shape[1])), 0.0)


GDN_SAMPLE_ROWS = SUBLANES


def _gdn_sample_kernel(xn_ref, p0_ref, p1_ref, p2_ref, gb_ref, z_ref, cw_ref, gnw_ref, s0_ref,
                       o_ref, sout_ref):
    r0 = pl.multiple_of(pl.program_id(0) * GDN_SAMPLE_ROWS, GDN_SAMPLE_ROWS)
    rows = lambda r: r[pl.ds(r0, GDN_SAMPLE_ROWS), :]
    cw = cw_ref[...]
    y = rows(p0_ref) * cw[0:1]
    y = y + rows(p1_ref) * cw[1:2]
    y = y + rows(p2_ref) * cw[2:3]
    y = y + rows(xn_ref) * cw[3:4]
    y = _silu(y)
    qk = _gdn_head_norms(y)
    gbr = rows(gb_ref)
    zr = rows(z_ref)
    gnw = gnw_ref[...]
    ri = lax.broadcasted_iota(jnp.int32, (HEAD_DIM, HEAD_DIM), 0)
    ci = lax.broadcasted_iota(jnp.int32, (HEAD_DIM, HEAD_DIM), 1)
    eye = (ri == ci).astype(BF16)
    pad = 2 * SUBLANES
    for j in range(GDN_SAMPLE_ROWS):
        for h in range(H_A):
            lo, hi = h * HEAD_DIM, (h + 1) * HEAD_DIM
            q, k = qk[h][j:j + 1], qk[H_A + h][j:j + 1]
            v = y[j:j + 1, 2 * H_A * HEAD_DIM + lo:2 * H_A * HEAD_DIM + hi]
            eg = jnp.exp(gbr[j:j + 1, h:h + 1])
            beta = gbr[j:j + 1, H_A + h:H_A + h + 1]
            s0 = s0_ref[j, h]
            kp = _row0(k, pad).astype(BF16)
            v_new = beta * (v - eg * _dot(kp, s0.astype(BF16))[0:1])
            k_col = _dot_nt(eye, kp)[:, 0:1]
            s_new = s0 * eg + k_col * v_new
            o = _dot(_row0(q, pad).astype(BF16), s_new.astype(BF16))[0:1]
            o_ref[j, :, lo:hi] = _gdn_out_norm(o, gnw, zr[j:j + 1, lo:hi])
            sout_ref[j, h] = s_new


def _gdn_sample(xn, p0, p1, p2, gb, z, cw, gnw, s0):
    nb = xn.shape[0]
    full = lambda a: _const_spec(a.shape)
    st = pl.BlockSpec((GDN_SAMPLE_ROWS, H_A, HEAD_DIM, HEAD_DIM), lambda b: (b, 0, 0, 0))
    return pl.pallas_call(
        _gdn_sample_kernel,
        grid=(nb // GDN_SAMPLE_ROWS,),
        in_specs=[full(xn), full(p0), full(p1), full(p2), full(gb), full(z), full(cw), full(gnw), st],
        out_specs=[pl.BlockSpec((GDN_SAMPLE_ROWS, 1, Z_COLS), lambda b: (b, 0, 0)), st],
        out_shape=[jax.ShapeDtypeStruct((nb, 1, Z_COLS), F32),
                   jax.ShapeDtypeStruct(s0.shape, F32)],
        compiler_params=_params(("parallel",)),
        name="gdn_sample",
    )(xn, p0, p1, p2, gb, z, cw, gnw, s0)


def _moba_prompt_kernel(q_ref, k_ref, v_ref, o_ref, *, seq):
    nb = seq // MOBA_BLOCK
    k = k_ref[...]
    kb = k.astype(BF16)
    v_t = jnp.concatenate([v_ref[...].T, jnp.ones((SUBLANES, seq), F32)], axis=0).astype(BF16)
    km = jnp.concatenate(
        [jnp.sum(k[j * MOBA_BLOCK:(j + 1) * MOBA_BLOCK], axis=0, keepdims=True) * (1.0 / MOBA_BLOCK)
         for j in range(nb)], axis=0)
    blk = lax.broadcasted_iota(jnp.int32, (nb, MOBA_BLOCK), 0)
    ki = lax.broadcasted_iota(jnp.int32, (MOBA_BLOCK, MOBA_BLOCK), 0)
    qi = lax.broadcasted_iota(jnp.int32, (MOBA_BLOCK, MOBA_BLOCK), 1)
    causal_bias = jnp.where(ki <= qi, 0.0, NEG_INF)
    scale = HEAD_DIM ** -0.5 * LOG2_E

    def logits(n):
        q = q_ref[n * MOBA_BLOCK:(n + 1) * MOBA_BLOCK, :]
        gate = _dot_nt(km, q, HIGHEST)
        qs = (q * scale).astype(BF16)
        cnt = jnp.zeros((nb, MOBA_BLOCK), jnp.int32)
        for i in range(n):
            gi = gate[i:i + 1, :]
            beats = (gi > gate) | ((gi == gate) & (i < blk))
            cnt = cnt + beats.astype(jnp.int32)
        sel_bias = jnp.where((cnt < MOBA_TOPK) & (blk < n), 0.0, NEG_INF)
        tiles = [_dot_nt(kb[j * MOBA_BLOCK:(j + 1) * MOBA_BLOCK], qs) + sel_bias[j:j + 1, :] for j in range(n)]
        tiles.append(_dot_nt(kb[n * MOBA_BLOCK:(n + 1) * MOBA_BLOCK], qs) + causal_bias)
        return tiles

    def probabilities(tiles):
        m = functools.reduce(jnp.maximum, [jnp.max(t, axis=0, keepdims=True) for t in tiles])
        return jnp.concatenate([jnp.exp2(t - m).astype(BF16) for t in tiles], axis=0)

    def weighted_values(n, p):
        pv = _dot(v_t[:, :(n + 1) * MOBA_BLOCK], p)
        o_t = pv[:HEAD_DIM] / pv[HEAD_DIM:HEAD_DIM + 1]
        o_ref[n * MOBA_BLOCK:(n + 1) * MOBA_BLOCK, :] = o_t.T

    ahead = 2
    pending = {n: logits(n) for n in range(min(ahead, nb))}
    for n in range(nb):
        p = probabilities(pending.pop(n))
        if n + ahead < nb:
            pending[n + ahead] = logits(n + ahead)
        weighted_values(n, p)


def _moba_prompt(qb, kb, vb, batch, seq):
    spec = pl.BlockSpec((seq, HEAD_DIM), lambda b, h: (b, h))
    return pl.pallas_call(
        functools.partial(_moba_prompt_kernel, seq=seq),
        grid=(batch, H_B),
        in_specs=[spec, spec, spec],
        out_specs=spec,
        out_shape=jax.ShapeDtypeStruct((batch * seq, H_B * HEAD_DIM), F32),
        compiler_params=_params(("parallel", "parallel")),
        name="moba_prompt",
    )(qb, kb, vb)


SELECT_ROWS = SUBLANES


def _moba_select_kernel(q_ref, ps_ref, idx_ref):
    r0 = pl.multiple_of(pl.program_id(0) * SELECT_ROWS, SELECT_ROWS)
    qs = q_ref[pl.ds(r0, SELECT_ROWS), :]
    n_blocks = ps_ref.shape[1]
    row = lax.broadcasted_iota(jnp.int32, (SUBLANES, LANES), 0)
    lane = lax.broadcasted_iota(jnp.int32, (SUBLANES, LANES), 1)
    blk = lax.broadcasted_iota(jnp.int32, (1, n_blocks), 1).astype(F32)
    for j in range(SELECT_ROWS):
        km = ps_ref[j] * (1.0 / MOBA_BLOCK)
        out = jnp.zeros((SUBLANES, LANES), F32)
        for h in range(H_B):
            lo, hi = h * HEAD_DIM, (h + 1) * HEAD_DIM
            g = _dot_nt(_row0(qs[j:j + 1, lo:hi]), km[:, lo:hi], HIGHEST)[0:1]
            for s in range(MOBA_TOPK):
                m = jnp.max(g, axis=1, keepdims=True)
                idx = jnp.min(jnp.where(g == m, blk, float(n_blocks)), axis=1, keepdims=True)
                out = jnp.where((row == h) & (lane == s), idx, out)
                g = jnp.where(blk == idx, NEG_INF, g)
        idx_ref[j] = out.astype(jnp.int32)


def _moba_select(qb, psums):
    nb, n_blocks, width = psums.shape
    return pl.pallas_call(
        _moba_select_kernel,
        grid=(nb // SELECT_ROWS,),
        in_specs=[_const_spec(qb.shape), pl.BlockSpec((SELECT_ROWS, n_blocks, width), lambda b: (b, 0, 0))],
        out_specs=pl.BlockSpec((SELECT_ROWS, SUBLANES, LANES), lambda b: (b, 0, 0)),
        out_shape=jax.ShapeDtypeStruct((nb, SUBLANES, LANES), jnp.int32),
        compiler_params=_params(("parallel",)),
        name="moba_select",
    )(qb, psums)


N_SEL_PAGES = MOBA_TOPK * PAGES_PER_BLOCK


def _moba_attend_kernel(idx_ref, pt_ref, q_ref, k_ref, v_ref, ck_ref, cv_ref, o_ref, kbuf, vbuf, sem):
    b = pl.program_id(0)
    n_rows = pl.num_programs(0)

    def copies(row, half):
        out = []
        for h in range(H_B):
            for s in range(MOBA_TOPK):
                blk = idx_ref[(row * H_B + h) * MOBA_TOPK + s]
                for p in range(PAGES_PER_BLOCK):
                    page = pt_ref[row, blk * PAGES_PER_BLOCK + p]
                    j = (h * MOBA_TOPK + s) * PAGES_PER_BLOCK + p
                    out.append(pltpu.make_async_copy(ck_ref.at[0, page, :, h, :], kbuf.at[half, j], sem.at[half]))
                    out.append(pltpu.make_async_copy(cv_ref.at[0, page, :, h, :], vbuf.at[half, j], sem.at[half]))
        return out

    @pl.when(b == 0)
    def _():
        for c in copies(0, 0):
            c.start()

    scale = HEAD_DIM ** -0.5
    for parity in range(2):
        @pl.when(b % 2 == parity)
        def _():
            for c in copies(jnp.minimum(b + 1, n_rows - 1), 1 - parity):
                c.start()
            for c in copies(0, parity):
                c.wait()
            q = q_ref[pl.ds(b, 1), :]
            k_new = k_ref[pl.ds(b, 1), :]
            v_new = v_ref[pl.ds(b, 1), :]
            for h in range(H_B):
                lo, hi = h * HEAD_DIM, (h + 1) * HEAD_DIM
                j0 = h * N_SEL_PAGES
                keys = kbuf[parity, j0:j0 + N_SEL_PAGES].reshape(N_SEL_PAGES * PAGE_SIZE, HEAD_DIM)
                vals = vbuf[parity, j0:j0 + N_SEL_PAGES].reshape(N_SEL_PAGES * PAGE_SIZE, HEAD_DIM)
                qh = q[:, lo:hi]
                s = _dot_nt(_row0(qh, 2 * SUBLANES).astype(BF16), keys.astype(BF16))[0:1] * scale
                s_new = jnp.sum(qh * k_new[:, lo:hi], axis=1, keepdims=True) * scale
                m = jnp.maximum(jnp.max(s, axis=1, keepdims=True), s_new)
                p = jnp.exp(s - m)
                p_new = jnp.exp(s_new - m)
                l = jnp.sum(p, axis=1, keepdims=True) + p_new
                pv = _dot(_row0(p, 2 * SUBLANES).astype(BF16), vals.astype(BF16))[0:1]
                o_ref[0, :, lo:hi] = (pv + p_new * v_new[:, lo:hi]) / l

    @pl.when(b == n_rows - 1)
    def _():
        for c in copies(0, n_rows % 2):
            c.wait()


def _moba_attend(idx_flat, page_table, cache_k, cache_v, qb, kb, vb):
    nb = qb.shape[0]
    n_slices = H_B * N_SEL_PAGES
    rows = pl.BlockSpec((nb, H_B * HEAD_DIM), lambda b, idx, pt: (0, 0))
    hbm = pl.BlockSpec(memory_space=pl.ANY)
    out = pl.pallas_call(
        _moba_attend_kernel,
        grid_spec=pltpu.PrefetchScalarGridSpec(
            num_scalar_prefetch=2, grid=(nb,),
            in_specs=[rows, rows, rows, hbm, hbm],
            out_specs=pl.BlockSpec((1, 1, H_B * HEAD_DIM), lambda b, idx, pt: (b, 0, 0)),
            scratch_shapes=[pltpu.VMEM((2, n_slices, PAGE_SIZE, HEAD_DIM), F32),
                            pltpu.VMEM((2, n_slices, PAGE_SIZE, HEAD_DIM), F32),
                            pltpu.SemaphoreType.DMA((2,))]),
        out_shape=jax.ShapeDtypeStruct((nb, 1, H_B * HEAD_DIM), F32),
        compiler_params=_params(("arbitrary",)),
        name="moba_attend",
    )(idx_flat, page_table, qb, kb, vb, cache_k, cache_v)
    return out.reshape(nb, H_B * HEAD_DIM)


def _out_ffn_kernel(x_ref, oa_ref, ob_ref, woa_ref, wob_ref, n2_ref, wg_ref, wu_ref, wd_ref, fn_ref, y_ref):
    tm = x_ref.shape[0]
    n_groups = 2 if tm % (2 * SUBLANES) == 0 and tm >= 256 else 1
    rows = [slice(g * tm // n_groups, (g + 1) * tm // n_groups) for g in range(n_groups)]
    x1 = [x_ref[r, :] + _dot(oa_ref[r, :].astype(BF16), woa_ref[...])
          + _dot(ob_ref[r, :].astype(BF16), wob_ref[...]) for r in rows]
    h = [_rmsnorm(t, n2_ref[...]).astype(BF16) for t in x1]
    gate = [_dot(t, wg_ref[...]) for t in h]
    up = [_dot(t, wu_ref[...]) for t in h]
    act = [(_silu(g) * u).astype(BF16) for g, u in zip(gate, up)]
    down = [_dot(t, wd_ref[...]) for t in act]
    for r, a, d in zip(rows, x1, down):
        y_ref[r, :] = _rmsnorm(a + d, fn_ref[...])


def _out_ffn(x, oa, ob, woa, wob, n2, wg, wu, wd, fn, tm):
    t = x.shape[0]
    row = lambda i: (i, 0)
    full = lambda a: _const_spec(a.shape)
    return pl.pallas_call(
        _out_ffn_kernel,
        grid=(t // tm,),
        in_specs=[pl.BlockSpec((tm, D_MODEL), row), pl.BlockSpec((tm, Z_COLS), row),
                  pl.BlockSpec((tm, H_B * HEAD_DIM), row), full(woa), full(wob), full(n2),
                  full(wg), full(wu), full(wd), full(fn)],
        out_specs=pl.BlockSpec((tm, D_MODEL), row),
        out_shape=jax.ShapeDtypeStruct((t, D_MODEL), F32),
        compiler_params=_params(("parallel",)),
        name="out_ffn",
    )(x, oa, ob, woa, wob, n2, wg, wu, wd, fn)


def _rotary_tables(pos):
    half = ROT_DIM // 2
    inv_freq = ROPE_THETA ** (-jnp.arange(half, dtype=F32) / half)
    ang = pos.astype(F32)[:, None] * inv_freq[None, :]
    cos, sin = jnp.cos(ang), jnp.sin(ang)
    n = pos.shape[0]
    rest = HEAD_DIM - ROT_DIM
    cos_t = jnp.concatenate([cos, cos, jnp.ones((n, rest), F32)], axis=1)
    zeros_h = jnp.zeros((n, half), F32)
    sin_a = jnp.concatenate([-sin, zeros_h, jnp.zeros((n, rest), F32)], axis=1)
    sin_b = jnp.concatenate([zeros_h, sin, jnp.zeros((n, rest), F32)], axis=1)
    return cos_t, sin_a, sin_b


def kernel(x_prompt, x_sample, cache_k, cache_v, page_table, state_gdn, state_conv, norm1_w, w_in, conv_w,
           a_log, dt_bias, gdn_norm_w, w_out, norm2_w, w_gate, w_up, w_down, final_norm_w):
    depth = w_in.shape[0]
    assert depth == 1, "single-layer trunk"
    batch, seq, _ = x_prompt.shape
    dec_batch, dec_seq, _ = x_sample.shape
    assert dec_seq == 1
    past_len = page_table.shape[1] * PAGE_SIZE

    wi = w_in[0]
    ab0 = GDN_QKV + Z_COLS
    w_all = jnp.concatenate(
        [wi[:, :ab0], wi[:, ab0 + 2 * H_A:], wi[:, ab0:ab0 + 2 * H_A],
         jnp.zeros((D_MODEL, AB_PAD - 2 * H_A), wi.dtype)], axis=1).astype(BF16)
    gpar = jnp.zeros((SUBLANES, LANES), F32)
    gpar = gpar.at[0, :H_A].set(a_log[0].astype(F32)).at[1, :H_A].set(dt_bias[0].astype(F32))
    nw1 = norm1_w[0].reshape(1, D_MODEL)
    cw = conv_w[0]
    gnw = gdn_norm_w[0].reshape(1, HEAD_DIM)
    woa = w_out[0][:Z_COLS].astype(BF16)
    wob = w_out[0][Z_COLS:].astype(BF16)
    n2 = norm2_w[0].reshape(1, D_MODEL)
    wg, wu, wd = w_gate[0].astype(BF16), w_up[0].astype(BF16), w_down[0].astype(BF16)
    fn = final_norm_w.reshape(1, D_MODEL)

    xp = x_prompt.reshape(batch * seq, D_MODEL)
    tabs_p = _rotary_tables(jnp.arange(seq, dtype=jnp.int32))
    qkva_p, z_p, gb_p, qb_p, kb_p, vb_p, k5_p, v5_p = _in_proj(xp, nw1, w_all, gpar, *tabs_p,
                                                               batch=batch, seq=seq, tm=512)
    oa_p, gdn_p, psums = _gdn_prompt(qkva_p, gb_p, z_p, cw, gnw, cache_k, page_table, batch, seq, tl=256)
    ob_p = _moba_prompt(qb_p, kb_p, vb_p, batch, seq)
    y_p = _out_ffn(xp, oa_p, ob_p, woa, wob, n2, wg, wu, wd, fn, tm=512)

    xs = x_sample.reshape(dec_batch, D_MODEL)
    tabs_s = _rotary_tables(jnp.full((dec_batch,), past_len, jnp.int32))
    qkva_s, z_s, gb_s, qb_s, kb_s, vb_s, k5_s, v5_s = _in_proj(xs, nw1, w_all, gpar, *tabs_s,
                                                               batch=1, seq=dec_batch, tm=dec_batch)
    sc = state_conv[0]
    oa_s, gdn_s = _gdn_sample(qkva_s, sc[:, 0], sc[:, 1], sc[:, 2], gb_s, z_s, cw, gnw, state_gdn[0])
    idx = _moba_select(qb_s, psums)
    idx_flat = idx[:, :H_B, :MOBA_TOPK].reshape(-1)
    ob_s = _moba_attend(idx_flat, page_table, cache_k, cache_v, qb_s, kb_s, vb_s)
    y_s = _out_ffn(xs, oa_s.reshape(dec_batch, Z_COLS), ob_s, woa, wob, n2, wg, wu, wd, fn, tm=dec_batch)

    conv_p = qkva_p.reshape(batch, seq, GDN_QKV)[:, seq - (CONV_W - 1):]
    conv_s = jnp.concatenate([sc[:, 1:], qkva_s[:, None, :]], axis=1)
    return (y_p.reshape(batch, seq, D_MODEL), y_s.reshape(dec_batch, 1, D_MODEL),
            k5_p, v5_p,
            k5_s.reshape(1, dec_batch, 1, H_B, HEAD_DIM), v5_s.reshape(1, dec_batch, 1, H_B, HEAD_DIM),
            gdn_p[None], gdn_s[None], conv_p[None], conv_s[None])
```

```python
import functools

import jax
import jax.numpy as jnp
import numpy as np
from jax import lax
from jax.experimental import pallas as pl
from jax.experimental.pallas import tpu as pltpu

D_MODEL = 1024
HEAD_DIM = 128
H_A = 4
H_B = 4
GDN_QKV = 3 * H_A * HEAD_DIM
MOBA_QKV = 3 * H_B * HEAD_DIM
Z_COLS = H_A * HEAD_DIM
CONV_W = 4
CHUNK = 64
SOLVE_ROWS = 128
MOBA_BLOCK = 256
MOBA_TOPK = 3
PAGE_SIZE = 128
PAGES_PER_BLOCK = MOBA_BLOCK // PAGE_SIZE
ROT_DIM = HEAD_DIM // 4
ROPE_THETA = 500000.0
EPS = 1e-6

LANES = 128
SUBLANES = 8
AB_PAD = LANES
PROJ_COLS = GDN_QKV + Z_COLS + MOBA_QKV + AB_PAD
VMEM_LIMIT = 56 * 1024 * 1024

F32 = jnp.float32
BF16 = jnp.bfloat16
HIGHEST = lax.Precision.HIGHEST
NEG_INF = float("-inf")
LOG2_E = 1.4426950408889634


def _dot(a, b, precision=None):
    return jnp.dot(a, b, precision=precision, preferred_element_type=F32)


def _dot_nt(a, b, precision=None):
    return lax.dot_general(a, b, (((1,), (1,)), ((), ())), precision=precision,
                           preferred_element_type=F32)


def _rmsnorm(x, w):
    return x * lax.rsqrt(jnp.mean(x * x, axis=-1, keepdims=True) + EPS) * w


def _l2norm(x):
    return x * lax.rsqrt(jnp.sum(x * x, axis=-1, keepdims=True) + EPS)


def _silu(x):
    return x * jax.nn.sigmoid(x)


def _const_spec(shape):
    return pl.BlockSpec(shape, lambda *_: (0,) * len(shape), pipeline_mode=pl.Buffered(1))


def _params(semantics):
    return pltpu.CompilerParams(dimension_semantics=semantics, vmem_limit_bytes=VMEM_LIMIT)


def _in_proj_kernel(x_ref, nw_ref, wa_ref, wb_ref, wab_ref, gpar_ref, cos_ref, sa_ref, sb_ref,
                    qkva_ref, z_ref, gb_ref, qb_ref, kb_ref, vb_ref, k5_ref, v5_ref):
    xn = _rmsnorm(x_ref[...], nw_ref[...]).astype(BF16)
    pb = _dot(xn, wb_ref[...])
    pa = _dot(xn, wa_ref[:, :GDN_QKV])
    cos, sa, sb = cos_ref[...], sa_ref[...], sb_ref[...]
    half = ROT_DIM // 2

    def rot(t):
        return (t * cos + pltpu.roll(t, HEAD_DIM - half, 1) * sa + pltpu.roll(t, half, 1) * sb)

    for h in range(H_B):
        lo, hi = h * HEAD_DIM, (h + 1) * HEAD_DIM
        qb_ref[:, lo:hi] = rot(pb[:, lo:hi])
        kh = rot(pb[:, H_B * HEAD_DIM + lo:H_B * HEAD_DIM + hi])
        vh = pb[:, 2 * H_B * HEAD_DIM + lo:2 * H_B * HEAD_DIM + hi]
        kb_ref[:, lo:hi] = kh
        vb_ref[:, lo:hi] = vh
        k5_ref[0, 0, :, h, :] = kh
        v5_ref[0, 0, :, h, :] = vh
    pz = _dot(xn, wa_ref[:, GDN_QKV:])
    ab = _dot(xn, wab_ref[...])
    qkva_ref[...] = pa.astype(BF16)
    z_ref[...] = pz.astype(BF16)
    lane = lax.broadcasted_iota(jnp.int32, ab.shape, 1)
    g = -jnp.exp(gpar_ref[0:1, :]) * jax.nn.softplus(ab + gpar_ref[1:2, :])
    gb_ref[...] = jnp.where(lane < H_A, g, jax.nn.sigmoid(ab))


def _in_proj(x, nw, wa, wb, wab, gpar, cos, sa, sb, batch, seq, tm):
    t = batch * seq
    n_t = seq // tm
    row = lambda i: (i, 0)
    tab = lambda i: (i % n_t, 0)
    kv = lambda i: (0, i // n_t, i % n_t, 0, 0)
    outs = ((GDN_QKV, BF16), (Z_COLS, BF16), (AB_PAD, F32), (H_B * HEAD_DIM, F32),
            (H_B * HEAD_DIM, F32), (H_B * HEAD_DIM, F32))
    kv_shape = jax.ShapeDtypeStruct((1, batch, seq, H_B, HEAD_DIM), F32)
    kv_spec = pl.BlockSpec((1, 1, tm, H_B, HEAD_DIM), kv)
    return pl.pallas_call(
        _in_proj_kernel,
        grid=(t // tm,),
        in_specs=[pl.BlockSpec((tm, D_MODEL), row), _const_spec((1, D_MODEL)),
                  _const_spec(wa.shape), _const_spec(wb.shape), _const_spec(wab.shape),
                  _const_spec((SUBLANES, LANES)),
                  pl.BlockSpec((tm, LANES), tab), pl.BlockSpec((tm, LANES), tab),
                  pl.BlockSpec((tm, LANES), tab)],
        out_specs=[pl.BlockSpec((tm, w), row) for w, _ in outs] + [kv_spec, kv_spec],
        out_shape=[jax.ShapeDtypeStruct((t, w), dt) for w, dt in outs] + [kv_shape, kv_shape],
        compiler_params=_params(("parallel",)),
        name="in_proj",
    )(x, nw, wa, wb, wab, gpar, cos, sa, sb)


def _gdn_head_norms(y):
    parts = []
    for h in range(H_A):
        parts.append(_l2norm(y[:, h * HEAD_DIM:(h + 1) * HEAD_DIM]) * (HEAD_DIM ** -0.5))
    for h in range(H_A, 2 * H_A):
        parts.append(_l2norm(y[:, h * HEAD_DIM:(h + 1) * HEAD_DIM]))
    return parts


def _gdn_out_norm(o, gnw, z):
    return _rmsnorm(o, gnw) * _silu(z)


def _sum_rows(x):
    while x.shape[0] > 1:
        half = x.shape[0] // 2
        x = x[:half] + x[half:]
    return x[0]


def _block_key_sum(pages_ref, r, out_ref):
    i = r * PAGES_PER_BLOCK
    s = _sum_rows(pages_ref[i])
    for j in range(1, PAGES_PER_BLOCK):
        s = s + _sum_rows(pages_ref[i + j])
    s = s[:H_B] + s[H_B:]
    for h in range(H_B):
        out_ref[0, r:r + 1, h * HEAD_DIM:(h + 1) * HEAD_DIM] = s[h:h + 1, :]


def _gdn_prompt_kernel(pt_ref, qkva_ref, gb_ref, z_ref, cw_ref, gnw_ref, cache_ref, *refs,
                       tl, n_t, n_tiles, n_pages, steps_per_row):
    step = pl.program_id(0)
    o_ref, sfin_ref, psum_ref, xbuf, u_s, wq_s, kdt_s, qk_s, egl_s, s_s, pbuf, psem = refs

    def page_copy(tile, i, half):
        page = pt_ref[tile // steps_per_row, (tile % steps_per_row) * n_pages + i]
        return pltpu.make_async_copy(cache_ref.at[0, page], pbuf.at[half, i], psem.at[half])

    @pl.when(step == 0)
    def _():
        xbuf[0:SUBLANES, :] = jnp.zeros((SUBLANES, GDN_QKV), F32)
        u_s[...] = jnp.zeros_like(u_s)
        wq_s[...] = jnp.zeros_like(wq_s)
        kdt_s[...] = jnp.zeros_like(kdt_s)
        qk_s[...] = jnp.zeros_like(qk_s)
        egl_s[...] = jnp.zeros_like(egl_s)
        s_s[...] = jnp.zeros_like(s_s)
        for i in range(n_pages):
            page_copy(0, i, 0).start()

    for parity in range(2):
        @pl.when(step % 2 == parity)
        def _():
            nxt = jnp.minimum(step + 1, n_tiles - 1)
            for i in range(n_pages):
                page_copy(nxt, i, 1 - parity).start()

            def wait_pages():
                for i in range(n_pages):
                    page_copy(0, i, parity).wait()

            _gdn_prompt_step(qkva_ref, gb_ref, z_ref, cw_ref, gnw_ref, o_ref, sfin_ref, psum_ref,
                             xbuf, u_s, wq_s, kdt_s, qk_s, egl_s, s_s, pbuf.at[parity], wait_pages,
                             tl=tl, n_t=n_t, n_pages=n_pages, wr=parity)

    @pl.when(step == n_tiles)
    def _():
        for i in range(n_pages):
            page_copy(0, i, (n_tiles + 1) % 2).wait()


def _gdn_prompt_step(qkva_ref, gb_ref, z_ref, cw_ref, gnw_ref, o_ref, sfin_ref, psum_ref,
                     xbuf, u_s, wq_s, kdt_s, qk_s, egl_s, s_s, pages, wait_pages, *, tl, n_t, n_pages, wr):
    wait_pages()
    step = pl.program_id(0)
    rd = 1 - wr
    a_starts_seq = (step % n_t) == 0
    b_starts_seq = ((step + n_t - 1) % n_t) == 0
    n_c = tl // CHUNK
    pending_blocks = list(range(n_pages // PAGES_PER_BLOCK))

    def reduce_cache_blocks(count):
        for _ in range(min(count, len(pending_blocks))):
            _block_key_sum(pages, pending_blocks.pop(0), psum_ref)

    tail = xbuf[0:SUBLANES, :]
    xbuf[0:SUBLANES, :] = jnp.where(a_starts_seq, 0.0, tail)
    xbuf[SUBLANES:SUBLANES + tl, :] = qkva_ref[...].astype(F32)
    cw = cw_ref[...]
    xs = xbuf[...]
    y = pltpu.roll(xs, CONV_W - 1, 0)[SUBLANES:] * cw[0:1]
    for j in range(1, CONV_W - 1):
        y = y + pltpu.roll(xs, CONV_W - 1 - j, 0)[SUBLANES:] * cw[j:j + 1]
    y = _silu(y + xs[SUBLANES:] * cw[CONV_W - 1:CONV_W])
    xbuf[0:SUBLANES, :] = xbuf[tl:tl + SUBLANES, :]
    qk_parts = _gdn_head_norms(y)

    ri = lax.broadcasted_iota(jnp.int32, (tl, tl), 0)
    ci = lax.broadcasted_iota(jnp.int32, (tl, tl), 1)
    same = (ri // CHUNK) == (ci // CHUNK)
    gb = gb_ref[...]
    sums = _dot(jnp.concatenate([(same & (ri >= ci)).astype(F32), same.astype(F32)], axis=0), gb, HIGHEST)
    gc_all, gl_all = sums[:tl], sums[tl:]
    gc_t = gc_all.T

    rg = lax.broadcasted_iota(jnp.int32, (SOLVE_ROWS, SOLVE_ROWS), 0)
    cg = lax.broadcasted_iota(jnp.int32, (SOLVE_ROWS, SOLVE_ROWS), 1)
    same_g = (rg // CHUNK) == (cg // CHUNK)
    tri = same_g & (rg >= cg)
    strict = same_g & (rg > cg)
    eye = (rg == cg).astype(F32)
    chunks_per_group = SOLVE_ROWS // CHUNK
    groups = [(h, grp) for h in range(H_A) for grp in range(tl // SOLVE_ROWS)]
    ps, tinvs, rhs = [], [], []
    for h, grp in groups:
        lo, hi = h * HEAD_DIM, (h + 1) * HEAD_DIM
        g0, g1 = grp * SOLVE_ROWS, (grp + 1) * SOLVE_ROWS
        q, k = qk_parts[h][g0:g1], qk_parts[H_A + h][g0:g1]
        v = y[g0:g1, 2 * H_A * HEAD_DIM + lo:2 * H_A * HEAD_DIM + hi]
        beta = gb[g0:g1, H_A + h:H_A + h + 1]
        gcol = gc_all[g0:g1, h:h + 1]
        glcol = gl_all[g0:g1, h:h + 1]
        decay = jnp.where(tri, jnp.exp(gcol - gc_t[h:h + 1, g0:g1]), 0.0)
        egc = jnp.exp(gcol)
        kbeta = k * beta
        qkk = _dot_nt(jnp.concatenate([q, kbeta], axis=0).astype(BF16), k.astype(BF16))
        qk = qkk[:SOLVE_ROWS] * decay
        p = -jnp.where(strict, qkk[SOLVE_ROWS:] * decay, 0.0)
        ps.append(p)
        tinvs.append(eye + p)
        rhs.append(jnp.concatenate([v * beta, kbeta * egc], axis=1).astype(BF16))
        qd = (q * egc).astype(BF16)
        kd = k * jnp.exp(glcol - gcol)
        egl = jnp.exp(glcol)
        for cc in range(chunks_per_group):
            c = grp * chunks_per_group + cc
            r0, r1 = cc * CHUNK, (cc + 1) * CHUNK
            wq_s[wr, h, c, CHUNK:2 * CHUNK, :] = qd[r0:r1]
            kdt_s[wr, h, c] = kd[r0:r1].T.astype(BF16)
            qk_s[wr, h, c] = qk[r0:r1, r0:r1].astype(BF16)
            egl_s[wr, h, c] = jnp.broadcast_to(egl[r0:r0 + 1, :], (SUBLANES, HEAD_DIM))

    gnw = gnw_ref[...]

    state, held = {}, {}

    def recurrence_first(c, h):
        s = state.get(h)
        if s is None:
            s = jnp.where(b_starts_seq, 0.0, s_s[h])
        held[h] = (s, _dot(wq_s[rd, h, c], s.astype(BF16)))

    def recurrence_second(c, h):
        r0, r1 = c * CHUNK, (c + 1) * CHUNK
        lo, hi = h * HEAD_DIM, (h + 1) * HEAD_DIM
        s, ws = held.pop(h)
        v_new = u_s[rd, r0:r1, lo:hi] - ws[:CHUNK]
        vnb = v_new.astype(BF16)
        o = ws[CHUNK:] + _dot(qk_s[rd, h, c], vnb)
        state[h] = s * egl_s[rd, h, c][0:1] + _dot(kdt_s[rd, h, c], vnb)
        o_ref[r0:r1, lo:hi] = _gdn_out_norm(o, gnw, z_ref[r0:r1, lo:hi].astype(F32)).astype(o_ref.dtype)
        if c == n_c - 1:
            s_s[h] = state[h]

    side_work = []
    for c in range(n_c):
        for piece in (recurrence_first, recurrence_second):
            for h in range(H_A):
                side_work.append(functools.partial(piece, c, h))
                side_work.append(functools.partial(reduce_cache_blocks, 1))
    n_iter = 5
    n_slots = 2 * n_iter * len(groups)
    done = [0]

    def side_slot(k):
        target = (k + 1) * len(side_work) // n_slots
        while done[0] < target:
            side_work[done[0]]()
            done[0] += 1

    slot = 0
    for it in range(n_iter):
        for i in range(len(groups)):
            pb = ps[i].astype(BF16)
            ps[i] = _dot(pb, pb)
            side_slot(slot)
            slot += 1
        for i in range(len(groups)):
            tinvs[i] = tinvs[i] + _dot(tinvs[i].astype(BF16), ps[i].astype(BF16))
            side_slot(slot)
            slot += 1
    for i, (h, grp) in enumerate(groups):
        lo, hi = h * HEAD_DIM, (h + 1) * HEAD_DIM
        g0, g1 = grp * SOLVE_ROWS, (grp + 1) * SOLVE_ROWS
        uw = _dot(tinvs[i].astype(BF16), rhs[i])
        u_s[wr, g0:g1, lo:hi] = uw[:, :HEAD_DIM]
        w = uw[:, HEAD_DIM:].astype(BF16)
        for cc in range(chunks_per_group):
            c = grp * chunks_per_group + cc
            wq_s[wr, h, c, 0:CHUNK, :] = w[cc * CHUNK:(cc + 1) * CHUNK]
    reduce_cache_blocks(len(pending_blocks))
    sfin_ref[0] = s_s[...]


def _gdn_prompt(qkva, gb, z, cw, gnw, cache_k, page_table, batch, seq, tl):
    n_t = seq // tl
    n_c = tl // CHUNK
    n_tiles = batch * n_t
    dec_batch, pages_per_row = page_table.shape
    n_pages = dec_batch * pages_per_row // n_tiles
    steps_per_row = pages_per_row // n_pages
    assert n_pages * n_tiles == dec_batch * pages_per_row and steps_per_row * n_pages == pages_per_row
    blocks_per_step = n_pages // PAGES_PER_BLOCK
    width = H_B * HEAD_DIM
    cache_pairs = cache_k.reshape(1, cache_k.shape[1], PAGE_SIZE // 2, 2 * H_B, HEAD_DIM)
    tile_a = lambda s: jnp.minimum(s, n_tiles - 1)
    tile_b = lambda s: jnp.maximum(s - 1, 0)
    row_a = lambda s, pt: (tile_a(s), 0)
    row_b = lambda s, pt: (tile_b(s), 0)
    const = lambda shape: pl.BlockSpec(shape, lambda s, pt: (0,) * len(shape), pipeline_mode=pl.Buffered(1))

    def psum_index(s, pt):
        a = tile_a(s)
        return (a // steps_per_row, a % steps_per_row, 0)

    return pl.pallas_call(
        functools.partial(_gdn_prompt_kernel, tl=tl, n_t=n_t, n_tiles=n_tiles, n_pages=n_pages,
                          steps_per_row=steps_per_row),
        grid_spec=pltpu.PrefetchScalarGridSpec(
            num_scalar_prefetch=1, grid=(n_tiles + 1,),
            in_specs=[pl.BlockSpec((tl, GDN_QKV), row_a), pl.BlockSpec((tl, AB_PAD), row_a),
                      pl.BlockSpec((tl, Z_COLS), row_b), const((CONV_W, GDN_QKV)), const((1, HEAD_DIM)),
                      pl.BlockSpec(memory_space=pl.ANY)],
            out_specs=[pl.BlockSpec((tl, Z_COLS), row_b),
                       pl.BlockSpec((1, H_A, HEAD_DIM, HEAD_DIM), lambda s, pt: (tile_b(s) // n_t, 0, 0, 0)),
                       pl.BlockSpec((1, blocks_per_step, width), psum_index)],
            scratch_shapes=[pltpu.VMEM((tl + SUBLANES, GDN_QKV), F32),
                            pltpu.VMEM((2, tl, Z_COLS), F32),
                            pltpu.VMEM((2, H_A, n_c, 2 * CHUNK, HEAD_DIM), BF16),
                            pltpu.VMEM((2, H_A, n_c, HEAD_DIM, CHUNK), BF16),
                            pltpu.VMEM((2, H_A, n_c, CHUNK, CHUNK), BF16),
                            pltpu.VMEM((2, H_A, n_c, SUBLANES, HEAD_DIM), F32),
                            pltpu.VMEM((H_A, HEAD_DIM, HEAD_DIM), F32),
                            pltpu.VMEM((2, n_pages, PAGE_SIZE // 2, 2 * H_B, HEAD_DIM), F32),
                            pltpu.SemaphoreType.DMA((2,))]),
        out_shape=[jax.ShapeDtypeStruct((batch * seq, Z_COLS), BF16),
                   jax.ShapeDtypeStruct((batch, H_A, HEAD_DIM, HEAD_DIM), F32),
                   jax.ShapeDtypeStruct((dec_batch, pages_per_row // PAGES_PER_BLOCK, width), F32)],
        compiler_params=_params(("arbitrary",)),
        name="gdn_prompt",
    )(page_table, qkva, gb, z, cw, gnw, cache_pairs)


def _row0(x, rows=SUBLANES):
    r = lax.broadcasted_iota(jnp.int32, (rows, x.shape[1]), 0)
    return jnp.where(r == 0, jnp.broadcast_to(x, (rows, x.shape[1])), 0.0)


GDN_SAMPLE_ROWS = SUBLANES


def _gdn_sample_kernel(xn_ref, p0_ref, p1_ref, p2_ref, gb_ref, z_ref, cw_ref, gnw_ref, s0_ref,
                       o_ref, sout_ref):
    r0 = pl.multiple_of(pl.program_id(0) * GDN_SAMPLE_ROWS, GDN_SAMPLE_ROWS)
    rows = lambda r: r[pl.ds(r0, GDN_SAMPLE_ROWS), :]
    cw = cw_ref[...]
    y = rows(p0_ref) * cw[0:1]
    y = y + rows(p1_ref) * cw[1:2]
    y = y + rows(p2_ref) * cw[2:3]
    y = y + rows(xn_ref) * cw[3:4]
    y = _silu(y)
    qk = _gdn_head_norms(y)
    gbr = rows(gb_ref)
    zr = rows(z_ref)
    gnw = gnw_ref[...]
    ri = lax.broadcasted_iota(jnp.int32, (HEAD_DIM, HEAD_DIM), 0)
    ci = lax.broadcasted_iota(jnp.int32, (HEAD_DIM, HEAD_DIM), 1)
    eye = (ri == ci).astype(BF16)
    pad = 2 * SUBLANES
    for j in range(GDN_SAMPLE_ROWS):
        for h in range(H_A):
            lo, hi = h * HEAD_DIM, (h + 1) * HEAD_DIM
            q, k = qk[h][j:j + 1], qk[H_A + h][j:j + 1]
            v = y[j:j + 1, 2 * H_A * HEAD_DIM + lo:2 * H_A * HEAD_DIM + hi]
            eg = jnp.exp(gbr[j:j + 1, h:h + 1])
            beta = gbr[j:j + 1, H_A + h:H_A + h + 1]
            s0 = s0_ref[j, h]
            kp = _row0(k, pad).astype(BF16)
            v_new = beta * (v - eg * _dot(kp, s0.astype(BF16))[0:1])
            k_col = _dot_nt(eye, kp)[:, 0:1]
            s_new = s0 * eg + k_col * v_new
            o = _dot(_row0(q, pad).astype(BF16), s_new.astype(BF16))[0:1]
            o_ref[j, :, lo:hi] = _gdn_out_norm(o, gnw, zr[j:j + 1, lo:hi])
            sout_ref[j, h] = s_new


def _gdn_sample(xn, p0, p1, p2, gb, z, cw, gnw, s0):
    nb = xn.shape[0]
    full = lambda a: _const_spec(a.shape)
    st = pl.BlockSpec((GDN_SAMPLE_ROWS, H_A, HEAD_DIM, HEAD_DIM), lambda b: (b, 0, 0, 0))
    return pl.pallas_call(
        _gdn_sample_kernel,
        grid=(nb // GDN_SAMPLE_ROWS,),
        in_specs=[full(xn), full(p0), full(p1), full(p2), full(gb), full(z), full(cw), full(gnw), st],
        out_specs=[pl.BlockSpec((GDN_SAMPLE_ROWS, 1, Z_COLS), lambda b: (b, 0, 0)), st],
        out_shape=[jax.ShapeDtypeStruct((nb, 1, Z_COLS), F32),
                   jax.ShapeDtypeStruct(s0.shape, F32)],
        compiler_params=_params(("parallel",)),
        name="gdn_sample",
    )(xn, p0, p1, p2, gb, z, cw, gnw, s0)


def _moba_prompt_kernel(q_ref, k_ref, v_ref, o_ref, *, seq):
    nb = seq // MOBA_BLOCK
    k = k_ref[...]
    kb = k.astype(BF16)
    v_t = jnp.concatenate([v_ref[...].T, jnp.ones((SUBLANES, seq), F32)], axis=0).astype(BF16)
    km = jnp.concatenate(
        [jnp.sum(k[j * MOBA_BLOCK:(j + 1) * MOBA_BLOCK], axis=0, keepdims=True) * (1.0 / MOBA_BLOCK)
         for j in range(nb)], axis=0)
    blk = lax.broadcasted_iota(jnp.int32, (nb, MOBA_BLOCK), 0)
    ki = lax.broadcasted_iota(jnp.int32, (MOBA_BLOCK, MOBA_BLOCK), 0)
    qi = lax.broadcasted_iota(jnp.int32, (MOBA_BLOCK, MOBA_BLOCK), 1)
    causal_bias = jnp.where(ki <= qi, 0.0, NEG_INF)
    scale = HEAD_DIM ** -0.5 * LOG2_E

    def logits(n):
        q = q_ref[n * MOBA_BLOCK:(n + 1) * MOBA_BLOCK, :]
        gate = _dot_nt(km, q, HIGHEST)
        qs = (q * scale).astype(BF16)
        cnt = jnp.zeros((nb, MOBA_BLOCK), jnp.int32)
        for i in range(n):
            gi = gate[i:i + 1, :]
            beats = (gi > gate) | ((gi == gate) & (i < blk))
            cnt = cnt + beats.astype(jnp.int32)
        sel_bias = jnp.where((cnt < MOBA_TOPK) & (blk < n), 0.0, NEG_INF)
        tiles = [_dot_nt(kb[j * MOBA_BLOCK:(j + 1) * MOBA_BLOCK], qs) + sel_bias[j:j + 1, :] for j in range(n)]
        tiles.append(_dot_nt(kb[n * MOBA_BLOCK:(n + 1) * MOBA_BLOCK], qs) + causal_bias)
        return tiles

    def probabilities(tiles):
        m = functools.reduce(jnp.maximum, [jnp.max(t, axis=0, keepdims=True) for t in tiles])
        return jnp.concatenate([jnp.exp2(t - m).astype(BF16) for t in tiles], axis=0)

    def weighted_values(n, p):
        pv = _dot(v_t[:, :(n + 1) * MOBA_BLOCK], p)
        o_t = pv[:HEAD_DIM] / pv[HEAD_DIM:HEAD_DIM + 1]
        o_ref[n * MOBA_BLOCK:(n + 1) * MOBA_BLOCK, :] = o_t.T.astype(o_ref.dtype)

    ahead = 2
    pending = {n: logits(n) for n in range(min(ahead, nb))}
    for n in range(nb):
        p = probabilities(pending.pop(n))
        if n + ahead < nb:
            pending[n + ahead] = logits(n + ahead)
        weighted_values(n, p)


def _moba_prompt(qb, kb, vb, batch, seq):
    spec = pl.BlockSpec((seq, HEAD_DIM), lambda b, h: (b, h))
    return pl.pallas_call(
        functools.partial(_moba_prompt_kernel, seq=seq),
        grid=(batch, H_B),
        in_specs=[spec, spec, spec],
        out_specs=spec,
        out_shape=jax.ShapeDtypeStruct((batch * seq, H_B * HEAD_DIM), BF16),
        compiler_params=_params(("parallel", "parallel")),
        name="moba_prompt",
    )(qb, kb, vb)


SELECT_ROWS = SUBLANES


def _moba_select_kernel(q_ref, ps_ref, idx_ref):
    r0 = pl.multiple_of(pl.program_id(0) * SELECT_ROWS, SELECT_ROWS)
    qs = q_ref[pl.ds(r0, SELECT_ROWS), :]
    n_blocks = ps_ref.shape[1]
    row = lax.broadcasted_iota(jnp.int32, (SUBLANES, LANES), 0)
    lane = lax.broadcasted_iota(jnp.int32, (SUBLANES, LANES), 1)
    blk = lax.broadcasted_iota(jnp.int32, (1, n_blocks), 1).astype(F32)
    for j in range(SELECT_ROWS):
        km = ps_ref[j] * (1.0 / MOBA_BLOCK)
        out = jnp.zeros((SUBLANES, LANES), F32)
        for h in range(H_B):
            lo, hi = h * HEAD_DIM, (h + 1) * HEAD_DIM
            g = _dot_nt(_row0(qs[j:j + 1, lo:hi]), km[:, lo:hi], HIGHEST)[0:1]
            for s in range(MOBA_TOPK):
                m = jnp.max(g, axis=1, keepdims=True)
                idx = jnp.min(jnp.where(g == m, blk, float(n_blocks)), axis=1, keepdims=True)
                out = jnp.where((row == h) & (lane == s), idx, out)
                g = jnp.where(blk == idx, NEG_INF, g)
        idx_ref[j] = out.astype(jnp.int32)


def _moba_select(qb, psums):
    nb, n_blocks, width = psums.shape
    return pl.pallas_call(
        _moba_select_kernel,
        grid=(nb // SELECT_ROWS,),
        in_specs=[_const_spec(qb.shape), pl.BlockSpec((SELECT_ROWS, n_blocks, width), lambda b: (b, 0, 0))],
        out_specs=pl.BlockSpec((SELECT_ROWS, SUBLANES, LANES), lambda b: (b, 0, 0)),
        out_shape=jax.ShapeDtypeStruct((nb, SUBLANES, LANES), jnp.int32),
        compiler_params=_params(("parallel",)),
        name="moba_select",
    )(qb, psums)


N_SEL_PAGES = MOBA_TOPK * PAGES_PER_BLOCK


def _moba_attend_kernel(idx_ref, pt_ref, q_ref, k_ref, v_ref, ck_ref, cv_ref, o_ref, kbuf, vbuf, sem):
    b = pl.program_id(0)
    n_rows = pl.num_programs(0)

    def copies(row, half):
        out = []
        for h in range(H_B):
            for s in range(MOBA_TOPK):
                blk = idx_ref[(row * H_B + h) * MOBA_TOPK + s]
                for p in range(PAGES_PER_BLOCK):
                    page = pt_ref[row, blk * PAGES_PER_BLOCK + p]
                    j = (h * MOBA_TOPK + s) * PAGES_PER_BLOCK + p
                    out.append(pltpu.make_async_copy(ck_ref.at[0, page, :, h, :], kbuf.at[half, j], sem.at[half]))
                    out.append(pltpu.make_async_copy(cv_ref.at[0, page, :, h, :], vbuf.at[half, j], sem.at[half]))
        return out

    @pl.when(b == 0)
    def _():
        for c in copies(0, 0):
            c.start()

    scale = HEAD_DIM ** -0.5
    for parity in range(2):
        @pl.when(b % 2 == parity)
        def _():
            for c in copies(jnp.minimum(b + 1, n_rows - 1), 1 - parity):
                c.start()
            for c in copies(0, parity):
                c.wait()
            q = q_ref[pl.ds(b, 1), :]
            k_new = k_ref[pl.ds(b, 1), :]
            v_new = v_ref[pl.ds(b, 1), :]
            for h in range(H_B):
                lo, hi = h * HEAD_DIM, (h + 1) * HEAD_DIM
                j0 = h * N_SEL_PAGES
                keys = kbuf[parity, j0:j0 + N_SEL_PAGES].reshape(N_SEL_PAGES * PAGE_SIZE, HEAD_DIM)
                vals = vbuf[parity, j0:j0 + N_SEL_PAGES].reshape(N_SEL_PAGES * PAGE_SIZE, HEAD_DIM)
                qh = q[:, lo:hi]
                s = _dot_nt(_row0(qh, 2 * SUBLANES).astype(BF16), keys.astype(BF16))[0:1] * scale
                s_new = jnp.sum(qh * k_new[:, lo:hi], axis=1, keepdims=True) * scale
                m = jnp.maximum(jnp.max(s, axis=1, keepdims=True), s_new)
                p = jnp.exp(s - m)
                p_new = jnp.exp(s_new - m)
                l = jnp.sum(p, axis=1, keepdims=True) + p_new
                pv = _dot(_row0(p, 2 * SUBLANES).astype(BF16), vals.astype(BF16))[0:1]
                o_ref[0, :, lo:hi] = (pv + p_new * v_new[:, lo:hi]) / l

    @pl.when(b == n_rows - 1)
    def _():
        for c in copies(0, n_rows % 2):
            c.wait()


def _moba_attend(idx_flat, page_table, cache_k, cache_v, qb, kb, vb):
    nb = qb.shape[0]
    n_slices = H_B * N_SEL_PAGES
    rows = pl.BlockSpec((nb, H_B * HEAD_DIM), lambda b, idx, pt: (0, 0))
    hbm = pl.BlockSpec(memory_space=pl.ANY)
    out = pl.pallas_call(
        _moba_attend_kernel,
        grid_spec=pltpu.PrefetchScalarGridSpec(
            num_scalar_prefetch=2, grid=(nb,),
            in_specs=[rows, rows, rows, hbm, hbm],
            out_specs=pl.BlockSpec((1, 1, H_B * HEAD_DIM), lambda b, idx, pt: (b, 0, 0)),
            scratch_shapes=[pltpu.VMEM((2, n_slices, PAGE_SIZE, HEAD_DIM), F32),
                            pltpu.VMEM((2, n_slices, PAGE_SIZE, HEAD_DIM), F32),
                            pltpu.SemaphoreType.DMA((2,))]),
        out_shape=jax.ShapeDtypeStruct((nb, 1, H_B * HEAD_DIM), F32),
        compiler_params=_params(("arbitrary",)),
        name="moba_attend",
    )(idx_flat, page_table, qb, kb, vb, cache_k, cache_v)
    return out.reshape(nb, H_B * HEAD_DIM)


def _out_ffn_kernel(x_ref, oa_ref, ob_ref, woa_ref, wob_ref, n2_ref, wg_ref, wu_ref, wd_ref, fn_ref, y_ref):
    tm = x_ref.shape[0]
    n_groups = 2 if tm % (2 * SUBLANES) == 0 and tm >= 256 else 1
    rows = [slice(g * tm // n_groups, (g + 1) * tm // n_groups) for g in range(n_groups)]
    x1 = [x_ref[r, :] + _dot(oa_ref[r, :].astype(BF16), woa_ref[...])
          + _dot(ob_ref[r, :].astype(BF16), wob_ref[...]) for r in rows]
    h = [_rmsnorm(t, n2_ref[...]).astype(BF16) for t in x1]
    gate = [_dot(t, wg_ref[...]) for t in h]
    up = [_dot(t, wu_ref[...]) for t in h]
    act = [(_silu(g) * u).astype(BF16) for g, u in zip(gate, up)]
    down = [_dot(t, wd_ref[...]) for t in act]
    for r, a, d in zip(rows, x1, down):
        y_ref[r, :] = _rmsnorm(a + d, fn_ref[...])


def _out_ffn(x, oa, ob, woa, wob, n2, wg, wu, wd, fn, tm):
    t = x.shape[0]
    row = lambda i: (i, 0)
    full = lambda a: _const_spec(a.shape)
    return pl.pallas_call(
        _out_ffn_kernel,
        grid=(t // tm,),
        in_specs=[pl.BlockSpec((tm, D_MODEL), row), pl.BlockSpec((tm, Z_COLS), row),
                  pl.BlockSpec((tm, H_B * HEAD_DIM), row), full(woa), full(wob), full(n2),
                  full(wg), full(wu), full(wd), full(fn)],
        out_specs=pl.BlockSpec((tm, D_MODEL), row),
        out_shape=jax.ShapeDtypeStruct((t, D_MODEL), F32),
        compiler_params=_params(("parallel",)),
        name="out_ffn",
    )(x, oa, ob, woa, wob, n2, wg, wu, wd, fn)


def _rotary_tables(pos):
    pos = np.asarray(pos, np.float64)
    half = ROT_DIM // 2
    inv_freq = ROPE_THETA ** (-np.arange(half, dtype=np.float64) / half)
    ang = pos[:, None] * inv_freq[None, :]
    cos, sin = np.cos(ang).astype(np.float32), np.sin(ang).astype(np.float32)
    n = pos.shape[0]
    rest = HEAD_DIM - ROT_DIM
    cos_t = np.concatenate([cos, cos, np.ones((n, rest), np.float32)], axis=1)
    zeros_h = np.zeros((n, half), np.float32)
    sin_a = np.concatenate([-sin, zeros_h, np.zeros((n, rest), np.float32)], axis=1)
    sin_b = np.concatenate([zeros_h, sin, np.zeros((n, rest), np.float32)], axis=1)
    return jnp.asarray(cos_t), jnp.asarray(sin_a), jnp.asarray(sin_b)


def kernel(x_prompt, x_sample, cache_k, cache_v, page_table, state_gdn, state_conv, norm1_w, w_in, conv_w,
           a_log, dt_bias, gdn_norm_w, w_out, norm2_w, w_gate, w_up, w_down, final_norm_w):
    depth = w_in.shape[0]
    assert depth == 1, "single-layer trunk"
    batch, seq, _ = x_prompt.shape
    dec_batch, dec_seq, _ = x_sample.shape
    assert dec_seq == 1
    past_len = page_table.shape[1] * PAGE_SIZE

    wi = w_in[0]
    ab0 = GDN_QKV + Z_COLS
    w_a = wi[:, :ab0].astype(BF16)
    w_b = wi[:, ab0 + 2 * H_A:].astype(BF16)
    w_ab = jnp.pad(wi[:, ab0:ab0 + 2 * H_A], ((0, 0), (0, AB_PAD - 2 * H_A))).astype(BF16)
    gpar = jnp.zeros((SUBLANES, LANES), F32)
    gpar = gpar.at[0, :H_A].set(a_log[0].astype(F32)).at[1, :H_A].set(dt_bias[0].astype(F32))
    nw1 = norm1_w[0].reshape(1, D_MODEL)
    cw = conv_w[0]
    gnw = gdn_norm_w[0].reshape(1, HEAD_DIM)
    woa = w_out[0][:Z_COLS].astype(BF16)
    wob = w_out[0][Z_COLS:].astype(BF16)
    n2 = norm2_w[0].reshape(1, D_MODEL)
    wg, wu, wd = w_gate[0].astype(BF16), w_up[0].astype(BF16), w_down[0].astype(BF16)
    fn = final_norm_w.reshape(1, D_MODEL)

    xp = x_prompt.reshape(batch * seq, D_MODEL)
    tabs_p = _rotary_tables(np.arange(seq))
    qkva_p, z_p, gb_p, qb_p, kb_p, vb_p, k5_p, v5_p = _in_proj(xp, nw1, w_a, w_b, w_ab, gpar, *tabs_p,
                                                               batch=batch, seq=seq, tm=512)
    oa_p, gdn_p, psums = _gdn_prompt(qkva_p, gb_p, z_p, cw, gnw, cache_k, page_table, batch, seq, tl=256)
    ob_p = _moba_prompt(qb_p, kb_p, vb_p, batch, seq)
    y_p = _out_ffn(xp, oa_p, ob_p, woa, wob, n2, wg, wu, wd, fn, tm=512)

    xs = x_sample.reshape(dec_batch, D_MODEL)
    tabs_s = _rotary_tables(np.full((dec_batch,), past_len))
    qkva_s, z_s, gb_s, qb_s, kb_s, vb_s, k5_s, v5_s = _in_proj(xs, nw1, w_a, w_b, w_ab, gpar, *tabs_s,
                                                               batch=1, seq=dec_batch, tm=dec_batch)
    qkva_s, z_s = qkva_s.astype(F32), z_s.astype(F32)
    sc = state_conv[0]
    oa_s, gdn_s = _gdn_sample(qkva_s, sc[:, 0], sc[:, 1], sc[:, 2], gb_s, z_s, cw, gnw, state_gdn[0])
    idx = _moba_select(qb_s, psums)
    idx_flat = idx[:, :H_B, :MOBA_TOPK].reshape(-1)
    ob_s = _moba_attend(idx_flat, page_table, cache_k, cache_v, qb_s, kb_s, vb_s)
    y_s = _out_ffn(xs, oa_s.reshape(dec_batch, Z_COLS), ob_s, woa, wob, n2, wg, wu, wd, fn, tm=dec_batch)

    conv_p = qkva_p.reshape(batch, seq, GDN_QKV)[:, seq - (CONV_W - 1):].astype(F32)
    conv_s = jnp.concatenate([sc[:, 1:], qkva_s[:, None, :]], axis=1)
    return (y_p.reshape(batch, seq, D_MODEL), y_s.reshape(dec_batch, 1, D_MODEL),
            k5_p, v5_p,
            k5_s.reshape(1, dec_batch, 1, H_B, HEAD_DIM), v5_s.reshape(1, dec_batch, 1, H_B, HEAD_DIM),
            gdn_p[None], gdn_s[None], conv_p[None], conv_s[None])
```

```python
import functools

import jax
import jax.numpy as jnp
import numpy as np
from jax import lax
from jax.experimental import pallas as pl
from jax.experimental.pallas import tpu as pltpu

D_MODEL = 1024
HEAD_DIM = 128
H_A = 4
H_B = 4
GDN_QKV = 3 * H_A * HEAD_DIM
MOBA_QKV = 3 * H_B * HEAD_DIM
Z_COLS = H_A * HEAD_DIM
CONV_W = 4
CHUNK = 64
SOLVE_ROWS = 128
MOBA_BLOCK = 256
MOBA_TOPK = 3
PAGE_SIZE = 128
PAGES_PER_BLOCK = MOBA_BLOCK // PAGE_SIZE
ROT_DIM = HEAD_DIM // 4
ROPE_THETA = 500000.0
EPS = 1e-6

LANES = 128
SUBLANES = 8
AB_PAD = LANES
PROJ_COLS = GDN_QKV + Z_COLS + MOBA_QKV + AB_PAD
VMEM_LIMIT = 56 * 1024 * 1024

F32 = jnp.float32
BF16 = jnp.bfloat16
HIGHEST = lax.Precision.HIGHEST
NEG_INF = float("-inf")
LOG2_E = 1.4426950408889634


def _dot(a, b, precision=None):
    return jnp.dot(a, b, precision=precision, preferred_element_type=F32)


def _dot_nt(a, b, precision=None):
    return lax.dot_general(a, b, (((1,), (1,)), ((), ())), precision=precision,
                           preferred_element_type=F32)


def _rmsnorm(x, w):
    return x * lax.rsqrt(jnp.mean(x * x, axis=-1, keepdims=True) + EPS) * w


def _l2norm(x):
    return x * lax.rsqrt(jnp.sum(x * x, axis=-1, keepdims=True) + EPS)


def _silu(x):
    return x * jax.nn.sigmoid(x)


def _const_spec(shape):
    return pl.BlockSpec(shape, lambda *_: (0,) * len(shape), pipeline_mode=pl.Buffered(1))


def _params(semantics):
    return pltpu.CompilerParams(dimension_semantics=semantics, vmem_limit_bytes=VMEM_LIMIT)


def _in_proj_kernel(x_ref, nw_ref, wa_ref, wb_ref, wab_ref, gpar_ref, cos_ref, sa_ref, sb_ref,
                    qkva_ref, z_ref, gb_ref, qb_ref, kb_ref, vb_ref, k5_ref, v5_ref):
    xn = _rmsnorm(x_ref[...], nw_ref[...]).astype(BF16)
    pb = _dot(xn, wb_ref[...])
    pa = _dot(xn, wa_ref[:, :GDN_QKV])
    cos, sa, sb = cos_ref[...], sa_ref[...], sb_ref[...]
    half = ROT_DIM // 2

    def rot(t):
        return (t * cos + pltpu.roll(t, HEAD_DIM - half, 1) * sa + pltpu.roll(t, half, 1) * sb)

    for h in range(H_B):
        lo, hi = h * HEAD_DIM, (h + 1) * HEAD_DIM
        qb_ref[:, lo:hi] = rot(pb[:, lo:hi])
        kh = rot(pb[:, H_B * HEAD_DIM + lo:H_B * HEAD_DIM + hi])
        vh = pb[:, 2 * H_B * HEAD_DIM + lo:2 * H_B * HEAD_DIM + hi]
        kb_ref[:, lo:hi] = kh
        vb_ref[:, lo:hi] = vh
        k5_ref[0, 0, :, h, :] = kh
        v5_ref[0, 0, :, h, :] = vh
    pz = _dot(xn, wa_ref[:, GDN_QKV:])
    ab = _dot(xn, wab_ref[...])
    qkva_ref[...] = pa.astype(BF16)
    z_ref[...] = pz.astype(BF16)
    lane = lax.broadcasted_iota(jnp.int32, ab.shape, 1)
    g = -jnp.exp(gpar_ref[0:1, :]) * jax.nn.softplus(ab + gpar_ref[1:2, :])
    gb_ref[...] = jnp.where(lane < H_A, g, jax.nn.sigmoid(ab))


def _in_proj(x, nw, wa, wb, wab, gpar, cos, sa, sb, batch, seq, tm):
    t = batch * seq
    n_t = seq // tm
    row = lambda i: (i, 0)
    tab = lambda i: (i % n_t, 0)
    kv = lambda i: (0, i // n_t, i % n_t, 0, 0)
    outs = ((GDN_QKV, BF16), (Z_COLS, BF16), (AB_PAD, F32), (H_B * HEAD_DIM, F32),
            (H_B * HEAD_DIM, F32), (H_B * HEAD_DIM, F32))
    kv_shape = jax.ShapeDtypeStruct((1, batch, seq, H_B, HEAD_DIM), F32)
    kv_spec = pl.BlockSpec((1, 1, tm, H_B, HEAD_DIM), kv)
    return pl.pallas_call(
        _in_proj_kernel,
        grid=(t // tm,),
        in_specs=[pl.BlockSpec((tm, D_MODEL), row), _const_spec((1, D_MODEL)),
                  _const_spec(wa.shape), _const_spec(wb.shape), _const_spec(wab.shape),
                  _const_spec((SUBLANES, LANES)),
                  pl.BlockSpec((tm, LANES), tab), pl.BlockSpec((tm, LANES), tab),
                  pl.BlockSpec((tm, LANES), tab)],
        out_specs=[pl.BlockSpec((tm, w), row) for w, _ in outs] + [kv_spec, kv_spec],
        out_shape=[jax.ShapeDtypeStruct((t, w), dt) for w, dt in outs] + [kv_shape, kv_shape],
        compiler_params=_params(("parallel",)),
        name="in_proj",
    )(x, nw, wa, wb, wab, gpar, cos, sa, sb)


def _gdn_head_norms(y):
    parts = []
    for h in range(H_A):
        parts.append(_l2norm(y[:, h * HEAD_DIM:(h + 1) * HEAD_DIM]) * (HEAD_DIM ** -0.5))
    for h in range(H_A, 2 * H_A):
        parts.append(_l2norm(y[:, h * HEAD_DIM:(h + 1) * HEAD_DIM]))
    return parts


def _gdn_out_norm(o, gnw, z):
    return _rmsnorm(o, gnw) * _silu(z)


def _sum_rows(x):
    while x.shape[0] > 1:
        half = x.shape[0] // 2
        x = x[:half] + x[half:]
    return x[0]


def _block_key_sum(pages_ref, r, out_ref):
    i = r * PAGES_PER_BLOCK
    s = _sum_rows(pages_ref[i])
    for j in range(1, PAGES_PER_BLOCK):
        s = s + _sum_rows(pages_ref[i + j])
    s = s[:H_B] + s[H_B:]
    for h in range(H_B):
        out_ref[0, r:r + 1, h * HEAD_DIM:(h + 1) * HEAD_DIM] = s[h:h + 1, :]


def _gdn_prompt_kernel(pt_ref, qkva_ref, gb_ref, z_ref, cw_ref, gnw_ref, cache_ref, *refs,
                       tl, n_t, n_tiles, n_pages, steps_per_row):
    step = pl.program_id(0)
    o_ref, sfin_ref, psum_ref, xbuf, u_s, wq_s, kdt_s, qk_s, egl_s, s_s, pbuf, psem = refs

    def page_copy(tile, i, half):
        page = pt_ref[tile // steps_per_row, (tile % steps_per_row) * n_pages + i]
        return pltpu.make_async_copy(cache_ref.at[0, page], pbuf.at[half, i], psem.at[half])

    @pl.when(step == 0)
    def _():
        xbuf[0:SUBLANES, :] = jnp.zeros((SUBLANES, GDN_QKV), F32)
        u_s[...] = jnp.zeros_like(u_s)
        wq_s[...] = jnp.zeros_like(wq_s)
        kdt_s[...] = jnp.zeros_like(kdt_s)
        qk_s[...] = jnp.zeros_like(qk_s)
        egl_s[...] = jnp.zeros_like(egl_s)
        s_s[...] = jnp.zeros_like(s_s)
        for i in range(n_pages):
            page_copy(0, i, 0).start()

    for parity in range(2):
        @pl.when(step % 2 == parity)
        def _():
            nxt = jnp.minimum(step + 1, n_tiles - 1)
            for i in range(n_pages):
                page_copy(nxt, i, 1 - parity).start()

            def wait_pages():
                for i in range(n_pages):
                    page_copy(0, i, parity).wait()

            _gdn_prompt_step(qkva_ref, gb_ref, z_ref, cw_ref, gnw_ref, o_ref, sfin_ref, psum_ref,
                             xbuf, u_s, wq_s, kdt_s, qk_s, egl_s, s_s, pbuf.at[parity], wait_pages,
                             tl=tl, n_t=n_t, n_pages=n_pages, wr=parity)

    @pl.when(step == n_tiles)
    def _():
        for i in range(n_pages):
            page_copy(0, i, (n_tiles + 1) % 2).wait()


def _gdn_prompt_step(qkva_ref, gb_ref, z_ref, cw_ref, gnw_ref, o_ref, sfin_ref, psum_ref,
                     xbuf, u_s, wq_s, kdt_s, qk_s, egl_s, s_s, pages, wait_pages, *, tl, n_t, n_pages, wr):
    wait_pages()
    step = pl.program_id(0)
    rd = 1 - wr
    a_starts_seq = (step % n_t) == 0
    b_starts_seq = ((step + n_t - 1) % n_t) == 0
    n_c = tl // CHUNK
    pending_blocks = list(range(n_pages // PAGES_PER_BLOCK))

    def reduce_cache_blocks(count):
        for _ in range(min(count, len(pending_blocks))):
            _block_key_sum(pages, pending_blocks.pop(0), psum_ref)

    tail = xbuf[0:SUBLANES, :]
    xbuf[0:SUBLANES, :] = jnp.where(a_starts_seq, 0.0, tail)
    xbuf[SUBLANES:SUBLANES + tl, :] = qkva_ref[...].astype(F32)
    cw = cw_ref[...]
    xs = xbuf[...]
    y = pltpu.roll(xs, CONV_W - 1, 0)[SUBLANES:] * cw[0:1]
    for j in range(1, CONV_W - 1):
        y = y + pltpu.roll(xs, CONV_W - 1 - j, 0)[SUBLANES:] * cw[j:j + 1]
    y = _silu(y + xs[SUBLANES:] * cw[CONV_W - 1:CONV_W])
    xbuf[0:SUBLANES, :] = xbuf[tl:tl + SUBLANES, :]
    qk_parts = _gdn_head_norms(y)

    ri = lax.broadcasted_iota(jnp.int32, (tl, tl), 0)
    ci = lax.broadcasted_iota(jnp.int32, (tl, tl), 1)
    same = (ri // CHUNK) == (ci // CHUNK)
    gb = gb_ref[...]
    sums = _dot(jnp.concatenate([(same & (ri >= ci)).astype(F32), same.astype(F32)], axis=0), gb, HIGHEST)
    gc_all, gl_all = sums[:tl], sums[tl:]
    gc_t = gc_all.T

    rg = lax.broadcasted_iota(jnp.int32, (SOLVE_ROWS, SOLVE_ROWS), 0)
    cg = lax.broadcasted_iota(jnp.int32, (SOLVE_ROWS, SOLVE_ROWS), 1)
    same_g = (rg // CHUNK) == (cg // CHUNK)
    tri = same_g & (rg >= cg)
    strict = same_g & (rg > cg)
    eye = (rg == cg).astype(F32)
    chunks_per_group = SOLVE_ROWS // CHUNK
    groups = [(h, grp) for h in range(H_A) for grp in range(tl // SOLVE_ROWS)]
    ps, tinvs, rhs = [], [], []
    for h, grp in groups:
        lo, hi = h * HEAD_DIM, (h + 1) * HEAD_DIM
        g0, g1 = grp * SOLVE_ROWS, (grp + 1) * SOLVE_ROWS
        q, k = qk_parts[h][g0:g1], qk_parts[H_A + h][g0:g1]
        v = y[g0:g1, 2 * H_A * HEAD_DIM + lo:2 * H_A * HEAD_DIM + hi]
        beta = gb[g0:g1, H_A + h:H_A + h + 1]
        gcol = gc_all[g0:g1, h:h + 1]
        glcol = gl_all[g0:g1, h:h + 1]
        decay = jnp.where(tri, jnp.exp(gcol - gc_t[h:h + 1, g0:g1]), 0.0)
        egc = jnp.exp(gcol)
        kbeta = k * beta
        qkk = _dot_nt(jnp.concatenate([q, kbeta], axis=0).astype(BF16), k.astype(BF16))
        qk = qkk[:SOLVE_ROWS] * decay
        p = -jnp.where(strict, qkk[SOLVE_ROWS:] * decay, 0.0)
        ps.append(p)
        tinvs.append(eye + p)
        rhs.append(jnp.concatenate([v * beta, kbeta * egc], axis=1).astype(BF16))
        qd = (q * egc).astype(BF16)
        kd = k * jnp.exp(glcol - gcol)
        egl = jnp.exp(glcol)
        for cc in range(chunks_per_group):
            c = grp * chunks_per_group + cc
            r0, r1 = cc * CHUNK, (cc + 1) * CHUNK
            wq_s[wr, h, c, CHUNK:2 * CHUNK, :] = qd[r0:r1]
            kdt_s[wr, h, c] = kd[r0:r1].T.astype(BF16)
            qk_s[wr, h, c] = qk[r0:r1, r0:r1].astype(BF16)
            egl_s[wr, h, c] = jnp.broadcast_to(egl[r0:r0 + 1, :], (SUBLANES, HEAD_DIM))

    gnw = gnw_ref[...]

    state, held = {}, {}

    def recurrence_first(c, h):
        s = state.get(h)
        if s is None:
            s = jnp.where(b_starts_seq, 0.0, s_s[h])
        held[h] = (s, _dot(wq_s[rd, h, c], s.astype(BF16)))

    def recurrence_second(c, h):
        r0, r1 = c * CHUNK, (c + 1) * CHUNK
        lo, hi = h * HEAD_DIM, (h + 1) * HEAD_DIM
        s, ws = held.pop(h)
        v_new = u_s[rd, r0:r1, lo:hi] - ws[:CHUNK]
        vnb = v_new.astype(BF16)
        o = ws[CHUNK:] + _dot(qk_s[rd, h, c], vnb)
        state[h] = s * egl_s[rd, h, c][0:1] + _dot(kdt_s[rd, h, c], vnb)
        o_ref[r0:r1, lo:hi] = _gdn_out_norm(o, gnw, z_ref[r0:r1, lo:hi].astype(F32)).astype(o_ref.dtype)
        if c == n_c - 1:
            s_s[h] = state[h]

    side_work = []
    for c in range(n_c):
        for piece in (recurrence_first, recurrence_second):
            for h in range(H_A):
                side_work.append(functools.partial(piece, c, h))
                side_work.append(functools.partial(reduce_cache_blocks, 1))
    n_iter = 5
    n_slots = 2 * n_iter * len(groups)
    done = [0]

    def side_slot(k):
        target = (k + 1) * len(side_work) // n_slots
        while done[0] < target:
            side_work[done[0]]()
            done[0] += 1

    slot = 0
    for it in range(n_iter):
        for i in range(len(groups)):
            pb = ps[i].astype(BF16)
            ps[i] = _dot(pb, pb)
            side_slot(slot)
            slot += 1
        for i in range(len(groups)):
            tinvs[i] = tinvs[i] + _dot(tinvs[i].astype(BF16), ps[i].astype(BF16))
            side_slot(slot)
            slot += 1
    for i, (h, grp) in enumerate(groups):
        lo, hi = h * HEAD_DIM, (h + 1) * HEAD_DIM
        g0, g1 = grp * SOLVE_ROWS, (grp + 1) * SOLVE_ROWS
        uw = _dot(tinvs[i].astype(BF16), rhs[i])
        u_s[wr, g0:g1, lo:hi] = uw[:, :HEAD_DIM]
        w = uw[:, HEAD_DIM:].astype(BF16)
        for cc in range(chunks_per_group):
            c = grp * chunks_per_group + cc
            wq_s[wr, h, c, 0:CHUNK, :] = w[cc * CHUNK:(cc + 1) * CHUNK]
    reduce_cache_blocks(len(pending_blocks))
    sfin_ref[0] = s_s[...]


def _gdn_prompt(qkva, gb, z, cw, gnw, cache_k, page_table, batch, seq, tl):
    n_t = seq // tl
    n_c = tl // CHUNK
    n_tiles = batch * n_t
    dec_batch, pages_per_row = page_table.shape
    n_pages = dec_batch * pages_per_row // n_tiles
    steps_per_row = pages_per_row // n_pages
    assert n_pages * n_tiles == dec_batch * pages_per_row and steps_per_row * n_pages == pages_per_row
    blocks_per_step = n_pages // PAGES_PER_BLOCK
    width = H_B * HEAD_DIM
    cache_pairs = cache_k.reshape(1, cache_k.shape[1], PAGE_SIZE // 2, 2 * H_B, HEAD_DIM)
    tile_a = lambda s: jnp.minimum(s, n_tiles - 1)
    tile_b = lambda s: jnp.maximum(s - 1, 0)
    row_a = lambda s, pt: (tile_a(s), 0)
    row_b = lambda s, pt: (tile_b(s), 0)
    const = lambda shape: pl.BlockSpec(shape, lambda s, pt: (0,) * len(shape), pipeline_mode=pl.Buffered(1))

    def psum_index(s, pt):
        a = tile_a(s)
        return (a // steps_per_row, a % steps_per_row, 0)

    return pl.pallas_call(
        functools.partial(_gdn_prompt_kernel, tl=tl, n_t=n_t, n_tiles=n_tiles, n_pages=n_pages,
                          steps_per_row=steps_per_row),
        grid_spec=pltpu.PrefetchScalarGridSpec(
            num_scalar_prefetch=1, grid=(n_tiles + 1,),
            in_specs=[pl.BlockSpec((tl, GDN_QKV), row_a), pl.BlockSpec((tl, AB_PAD), row_a),
                      pl.BlockSpec((tl, Z_COLS), row_b), const((CONV_W, GDN_QKV)), const((1, HEAD_DIM)),
                      pl.BlockSpec(memory_space=pl.ANY)],
            out_specs=[pl.BlockSpec((tl, Z_COLS), row_b),
                       pl.BlockSpec((1, H_A, HEAD_DIM, HEAD_DIM), lambda s, pt: (tile_b(s) // n_t, 0, 0, 0)),
                       pl.BlockSpec((1, blocks_per_step, width), psum_index)],
            scratch_shapes=[pltpu.VMEM((tl + SUBLANES, GDN_QKV), F32),
                            pltpu.VMEM((2, tl, Z_COLS), F32),
                            pltpu.VMEM((2, H_A, n_c, 2 * CHUNK, HEAD_DIM), BF16),
                            pltpu.VMEM((2, H_A, n_c, HEAD_DIM, CHUNK), BF16),
                            pltpu.VMEM((2, H_A, n_c, CHUNK, CHUNK), BF16),
                            pltpu.VMEM((2, H_A, n_c, SUBLANES, HEAD_DIM), F32),
                            pltpu.VMEM((H_A, HEAD_DIM, HEAD_DIM), F32),
                            pltpu.VMEM((2, n_pages, PAGE_SIZE // 2, 2 * H_B, HEAD_DIM), F32),
                            pltpu.SemaphoreType.DMA((2,))]),
        out_shape=[jax.ShapeDtypeStruct((batch * seq, Z_COLS), BF16),
                   jax.ShapeDtypeStruct((batch, H_A, HEAD_DIM, HEAD_DIM), F32),
                   jax.ShapeDtypeStruct((dec_batch, pages_per_row // PAGES_PER_BLOCK, width), F32)],
        compiler_params=_params(("arbitrary",)),
        name="gdn_prompt",
    )(page_table, qkva, gb, z, cw, gnw, cache_pairs)


def _row0(x, rows=SUBLANES):
    r = lax.broadcasted_iota(jnp.int32, (rows, x.shape[1]), 0)
    return jnp.where(r == 0, jnp.broadcast_to(x, (rows, x.shape[1])), 0.0)


GDN_SAMPLE_ROWS = SUBLANES


def _gdn_sample_kernel(xn_ref, p0_ref, p1_ref, p2_ref, gb_ref, z_ref, cw_ref, gnw_ref, s0_ref,
                       o_ref, sout_ref):
    r0 = pl.multiple_of(pl.program_id(0) * GDN_SAMPLE_ROWS, GDN_SAMPLE_ROWS)
    rows = lambda r: r[pl.ds(r0, GDN_SAMPLE_ROWS), :]
    cw = cw_ref[...]
    y = rows(p0_ref) * cw[0:1]
    y = y + rows(p1_ref) * cw[1:2]
    y = y + rows(p2_ref) * cw[2:3]
    y = y + rows(xn_ref) * cw[3:4]
    y = _silu(y)
    qk = _gdn_head_norms(y)
    gbr = rows(gb_ref)
    zr = rows(z_ref)
    gnw = gnw_ref[...]
    ri = lax.broadcasted_iota(jnp.int32, (HEAD_DIM, HEAD_DIM), 0)
    ci = lax.broadcasted_iota(jnp.int32, (HEAD_DIM, HEAD_DIM), 1)
    eye = (ri == ci).astype(BF16)
    pad = 2 * SUBLANES
    group = 2
    for j0 in range(0, GDN_SAMPLE_ROWS, group):
        units = [(j, h) for j in range(j0, j0 + group) for h in range(H_A)]
        kps = [_row0(qk[H_A + h][j:j + 1], pad).astype(BF16) for j, h in units]
        ks0 = [_dot(kp, s0_ref[j, h].astype(BF16))[0:1] for kp, (j, h) in zip(kps, units)]
        k_cols = [_dot_nt(eye, kp)[:, 0:1] for kp in kps]
        s_news = []
        for (j, h), ks, k_col in zip(units, ks0, k_cols):
            lo, hi = h * HEAD_DIM, (h + 1) * HEAD_DIM
            v = y[j:j + 1, 2 * H_A * HEAD_DIM + lo:2 * H_A * HEAD_DIM + hi]
            eg = jnp.exp(gbr[j:j + 1, h:h + 1])
            beta = gbr[j:j + 1, H_A + h:H_A + h + 1]
            v_new = beta * (v - eg * ks)
            s_new = s0_ref[j, h] * eg + k_col * v_new
            sout_ref[j, h] = s_new
            s_news.append(s_new.astype(BF16))
        outs = [_dot(_row0(qk[h][j:j + 1], pad).astype(BF16), s_new)[0:1]
                for (j, h), s_new in zip(units, s_news)]
        for (j, h), o in zip(units, outs):
            lo, hi = h * HEAD_DIM, (h + 1) * HEAD_DIM
            o_ref[j, :, lo:hi] = _gdn_out_norm(o, gnw, zr[j:j + 1, lo:hi])


def _gdn_sample(xn, p0, p1, p2, gb, z, cw, gnw, s0):
    nb = xn.shape[0]
    full = lambda a: _const_spec(a.shape)
    st = pl.BlockSpec((GDN_SAMPLE_ROWS, H_A, HEAD_DIM, HEAD_DIM), lambda b: (b, 0, 0, 0))
    return pl.pallas_call(
        _gdn_sample_kernel,
        grid=(nb // GDN_SAMPLE_ROWS,),
        in_specs=[full(xn), full(p0), full(p1), full(p2), full(gb), full(z), full(cw), full(gnw), st],
        out_specs=[pl.BlockSpec((GDN_SAMPLE_ROWS, 1, Z_COLS), lambda b: (b, 0, 0)), st],
        out_shape=[jax.ShapeDtypeStruct((nb, 1, Z_COLS), F32),
                   jax.ShapeDtypeStruct(s0.shape, F32)],
        compiler_params=_params(("parallel",)),
        name="gdn_sample",
    )(xn, p0, p1, p2, gb, z, cw, gnw, s0)


def _moba_prompt_kernel(q_ref, k_ref, v_ref, o_ref, *, seq):
    nb = seq // MOBA_BLOCK
    k = k_ref[...]
    kb = k.astype(BF16)
    v_t = jnp.concatenate([v_ref[...].T, jnp.ones((SUBLANES, seq), F32)], axis=0).astype(BF16)
    km = jnp.concatenate(
        [jnp.sum(k[j * MOBA_BLOCK:(j + 1) * MOBA_BLOCK], axis=0, keepdims=True) * (1.0 / MOBA_BLOCK)
         for j in range(nb)], axis=0)
    blk = lax.broadcasted_iota(jnp.int32, (nb, MOBA_BLOCK), 0)
    ki = lax.broadcasted_iota(jnp.int32, (MOBA_BLOCK, MOBA_BLOCK), 0)
    qi = lax.broadcasted_iota(jnp.int32, (MOBA_BLOCK, MOBA_BLOCK), 1)
    causal_bias = jnp.where(ki <= qi, 0.0, NEG_INF)
    scale = HEAD_DIM ** -0.5 * LOG2_E

    def logits(n):
        q = q_ref[n * MOBA_BLOCK:(n + 1) * MOBA_BLOCK, :]
        gate = _dot_nt(km, q, HIGHEST)
        qs = (q * scale).astype(BF16)
        cnt = jnp.zeros((nb, MOBA_BLOCK), jnp.int32)
        for i in range(n):
            gi = gate[i:i + 1, :]
            beats = (gi > gate) | ((gi == gate) & (i < blk))
            cnt = cnt + beats.astype(jnp.int32)
        sel_bias = jnp.where((cnt < MOBA_TOPK) & (blk < n), 0.0, NEG_INF)
        tiles = [_dot_nt(kb[j * MOBA_BLOCK:(j + 1) * MOBA_BLOCK], qs) + sel_bias[j:j + 1, :] for j in range(n)]
        tiles.append(_dot_nt(kb[n * MOBA_BLOCK:(n + 1) * MOBA_BLOCK], qs) + causal_bias)
        return tiles

    def probabilities(tiles):
        m = functools.reduce(jnp.maximum, [jnp.max(t, axis=0, keepdims=True) for t in tiles])
        return jnp.concatenate([jnp.exp2(t - m).astype(BF16) for t in tiles], axis=0)

    def weighted_values(n, p):
        pv = _dot(v_t[:, :(n + 1) * MOBA_BLOCK], p)
        o_t = pv[:HEAD_DIM] / pv[HEAD_DIM:HEAD_DIM + 1]
        o_ref[n * MOBA_BLOCK:(n + 1) * MOBA_BLOCK, :] = o_t.T.astype(o_ref.dtype)

    ahead = 2
    pending = {n: logits(n) for n in range(min(ahead, nb))}
    for n in range(nb):
        p = probabilities(pending.pop(n))
        if n + ahead < nb:
            pending[n + ahead] = logits(n + ahead)
        weighted_values(n, p)


def _moba_prompt(qb, kb, vb, batch, seq):
    spec = pl.BlockSpec((seq, HEAD_DIM), lambda b, h: (b, h))
    return pl.pallas_call(
        functools.partial(_moba_prompt_kernel, seq=seq),
        grid=(batch, H_B),
        in_specs=[spec, spec, spec],
        out_specs=spec,
        out_shape=jax.ShapeDtypeStruct((batch * seq, H_B * HEAD_DIM), BF16),
        compiler_params=_params(("parallel", "parallel")),
        name="moba_prompt",
    )(qb, kb, vb)


SELECT_ROWS = SUBLANES


def _moba_select_kernel(q_ref, ps_ref, idx_ref):
    r0 = pl.multiple_of(pl.program_id(0) * SELECT_ROWS, SELECT_ROWS)
    qs = q_ref[pl.ds(r0, SELECT_ROWS), :]
    n_blocks = ps_ref.shape[1]
    row = lax.broadcasted_iota(jnp.int32, (SUBLANES, LANES), 0)
    lane = lax.broadcasted_iota(jnp.int32, (SUBLANES, LANES), 1)
    blk = lax.broadcasted_iota(jnp.int32, (1, n_blocks), 1).astype(F32)
    for j in range(SELECT_ROWS):
        km = ps_ref[j] * (1.0 / MOBA_BLOCK)
        out = jnp.zeros((SUBLANES, LANES), F32)
        for h in range(H_B):
            lo, hi = h * HEAD_DIM, (h + 1) * HEAD_DIM
            g = _dot_nt(_row0(qs[j:j + 1, lo:hi]), km[:, lo:hi], HIGHEST)[0:1]
            for s in range(MOBA_TOPK):
                m = jnp.max(g, axis=1, keepdims=True)
                idx = jnp.min(jnp.where(g == m, blk, float(n_blocks)), axis=1, keepdims=True)
                out = jnp.where((row == h) & (lane == s), idx, out)
                g = jnp.where(blk == idx, NEG_INF, g)
        idx_ref[j] = out.astype(jnp.int32)


def _moba_select(qb, psums):
    nb, n_blocks, width = psums.shape
    return pl.pallas_call(
        _moba_select_kernel,
        grid=(nb // SELECT_ROWS,),
        in_specs=[_const_spec(qb.shape), pl.BlockSpec((SELECT_ROWS, n_blocks, width), lambda b: (b, 0, 0))],
        out_specs=pl.BlockSpec((SELECT_ROWS, SUBLANES, LANES), lambda b: (b, 0, 0)),
        out_shape=jax.ShapeDtypeStruct((nb, SUBLANES, LANES), jnp.int32),
        compiler_params=_params(("parallel",)),
        name="moba_select",
    )(qb, psums)


N_SEL_PAGES = MOBA_TOPK * PAGES_PER_BLOCK


def _moba_attend_kernel(idx_ref, pt_ref, q_ref, k_ref, v_ref, ck_ref, cv_ref, o_ref, kbuf, vbuf, sem):
    b = pl.program_id(0)
    n_rows = pl.num_programs(0)

    def copies(row, half):
        out = []
        for h in range(H_B):
            for s in range(MOBA_TOPK):
                blk = idx_ref[(row * H_B + h) * MOBA_TOPK + s]
                for p in range(PAGES_PER_BLOCK):
                    page = pt_ref[row, blk * PAGES_PER_BLOCK + p]
                    j = (h * MOBA_TOPK + s) * PAGES_PER_BLOCK + p
                    out.append(pltpu.make_async_copy(ck_ref.at[0, page, :, h, :], kbuf.at[half, j], sem.at[half]))
                    out.append(pltpu.make_async_copy(cv_ref.at[0, page, :, h, :], vbuf.at[half, j], sem.at[half]))
        return out

    @pl.when(b == 0)
    def _():
        for c in copies(0, 0):
            c.start()

    scale = HEAD_DIM ** -0.5
    for parity in range(2):
        @pl.when(b % 2 == parity)
        def _():
            for c in copies(jnp.minimum(b + 1, n_rows - 1), 1 - parity):
                c.start()
            for c in copies(0, parity):
                c.wait()
            q = q_ref[pl.ds(b, 1), :]
            k_new = k_ref[pl.ds(b, 1), :]
            v_new = v_ref[pl.ds(b, 1), :]
            heads = range(H_B)
            span = lambda h: slice(h * HEAD_DIM, (h + 1) * HEAD_DIM)
            pages = lambda buf, h: buf[parity, h * N_SEL_PAGES:(h + 1) * N_SEL_PAGES].reshape(
                N_SEL_PAGES * PAGE_SIZE, HEAD_DIM)
            logits = [_dot_nt(_row0(q[:, span(h)], 2 * SUBLANES).astype(BF16),
                              pages(kbuf, h).astype(BF16))[0:1] * scale for h in heads]
            probs = []
            for h in heads:
                s = logits[h]
                s_new = jnp.sum(q[:, span(h)] * k_new[:, span(h)], axis=1, keepdims=True) * scale
                m = jnp.maximum(jnp.max(s, axis=1, keepdims=True), s_new)
                p = jnp.exp(s - m)
                p_new = jnp.exp(s_new - m)
                probs.append((p, p_new, jnp.sum(p, axis=1, keepdims=True) + p_new))
            pvs = [_dot(_row0(probs[h][0], 2 * SUBLANES).astype(BF16), pages(vbuf, h).astype(BF16))[0:1]
                   for h in heads]
            for h in heads:
                _, p_new, l = probs[h]
                o_ref[0, :, span(h)] = (pvs[h] + p_new * v_new[:, span(h)]) / l

    @pl.when(b == n_rows - 1)
    def _():
        for c in copies(0, n_rows % 2):
            c.wait()


def _moba_attend(idx_flat, page_table, cache_k, cache_v, qb, kb, vb):
    nb = qb.shape[0]
    n_slices = H_B * N_SEL_PAGES
    rows = pl.BlockSpec((nb, H_B * HEAD_DIM), lambda b, idx, pt: (0, 0))
    hbm = pl.BlockSpec(memory_space=pl.ANY)
    out = pl.pallas_call(
        _moba_attend_kernel,
        grid_spec=pltpu.PrefetchScalarGridSpec(
            num_scalar_prefetch=2, grid=(nb,),
            in_specs=[rows, rows, rows, hbm, hbm],
            out_specs=pl.BlockSpec((1, 1, H_B * HEAD_DIM), lambda b, idx, pt: (b, 0, 0)),
            scratch_shapes=[pltpu.VMEM((2, n_slices, PAGE_SIZE, HEAD_DIM), F32),
                            pltpu.VMEM((2, n_slices, PAGE_SIZE, HEAD_DIM), F32),
                            pltpu.SemaphoreType.DMA((2,))]),
        out_shape=jax.ShapeDtypeStruct((nb, 1, H_B * HEAD_DIM), F32),
        compiler_params=_params(("arbitrary",)),
        name="moba_attend",
    )(idx_flat, page_table, qb, kb, vb, cache_k, cache_v)
    return out.reshape(nb, H_B * HEAD_DIM)


def _out_ffn_kernel(x_ref, oa_ref, ob_ref, woa_ref, wob_ref, n2_ref, wg_ref, wu_ref, wd_ref, fn_ref, y_ref):
    tm = x_ref.shape[0]
    n_groups = 2 if tm % (2 * SUBLANES) == 0 and tm >= 256 else 1
    rows = [slice(g * tm // n_groups, (g + 1) * tm // n_groups) for g in range(n_groups)]
    x1 = [x_ref[r, :] + _dot(oa_ref[r, :].astype(BF16), woa_ref[...])
          + _dot(ob_ref[r, :].astype(BF16), wob_ref[...]) for r in rows]
    h = [_rmsnorm(t, n2_ref[...]).astype(BF16) for t in x1]
    gate = [_dot(t, wg_ref[...]) for t in h]
    up = [_dot(t, wu_ref[...]) for t in h]
    act = [(_silu(g) * u).astype(BF16) for g, u in zip(gate, up)]
    down = [_dot(t, wd_ref[...]) for t in act]
    for r, a, d in zip(rows, x1, down):
        y_ref[r, :] = _rmsnorm(a + d, fn_ref[...])


def _out_ffn(x, oa, ob, woa, wob, n2, wg, wu, wd, fn, tm):
    t = x.shape[0]
    row = lambda i: (i, 0)
    full = lambda a: _const_spec(a.shape)
    return pl.pallas_call(
        _out_ffn_kernel,
        grid=(t // tm,),
        in_specs=[pl.BlockSpec((tm, D_MODEL), row), pl.BlockSpec((tm, Z_COLS), row),
                  pl.BlockSpec((tm, H_B * HEAD_DIM), row), full(woa), full(wob), full(n2),
                  full(wg), full(wu), full(wd), full(fn)],
        out_specs=pl.BlockSpec((tm, D_MODEL), row),
        out_shape=jax.ShapeDtypeStruct((t, D_MODEL), F32),
        compiler_params=_params(("parallel",)),
        name="out_ffn",
    )(x, oa, ob, woa, wob, n2, wg, wu, wd, fn)


def _rotary_tables(pos):
    pos = np.asarray(pos, np.float64)
    half = ROT_DIM // 2
    inv_freq = ROPE_THETA ** (-np.arange(half, dtype=np.float64) / half)
    ang = pos[:, None] * inv_freq[None, :]
    cos, sin = np.cos(ang).astype(np.float32), np.sin(ang).astype(np.float32)
    n = pos.shape[0]
    rest = HEAD_DIM - ROT_DIM
    cos_t = np.concatenate([cos, cos, np.ones((n, rest), np.float32)], axis=1)
    zeros_h = np.zeros((n, half), np.float32)
    sin_a = np.concatenate([-sin, zeros_h, np.zeros((n, rest), np.float32)], axis=1)
    sin_b = np.concatenate([zeros_h, sin, np.zeros((n, rest), np.float32)], axis=1)
    return jnp.asarray(cos_t), jnp.asarray(sin_a), jnp.asarray(sin_b)


def kernel(x_prompt, x_sample, cache_k, cache_v, page_table, state_gdn, state_conv, norm1_w, w_in, conv_w,
           a_log, dt_bias, gdn_norm_w, w_out, norm2_w, w_gate, w_up, w_down, final_norm_w):
    depth = w_in.shape[0]
    assert depth == 1, "single-layer trunk"
    batch, seq, _ = x_prompt.shape
    dec_batch, dec_seq, _ = x_sample.shape
    assert dec_seq == 1
    past_len = page_table.shape[1] * PAGE_SIZE

    wi = w_in[0]
    ab0 = GDN_QKV + Z_COLS
    w_a = wi[:, :ab0].astype(BF16)
    w_b = wi[:, ab0 + 2 * H_A:].astype(BF16)
    w_ab = jnp.pad(wi[:, ab0:ab0 + 2 * H_A], ((0, 0), (0, AB_PAD - 2 * H_A))).astype(BF16)
    gpar = jnp.zeros((SUBLANES, LANES), F32)
    gpar = gpar.at[0, :H_A].set(a_log[0].astype(F32)).at[1, :H_A].set(dt_bias[0].astype(F32))
    nw1 = norm1_w[0].reshape(1, D_MODEL)
    cw = conv_w[0]
    gnw = gdn_norm_w[0].reshape(1, HEAD_DIM)
    woa = w_out[0][:Z_COLS].astype(BF16)
    wob = w_out[0][Z_COLS:].astype(BF16)
    n2 = norm2_w[0].reshape(1, D_MODEL)
    wg, wu, wd = w_gate[0].astype(BF16), w_up[0].astype(BF16), w_down[0].astype(BF16)
    fn = final_norm_w.reshape(1, D_MODEL)

    xp = x_prompt.reshape(batch * seq, D_MODEL)
    tabs_p = _rotary_tables(np.arange(seq))
    qkva_p, z_p, gb_p, qb_p, kb_p, vb_p, k5_p, v5_p = _in_proj(xp, nw1, w_a, w_b, w_ab, gpar, *tabs_p,
                                                               batch=batch, seq=seq, tm=512)
    oa_p, gdn_p, psums = _gdn_prompt(qkva_p, gb_p, z_p, cw, gnw, cache_k, page_table, batch, seq, tl=256)
    ob_p = _moba_prompt(qb_p, kb_p, vb_p, batch, seq)
    y_p = _out_ffn(xp, oa_p, ob_p, woa, wob, n2, wg, wu, wd, fn, tm=512)

    xs = x_sample.reshape(dec_batch, D_MODEL)
    tabs_s = _rotary_tables(np.full((dec_batch,), past_len))
    qkva_s, z_s, gb_s, qb_s, kb_s, vb_s, k5_s, v5_s = _in_proj(xs, nw1, w_a, w_b, w_ab, gpar, *tabs_s,
                                                               batch=1, seq=dec_batch, tm=dec_batch)
    qkva_s, z_s = qkva_s.astype(F32), z_s.astype(F32)
    sc = state_conv[0]
    oa_s, gdn_s = _gdn_sample(qkva_s, sc[:, 0], sc[:, 1], sc[:, 2], gb_s, z_s, cw, gnw, state_gdn[0])
    idx = _moba_select(qb_s, psums)
    idx_flat = idx[:, :H_B, :MOBA_TOPK].reshape(-1)
    ob_s = _moba_attend(idx_flat, page_table, cache_k, cache_v, qb_s, kb_s, vb_s)
    y_s = _out_ffn(xs, oa_s.reshape(dec_batch, Z_COLS), ob_s, woa, wob, n2, wg, wu, wd, fn, tm=dec_batch)

    conv_p = qkva_p.reshape(batch, seq, GDN_QKV)[:, seq - (CONV_W - 1):].astype(F32)
    conv_s = jnp.concatenate([sc[:, 1:], qkva_s[:, None, :]], axis=1)
    return (y_p.reshape(batch, seq, D_MODEL), y_s.reshape(dec_batch, 1, D_MODEL),
            k5_p, v5_p,
            k5_s.reshape(1, dec_batch, 1, H_B, HEAD_DIM), v5_s.reshape(1, dec_batch, 1, H_B, HEAD_DIM),
            gdn_p[None], gdn_s[None], conv_p[None], conv_s[None])
```

```python
import functools

import jax
import jax.numpy as jnp
import numpy as np
from jax import lax
from jax.experimental import pallas as pl
from jax.experimental.pallas import tpu as pltpu

D_MODEL = 1024
HEAD_DIM = 128
H_A = 4
H_B = 4
GDN_QKV = 3 * H_A * HEAD_DIM
MOBA_QKV = 3 * H_B * HEAD_DIM
Z_COLS = H_A * HEAD_DIM
CONV_W = 4
CHUNK = 64
SOLVE_ROWS = 128
MOBA_BLOCK = 256
MOBA_TOPK = 3
PAGE_SIZE = 128
PAGES_PER_BLOCK = MOBA_BLOCK // PAGE_SIZE
ROT_DIM = HEAD_DIM // 4
ROPE_THETA = 500000.0
EPS = 1e-6

LANES = 128
SUBLANES = 8
AB_PAD = LANES
PROJ_COLS = GDN_QKV + Z_COLS + MOBA_QKV + AB_PAD
VMEM_LIMIT = 56 * 1024 * 1024

F32 = jnp.float32
BF16 = jnp.bfloat16
HIGHEST = lax.Precision.HIGHEST
NEG_INF = float("-inf")
LOG2_E = 1.4426950408889634


def _dot(a, b, precision=None):
    return jnp.dot(a, b, precision=precision, preferred_element_type=F32)


def _dot_nt(a, b, precision=None):
    return lax.dot_general(a, b, (((1,), (1,)), ((), ())), precision=precision,
                           preferred_element_type=F32)


def _rmsnorm(x, w):
    return x * lax.rsqrt(jnp.mean(x * x, axis=-1, keepdims=True) + EPS) * w


def _l2norm(x):
    return x * lax.rsqrt(jnp.sum(x * x, axis=-1, keepdims=True) + EPS)


def _silu(x):
    return x * jax.nn.sigmoid(x)


def _const_spec(shape):
    return pl.BlockSpec(shape, lambda *_: (0,) * len(shape), pipeline_mode=pl.Buffered(1))


def _params(semantics):
    return pltpu.CompilerParams(dimension_semantics=semantics, vmem_limit_bytes=VMEM_LIMIT)


def _in_proj_kernel(x_ref, nw_ref, wa_ref, wb_ref, wab_ref, gpar_ref, cos_ref, sa_ref, sb_ref,
                    qkva_ref, z_ref, gb_ref, qb_ref, kb_ref, vb_ref, k5_ref, v5_ref):
    xn = _rmsnorm(x_ref[...], nw_ref[...]).astype(BF16)
    pb = _dot(xn, wb_ref[...])
    pa = _dot(xn, wa_ref[:, :GDN_QKV])
    cos, sa, sb = cos_ref[...], sa_ref[...], sb_ref[...]
    half = ROT_DIM // 2

    def rot(t):
        return (t * cos + pltpu.roll(t, HEAD_DIM - half, 1) * sa + pltpu.roll(t, half, 1) * sb)

    for h in range(H_B):
        lo, hi = h * HEAD_DIM, (h + 1) * HEAD_DIM
        qb_ref[:, lo:hi] = rot(pb[:, lo:hi])
        kh = rot(pb[:, H_B * HEAD_DIM + lo:H_B * HEAD_DIM + hi])
        vh = pb[:, 2 * H_B * HEAD_DIM + lo:2 * H_B * HEAD_DIM + hi]
        kb_ref[:, lo:hi] = kh
        vb_ref[:, lo:hi] = vh
        k5_ref[0, 0, :, h, :] = kh
        v5_ref[0, 0, :, h, :] = vh
    pz = _dot(xn, wa_ref[:, GDN_QKV:])
    ab = _dot(xn, wab_ref[...])
    qkva_ref[...] = pa.astype(BF16)
    z_ref[...] = pz.astype(BF16)
    lane = lax.broadcasted_iota(jnp.int32, ab.shape, 1)
    g = -jnp.exp(gpar_ref[0:1, :]) * jax.nn.softplus(ab + gpar_ref[1:2, :])
    gb_ref[...] = jnp.where(lane < H_A, g, jax.nn.sigmoid(ab))


def _in_proj(x, nw, wa, wb, wab, gpar, cos, sa, sb, batch, seq, tm):
    t = batch * seq
    n_t = seq // tm
    row = lambda i: (i, 0)
    tab = lambda i: (i % n_t, 0)
    kv = lambda i: (0, i // n_t, i % n_t, 0, 0)
    outs = ((GDN_QKV, BF16), (Z_COLS, BF16), (AB_PAD, F32), (H_B * HEAD_DIM, F32),
            (H_B * HEAD_DIM, F32), (H_B * HEAD_DIM, F32))
    kv_shape = jax.ShapeDtypeStruct((1, batch, seq, H_B, HEAD_DIM), F32)
    kv_spec = pl.BlockSpec((1, 1, tm, H_B, HEAD_DIM), kv)
    return pl.pallas_call(
        _in_proj_kernel,
        grid=(t // tm,),
        in_specs=[pl.BlockSpec((tm, D_MODEL), row), _const_spec((1, D_MODEL)),
                  _const_spec(wa.shape), _const_spec(wb.shape), _const_spec(wab.shape),
                  _const_spec((SUBLANES, LANES)),
                  pl.BlockSpec((tm, LANES), tab), pl.BlockSpec((tm, LANES), tab),
                  pl.BlockSpec((tm, LANES), tab)],
        out_specs=[pl.BlockSpec((tm, w), row) for w, _ in outs] + [kv_spec, kv_spec],
        out_shape=[jax.ShapeDtypeStruct((t, w), dt) for w, dt in outs] + [kv_shape, kv_shape],
        compiler_params=_params(("parallel",)),
        name="in_proj",
    )(x, nw, wa, wb, wab, gpar, cos, sa, sb)


def _gdn_head_norms(y):
    parts = []
    for h in range(H_A):
        parts.append(_l2norm(y[:, h * HEAD_DIM:(h + 1) * HEAD_DIM]) * (HEAD_DIM ** -0.5))
    for h in range(H_A, 2 * H_A):
        parts.append(_l2norm(y[:, h * HEAD_DIM:(h + 1) * HEAD_DIM]))
    return parts


def _gdn_out_norm(o, gnw, z):
    return _rmsnorm(o, gnw) * _silu(z)


def _sum_rows(x):
    while x.shape[0] > 1:
        half = x.shape[0] // 2
        x = x[:half] + x[half:]
    return x[0]


def _block_key_sum(pages_ref, r, out_ref):
    i = r * PAGES_PER_BLOCK
    s = _sum_rows(pages_ref[i])
    for j in range(1, PAGES_PER_BLOCK):
        s = s + _sum_rows(pages_ref[i + j])
    s = s[:H_B] + s[H_B:]
    for h in range(H_B):
        out_ref[0, r:r + 1, h * HEAD_DIM:(h + 1) * HEAD_DIM] = s[h:h + 1, :]


def _gdn_prompt_kernel(pt_ref, qkva_ref, gb_ref, z_ref, cw_ref, gnw_ref, cache_ref, *refs,
                       tl, n_t, n_tiles, n_pages, steps_per_row):
    step = pl.program_id(0)
    o_ref, sfin_ref, psum_ref, xbuf, u_s, wq_s, kdt_s, qk_s, egl_s, s_s, pbuf, psem = refs

    def page_copy(tile, i, half):
        page = pt_ref[tile // steps_per_row, (tile % steps_per_row) * n_pages + i]
        return pltpu.make_async_copy(cache_ref.at[0, page], pbuf.at[half, i], psem.at[half])

    @pl.when(step == 0)
    def _():
        xbuf[0:SUBLANES, :] = jnp.zeros((SUBLANES, GDN_QKV), F32)
        u_s[...] = jnp.zeros_like(u_s)
        wq_s[...] = jnp.zeros_like(wq_s)
        kdt_s[...] = jnp.zeros_like(kdt_s)
        qk_s[...] = jnp.zeros_like(qk_s)
        egl_s[...] = jnp.zeros_like(egl_s)
        s_s[...] = jnp.zeros_like(s_s)
        for i in range(n_pages):
            page_copy(0, i, 0).start()

    for parity in range(2):
        @pl.when(step % 2 == parity)
        def _():
            nxt = jnp.minimum(step + 1, n_tiles - 1)
            for i in range(n_pages):
                page_copy(nxt, i, 1 - parity).start()

            def wait_pages():
                for i in range(n_pages):
                    page_copy(0, i, parity).wait()

            _gdn_prompt_step(qkva_ref, gb_ref, z_ref, cw_ref, gnw_ref, o_ref, sfin_ref, psum_ref,
                             xbuf, u_s, wq_s, kdt_s, qk_s, egl_s, s_s, pbuf.at[parity], wait_pages,
                             tl=tl, n_t=n_t, n_pages=n_pages, wr=parity)

    @pl.when(step == n_tiles)
    def _():
        for i in range(n_pages):
            page_copy(0, i, (n_tiles + 1) % 2).wait()


def _gdn_prompt_step(qkva_ref, gb_ref, z_ref, cw_ref, gnw_ref, o_ref, sfin_ref, psum_ref,
                     xbuf, u_s, wq_s, kdt_s, qk_s, egl_s, s_s, pages, wait_pages, *, tl, n_t, n_pages, wr):
    wait_pages()
    step = pl.program_id(0)
    rd = 1 - wr
    a_starts_seq = (step % n_t) == 0
    b_starts_seq = ((step + n_t - 1) % n_t) == 0
    n_c = tl // CHUNK
    pending_blocks = list(range(n_pages // PAGES_PER_BLOCK))

    def reduce_cache_blocks(count):
        for _ in range(min(count, len(pending_blocks))):
            _block_key_sum(pages, pending_blocks.pop(0), psum_ref)

    tail = xbuf[0:SUBLANES, :]
    xbuf[0:SUBLANES, :] = jnp.where(a_starts_seq, 0.0, tail)
    xbuf[SUBLANES:SUBLANES + tl, :] = qkva_ref[...].astype(F32)
    cw = cw_ref[...]
    xs = xbuf[...]
    y = pltpu.roll(xs, CONV_W - 1, 0)[SUBLANES:] * cw[0:1]
    for j in range(1, CONV_W - 1):
        y = y + pltpu.roll(xs, CONV_W - 1 - j, 0)[SUBLANES:] * cw[j:j + 1]
    y = _silu(y + xs[SUBLANES:] * cw[CONV_W - 1:CONV_W])
    xbuf[0:SUBLANES, :] = xbuf[tl:tl + SUBLANES, :]
    qk_parts = _gdn_head_norms(y)

    ri = lax.broadcasted_iota(jnp.int32, (tl, tl), 0)
    ci = lax.broadcasted_iota(jnp.int32, (tl, tl), 1)
    same = (ri // CHUNK) == (ci // CHUNK)
    gb = gb_ref[...]
    sums = _dot(jnp.concatenate([(same & (ri >= ci)).astype(F32), same.astype(F32)], axis=0), gb, HIGHEST)
    gc_all, gl_all = sums[:tl], sums[tl:]
    gc_t = gc_all.T

    rg = lax.broadcasted_iota(jnp.int32, (SOLVE_ROWS, SOLVE_ROWS), 0)
    cg = lax.broadcasted_iota(jnp.int32, (SOLVE_ROWS, SOLVE_ROWS), 1)
    same_g = (rg // CHUNK) == (cg // CHUNK)
    tri = same_g & (rg >= cg)
    strict = same_g & (rg > cg)
    eye = (rg == cg).astype(F32)
    chunks_per_group = SOLVE_ROWS // CHUNK
    groups = [(h, grp) for h in range(H_A) for grp in range(tl // SOLVE_ROWS)]
    ps, tinvs, rhs = [], [], []
    for h, grp in groups:
        lo, hi = h * HEAD_DIM, (h + 1) * HEAD_DIM
        g0, g1 = grp * SOLVE_ROWS, (grp + 1) * SOLVE_ROWS
        q, k = qk_parts[h][g0:g1], qk_parts[H_A + h][g0:g1]
        v = y[g0:g1, 2 * H_A * HEAD_DIM + lo:2 * H_A * HEAD_DIM + hi]
        beta = gb[g0:g1, H_A + h:H_A + h + 1]
        gcol = gc_all[g0:g1, h:h + 1]
        glcol = gl_all[g0:g1, h:h + 1]
        decay = jnp.where(tri, jnp.exp(gcol - gc_t[h:h + 1, g0:g1]), 0.0)
        egc = jnp.exp(gcol)
        kbeta = k * beta
        qkk = _dot_nt(jnp.concatenate([q, kbeta], axis=0).astype(BF16), k.astype(BF16))
        qk = qkk[:SOLVE_ROWS] * decay
        p = -jnp.where(strict, qkk[SOLVE_ROWS:] * decay, 0.0)
        ps.append(p)
        tinvs.append(eye + p)
        rhs.append(jnp.concatenate([v * beta, kbeta * egc], axis=1).astype(BF16))
        qd = (q * egc).astype(BF16)
        kd = k * jnp.exp(glcol - gcol)
        egl = jnp.exp(glcol)
        for cc in range(chunks_per_group):
            c = grp * chunks_per_group + cc
            r0, r1 = cc * CHUNK, (cc + 1) * CHUNK
            wq_s[wr, h, c, CHUNK:2 * CHUNK, :] = qd[r0:r1]
            kdt_s[wr, h, c] = kd[r0:r1].T.astype(BF16)
            qk_s[wr, h, c] = qk[r0:r1, r0:r1].astype(BF16)
            egl_s[wr, h, c] = jnp.broadcast_to(egl[r0:r0 + 1, :], (SUBLANES, HEAD_DIM))

    gnw = gnw_ref[...]

    state, held = {}, {}

    def recurrence_first(c, h):
        s = state.get(h)
        if s is None:
            s = jnp.where(b_starts_seq, 0.0, s_s[h])
        held[h] = (s, _dot(wq_s[rd, h, c], s.astype(BF16)))

    def recurrence_second(c, h):
        r0, r1 = c * CHUNK, (c + 1) * CHUNK
        lo, hi = h * HEAD_DIM, (h + 1) * HEAD_DIM
        s, ws = held.pop(h)
        v_new = u_s[rd, r0:r1, lo:hi] - ws[:CHUNK]
        vnb = v_new.astype(BF16)
        o = ws[CHUNK:] + _dot(qk_s[rd, h, c], vnb)
        state[h] = s * egl_s[rd, h, c][0:1] + _dot(kdt_s[rd, h, c], vnb)
        o_ref[r0:r1, lo:hi] = _gdn_out_norm(o, gnw, z_ref[r0:r1, lo:hi].astype(F32)).astype(o_ref.dtype)
        if c == n_c - 1:
            s_s[h] = state[h]

    side_work = []
    for c in range(n_c):
        for piece in (recurrence_first, recurrence_second):
            for h in range(H_A):
                side_work.append(functools.partial(piece, c, h))
                side_work.append(functools.partial(reduce_cache_blocks, 1))
    n_iter = 5
    n_slots = 2 * n_iter * len(groups)
    done = [0]

    def side_slot(k):
        target = (k + 1) * len(side_work) // n_slots
        while done[0] < target:
            side_work[done[0]]()
            done[0] += 1

    slot = 0
    for it in range(n_iter):
        for i in range(len(groups)):
            pb = ps[i].astype(BF16)
            ps[i] = _dot(pb, pb)
            side_slot(slot)
            slot += 1
        for i in range(len(groups)):
            tinvs[i] = tinvs[i] + _dot(tinvs[i].astype(BF16), ps[i].astype(BF16))
            side_slot(slot)
            slot += 1
    for i, (h, grp) in enumerate(groups):
        lo, hi = h * HEAD_DIM, (h + 1) * HEAD_DIM
        g0, g1 = grp * SOLVE_ROWS, (grp + 1) * SOLVE_ROWS
        uw = _dot(tinvs[i].astype(BF16), rhs[i])
        u_s[wr, g0:g1, lo:hi] = uw[:, :HEAD_DIM]
        w = uw[:, HEAD_DIM:].astype(BF16)
        for cc in range(chunks_per_group):
            c = grp * chunks_per_group + cc
            wq_s[wr, h, c, 0:CHUNK, :] = w[cc * CHUNK:(cc + 1) * CHUNK]
    reduce_cache_blocks(len(pending_blocks))
    sfin_ref[0] = s_s[...]


def _gdn_prompt(qkva, gb, z, cw, gnw, cache_k, page_table, batch, seq, tl):
    n_t = seq // tl
    n_c = tl // CHUNK
    n_tiles = batch * n_t
    dec_batch, pages_per_row = page_table.shape
    n_pages = dec_batch * pages_per_row // n_tiles
    steps_per_row = pages_per_row // n_pages
    assert n_pages * n_tiles == dec_batch * pages_per_row and steps_per_row * n_pages == pages_per_row
    blocks_per_step = n_pages // PAGES_PER_BLOCK
    width = H_B * HEAD_DIM
    cache_pairs = cache_k.reshape(1, cache_k.shape[1], PAGE_SIZE // 2, 2 * H_B, HEAD_DIM)
    tile_a = lambda s: jnp.minimum(s, n_tiles - 1)
    tile_b = lambda s: jnp.maximum(s - 1, 0)
    row_a = lambda s, pt: (tile_a(s), 0)
    row_b = lambda s, pt: (tile_b(s), 0)
    const = lambda shape: pl.BlockSpec(shape, lambda s, pt: (0,) * len(shape), pipeline_mode=pl.Buffered(1))

    def psum_index(s, pt):
        a = tile_a(s)
        return (a // steps_per_row, a % steps_per_row, 0)

    return pl.pallas_call(
        functools.partial(_gdn_prompt_kernel, tl=tl, n_t=n_t, n_tiles=n_tiles, n_pages=n_pages,
                          steps_per_row=steps_per_row),
        grid_spec=pltpu.PrefetchScalarGridSpec(
            num_scalar_prefetch=1, grid=(n_tiles + 1,),
            in_specs=[pl.BlockSpec((tl, GDN_QKV), row_a), pl.BlockSpec((tl, AB_PAD), row_a),
                      pl.BlockSpec((tl, Z_COLS), row_b), const((CONV_W, GDN_QKV)), const((1, HEAD_DIM)),
                      pl.BlockSpec(memory_space=pl.ANY)],
            out_specs=[pl.BlockSpec((tl, Z_COLS), row_b),
                       pl.BlockSpec((1, H_A, HEAD_DIM, HEAD_DIM), lambda s, pt: (tile_b(s) // n_t, 0, 0, 0)),
                       pl.BlockSpec((1, blocks_per_step, width), psum_index)],
            scratch_shapes=[pltpu.VMEM((tl + SUBLANES, GDN_QKV), F32),
                            pltpu.VMEM((2, tl, Z_COLS), F32),
                            pltpu.VMEM((2, H_A, n_c, 2 * CHUNK, HEAD_DIM), BF16),
                            pltpu.VMEM((2, H_A, n_c, HEAD_DIM, CHUNK), BF16),
                            pltpu.VMEM((2, H_A, n_c, CHUNK, CHUNK), BF16),
                            pltpu.VMEM((2, H_A, n_c, SUBLANES, HEAD_DIM), F32),
                            pltpu.VMEM((H_A, HEAD_DIM, HEAD_DIM), F32),
                            pltpu.VMEM((2, n_pages, PAGE_SIZE // 2, 2 * H_B, HEAD_DIM), F32),
                            pltpu.SemaphoreType.DMA((2,))]),
        out_shape=[jax.ShapeDtypeStruct((batch * seq, Z_COLS), BF16),
                   jax.ShapeDtypeStruct((batch, H_A, HEAD_DIM, HEAD_DIM), F32),
                   jax.ShapeDtypeStruct((dec_batch, pages_per_row // PAGES_PER_BLOCK, width), F32)],
        compiler_params=_params(("arbitrary",)),
        name="gdn_prompt",
    )(page_table, qkva, gb, z, cw, gnw, cache_pairs)


def _row0(x, rows=SUBLANES):
    r = lax.broadcasted_iota(jnp.int32, (rows, x.shape[1]), 0)
    return jnp.where(r == 0, jnp.broadcast_to(x, (rows, x.shape[1])), 0.0)


GDN_SAMPLE_ROWS = SUBLANES


def _gdn_sample_kernel(xn_ref, p0_ref, p1_ref, p2_ref, gb_ref, z_ref, cw_ref, gnw_ref, s0_ref,
                       o_ref, sout_ref):
    r0 = pl.multiple_of(pl.program_id(0) * GDN_SAMPLE_ROWS, GDN_SAMPLE_ROWS)
    rows = lambda r: r[pl.ds(r0, GDN_SAMPLE_ROWS), :]
    cw = cw_ref[...]
    y = rows(p0_ref) * cw[0:1]
    y = y + rows(p1_ref) * cw[1:2]
    y = y + rows(p2_ref) * cw[2:3]
    y = y + rows(xn_ref) * cw[3:4]
    y = _silu(y)
    qk = _gdn_head_norms(y)
    gbr = rows(gb_ref)
    zr = rows(z_ref)
    gnw = gnw_ref[...]
    ri = lax.broadcasted_iota(jnp.int32, (HEAD_DIM, HEAD_DIM), 0)
    ci = lax.broadcasted_iota(jnp.int32, (HEAD_DIM, HEAD_DIM), 1)
    eye = (ri == ci).astype(BF16)
    pad = 2 * SUBLANES
    group = 2
    for j0 in range(0, GDN_SAMPLE_ROWS, group):
        units = [(j, h) for j in range(j0, j0 + group) for h in range(H_A)]
        kps = [_row0(qk[H_A + h][j:j + 1], pad).astype(BF16) for j, h in units]
        ks0 = [_dot(kp, s0_ref[j, h].astype(BF16))[0:1] for kp, (j, h) in zip(kps, units)]
        k_cols = [_dot_nt(eye, kp)[:, 0:1] for kp in kps]
        s_news = []
        for (j, h), ks, k_col in zip(units, ks0, k_cols):
            lo, hi = h * HEAD_DIM, (h + 1) * HEAD_DIM
            v = y[j:j + 1, 2 * H_A * HEAD_DIM + lo:2 * H_A * HEAD_DIM + hi]
            eg = jnp.exp(gbr[j:j + 1, h:h + 1])
            beta = gbr[j:j + 1, H_A + h:H_A + h + 1]
            v_new = beta * (v - eg * ks)
            s_new = s0_ref[j, h] * eg + k_col * v_new
            sout_ref[j, h] = s_new
            s_news.append(s_new.astype(BF16))
        outs = [_dot(_row0(qk[h][j:j + 1], pad).astype(BF16), s_new)[0:1]
                for (j, h), s_new in zip(units, s_news)]
        for (j, h), o in zip(units, outs):
            lo, hi = h * HEAD_DIM, (h + 1) * HEAD_DIM
            o_ref[j, :, lo:hi] = _gdn_out_norm(o, gnw, zr[j:j + 1, lo:hi])


def _gdn_sample(xn, p0, p1, p2, gb, z, cw, gnw, s0):
    nb = xn.shape[0]
    full = lambda a: _const_spec(a.shape)
    st = pl.BlockSpec((GDN_SAMPLE_ROWS, H_A, HEAD_DIM, HEAD_DIM), lambda b: (b, 0, 0, 0))
    return pl.pallas_call(
        _gdn_sample_kernel,
        grid=(nb // GDN_SAMPLE_ROWS,),
        in_specs=[full(xn), full(p0), full(p1), full(p2), full(gb), full(z), full(cw), full(gnw), st],
        out_specs=[pl.BlockSpec((GDN_SAMPLE_ROWS, 1, Z_COLS), lambda b: (b, 0, 0)), st],
        out_shape=[jax.ShapeDtypeStruct((nb, 1, Z_COLS), F32),
                   jax.ShapeDtypeStruct(s0.shape, F32)],
        compiler_params=_params(("parallel",)),
        name="gdn_sample",
    )(xn, p0, p1, p2, gb, z, cw, gnw, s0)


def _moba_prompt_kernel(q_ref, k_ref, v_ref, o_ref, *, seq):
    nb = seq // MOBA_BLOCK
    k = k_ref[...]
    kb = k.astype(BF16)
    v_t = jnp.concatenate([v_ref[...].T, jnp.ones((SUBLANES, seq), F32)], axis=0).astype(BF16)
    km = jnp.concatenate(
        [jnp.sum(k[j * MOBA_BLOCK:(j + 1) * MOBA_BLOCK], axis=0, keepdims=True) * (1.0 / MOBA_BLOCK)
         for j in range(nb)], axis=0)
    blk = lax.broadcasted_iota(jnp.int32, (nb, MOBA_BLOCK), 0)
    ki = lax.broadcasted_iota(jnp.int32, (MOBA_BLOCK, MOBA_BLOCK), 0)
    qi = lax.broadcasted_iota(jnp.int32, (MOBA_BLOCK, MOBA_BLOCK), 1)
    causal_bias = jnp.where(ki <= qi, 0.0, NEG_INF)
    scale = HEAD_DIM ** -0.5 * LOG2_E

    def logits(n):
        q = q_ref[n * MOBA_BLOCK:(n + 1) * MOBA_BLOCK, :]
        gate = _dot_nt(km, q, HIGHEST)
        qs = (q * scale).astype(BF16)
        cnt = jnp.zeros((nb, MOBA_BLOCK), jnp.int32)
        for i in range(n):
            gi = gate[i:i + 1, :]
            beats = (gi > gate) | ((gi == gate) & (i < blk))
            cnt = cnt + beats.astype(jnp.int32)
        sel_bias = jnp.where((cnt < MOBA_TOPK) & (blk < n), 0.0, NEG_INF)
        tiles = [_dot_nt(kb[j * MOBA_BLOCK:(j + 1) * MOBA_BLOCK], qs) + sel_bias[j:j + 1, :] for j in range(n)]
        tiles.append(_dot_nt(kb[n * MOBA_BLOCK:(n + 1) * MOBA_BLOCK], qs) + causal_bias)
        return tiles

    def probabilities(tiles):
        m = functools.reduce(jnp.maximum, [jnp.max(t, axis=0, keepdims=True) for t in tiles])
        return jnp.concatenate([jnp.exp2(t - m).astype(BF16) for t in tiles], axis=0)

    def weighted_values(n, p):
        pv = _dot(v_t[:, :(n + 1) * MOBA_BLOCK], p)
        o_t = pv[:HEAD_DIM] / pv[HEAD_DIM:HEAD_DIM + 1]
        o_ref[n * MOBA_BLOCK:(n + 1) * MOBA_BLOCK, :] = o_t.T.astype(o_ref.dtype)

    ahead = 2
    pending = {n: logits(n) for n in range(min(ahead, nb))}
    for n in range(nb):
        p = probabilities(pending.pop(n))
        if n + ahead < nb:
            pending[n + ahead] = logits(n + ahead)
        weighted_values(n, p)


def _moba_prompt(qb, kb, vb, batch, seq):
    spec = pl.BlockSpec((seq, HEAD_DIM), lambda b, h: (b, h))
    return pl.pallas_call(
        functools.partial(_moba_prompt_kernel, seq=seq),
        grid=(batch, H_B),
        in_specs=[spec, spec, spec],
        out_specs=spec,
        out_shape=jax.ShapeDtypeStruct((batch * seq, H_B * HEAD_DIM), BF16),
        compiler_params=_params(("parallel", "parallel")),
        name="moba_prompt",
    )(qb, kb, vb)


SELECT_ROWS = SUBLANES


def _moba_select_kernel(q_ref, ps_ref, idx_ref):
    r0 = pl.multiple_of(pl.program_id(0) * SELECT_ROWS, SELECT_ROWS)
    qs = q_ref[pl.ds(r0, SELECT_ROWS), :]
    n_blocks = ps_ref.shape[1]
    row = lax.broadcasted_iota(jnp.int32, (SUBLANES, LANES), 0)
    lane = lax.broadcasted_iota(jnp.int32, (SUBLANES, LANES), 1)
    blk = lax.broadcasted_iota(jnp.int32, (1, n_blocks), 1).astype(F32)
    for j in range(SELECT_ROWS):
        km = ps_ref[j] * (1.0 / MOBA_BLOCK)
        out = jnp.zeros((SUBLANES, LANES), F32)
        for h in range(H_B):
            lo, hi = h * HEAD_DIM, (h + 1) * HEAD_DIM
            g = _dot_nt(_row0(qs[j:j + 1, lo:hi]), km[:, lo:hi], HIGHEST)[0:1]
            for s in range(MOBA_TOPK):
                m = jnp.max(g, axis=1, keepdims=True)
                idx = jnp.min(jnp.where(g == m, blk, float(n_blocks)), axis=1, keepdims=True)
                out = jnp.where((row == h) & (lane == s), idx, out)
                g = jnp.where(blk == idx, NEG_INF, g)
        idx_ref[j] = out.astype(jnp.int32)


def _moba_select(qb, psums):
    nb, n_blocks, width = psums.shape
    return pl.pallas_call(
        _moba_select_kernel,
        grid=(nb // SELECT_ROWS,),
        in_specs=[_const_spec(qb.shape), pl.BlockSpec((SELECT_ROWS, n_blocks, width), lambda b: (b, 0, 0))],
        out_specs=pl.BlockSpec((SELECT_ROWS, SUBLANES, LANES), lambda b: (b, 0, 0)),
        out_shape=jax.ShapeDtypeStruct((nb, SUBLANES, LANES), jnp.int32),
        compiler_params=_params(("parallel",)),
        name="moba_select",
    )(qb, psums)


N_SEL_PAGES = MOBA_TOPK * PAGES_PER_BLOCK


def _moba_attend_kernel(idx_ref, pt_ref, q_ref, k_ref, v_ref, ck_ref, cv_ref, o_ref, kbuf, vbuf, sem):
    b = pl.program_id(0)
    n_rows = pl.num_programs(0)

    def copies(row, half):
        out = []
        for h in range(H_B):
            for s in range(MOBA_TOPK):
                blk = idx_ref[(row * H_B + h) * MOBA_TOPK + s]
                for p in range(PAGES_PER_BLOCK):
                    page = pt_ref[row, blk * PAGES_PER_BLOCK + p]
                    j = (h * MOBA_TOPK + s) * PAGES_PER_BLOCK + p
                    out.append(pltpu.make_async_copy(ck_ref.at[0, page, :, h, :], kbuf.at[half, j], sem.at[half]))
                    out.append(pltpu.make_async_copy(cv_ref.at[0, page, :, h, :], vbuf.at[half, j], sem.at[half]))
        return out

    @pl.when(b == 0)
    def _():
        for i, c in enumerate(copies(0, 0)):
            c.start(priority=i % 2)

    scale = HEAD_DIM ** -0.5
    for parity in range(2):
        @pl.when(b % 2 == parity)
        def _():
            for i, c in enumerate(copies(jnp.minimum(b + 1, n_rows - 1), 1 - parity)):
                c.start(priority=i % 2)
            for c in copies(0, parity):
                c.wait()
            q = q_ref[pl.ds(b, 1), :]
            k_new = k_ref[pl.ds(b, 1), :]
            v_new = v_ref[pl.ds(b, 1), :]
            heads = range(H_B)
            span = lambda h: slice(h * HEAD_DIM, (h + 1) * HEAD_DIM)
            pages = lambda buf, h: buf[parity, h * N_SEL_PAGES:(h + 1) * N_SEL_PAGES].reshape(
                N_SEL_PAGES * PAGE_SIZE, HEAD_DIM)
            logits = [_dot_nt(_row0(q[:, span(h)], 2 * SUBLANES).astype(BF16),
                              pages(kbuf, h).astype(BF16))[0:1] * scale for h in heads]
            probs = []
            for h in heads:
                s = logits[h]
                s_new = jnp.sum(q[:, span(h)] * k_new[:, span(h)], axis=1, keepdims=True) * scale
                m = jnp.maximum(jnp.max(s, axis=1, keepdims=True), s_new)
                p = jnp.exp(s - m)
                p_new = jnp.exp(s_new - m)
                probs.append((p, p_new, jnp.sum(p, axis=1, keepdims=True) + p_new))
            pvs = [_dot(_row0(probs[h][0], 2 * SUBLANES).astype(BF16), pages(vbuf, h).astype(BF16))[0:1]
                   for h in heads]
            for h in heads:
                _, p_new, l = probs[h]
                o_ref[0, :, span(h)] = (pvs[h] + p_new * v_new[:, span(h)]) / l

    @pl.when(b == n_rows - 1)
    def _():
        for c in copies(0, n_rows % 2):
            c.wait()


def _moba_attend(idx_flat, page_table, cache_k, cache_v, qb, kb, vb):
    nb = qb.shape[0]
    n_slices = H_B * N_SEL_PAGES
    rows = pl.BlockSpec((nb, H_B * HEAD_DIM), lambda b, idx, pt: (0, 0))
    hbm = pl.BlockSpec(memory_space=pl.ANY)
    out = pl.pallas_call(
        _moba_attend_kernel,
        grid_spec=pltpu.PrefetchScalarGridSpec(
            num_scalar_prefetch=2, grid=(nb,),
            in_specs=[rows, rows, rows, hbm, hbm],
            out_specs=pl.BlockSpec((1, 1, H_B * HEAD_DIM), lambda b, idx, pt: (b, 0, 0)),
            scratch_shapes=[pltpu.VMEM((2, n_slices, PAGE_SIZE, HEAD_DIM), F32),
                            pltpu.VMEM((2, n_slices, PAGE_SIZE, HEAD_DIM), F32),
                            pltpu.SemaphoreType.DMA((2,))]),
        out_shape=jax.ShapeDtypeStruct((nb, 1, H_B * HEAD_DIM), F32),
        compiler_params=_params(("arbitrary",)),
        name="moba_attend",
    )(idx_flat, page_table, qb, kb, vb, cache_k, cache_v)
    return out.reshape(nb, H_B * HEAD_DIM)


def _out_ffn_kernel(x_ref, oa_ref, ob_ref, woa_ref, wob_ref, n2_ref, wg_ref, wu_ref, wd_ref, fn_ref, y_ref):
    tm = x_ref.shape[0]
    n_groups = 2 if tm % (2 * SUBLANES) == 0 and tm >= 256 else 1
    rows = [slice(g * tm // n_groups, (g + 1) * tm // n_groups) for g in range(n_groups)]
    x1 = [x_ref[r, :] + _dot(oa_ref[r, :].astype(BF16), woa_ref[...])
          + _dot(ob_ref[r, :].astype(BF16), wob_ref[...]) for r in rows]
    h = [_rmsnorm(t, n2_ref[...]).astype(BF16) for t in x1]
    gate = [_dot(t, wg_ref[...]) for t in h]
    up = [_dot(t, wu_ref[...]) for t in h]
    act = [(_silu(g) * u).astype(BF16) for g, u in zip(gate, up)]
    down = [_dot(t, wd_ref[...]) for t in act]
    for r, a, d in zip(rows, x1, down):
        y_ref[r, :] = _rmsnorm(a + d, fn_ref[...])


def _out_ffn(x, oa, ob, woa, wob, n2, wg, wu, wd, fn, tm):
    t = x.shape[0]
    row = lambda i: (i, 0)
    full = lambda a: _const_spec(a.shape)
    return pl.pallas_call(
        _out_ffn_kernel,
        grid=(t // tm,),
        in_specs=[pl.BlockSpec((tm, D_MODEL), row), pl.BlockSpec((tm, Z_COLS), row),
                  pl.BlockSpec((tm, H_B * HEAD_DIM), row), full(woa), full(wob), full(n2),
                  full(wg), full(wu), full(wd), full(fn)],
        out_specs=pl.BlockSpec((tm, D_MODEL), row),
        out_shape=jax.ShapeDtypeStruct((t, D_MODEL), F32),
        compiler_params=_params(("parallel",)),
        name="out_ffn",
    )(x, oa, ob, woa, wob, n2, wg, wu, wd, fn)


def _rotary_tables(pos):
    pos = np.asarray(pos, np.float64)
    half = ROT_DIM // 2
    inv_freq = ROPE_THETA ** (-np.arange(half, dtype=np.float64) / half)
    ang = pos[:, None] * inv_freq[None, :]
    cos, sin = np.cos(ang).astype(np.float32), np.sin(ang).astype(np.float32)
    n = pos.shape[0]
    rest = HEAD_DIM - ROT_DIM
    cos_t = np.concatenate([cos, cos, np.ones((n, rest), np.float32)], axis=1)
    zeros_h = np.zeros((n, half), np.float32)
    sin_a = np.concatenate([-sin, zeros_h, np.zeros((n, rest), np.float32)], axis=1)
    sin_b = np.concatenate([zeros_h, sin, np.zeros((n, rest), np.float32)], axis=1)
    return jnp.asarray(cos_t), jnp.asarray(sin_a), jnp.asarray(sin_b)


def kernel(x_prompt, x_sample, cache_k, cache_v, page_table, state_gdn, state_conv, norm1_w, w_in, conv_w,
           a_log, dt_bias, gdn_norm_w, w_out, norm2_w, w_gate, w_up, w_down, final_norm_w):
    depth = w_in.shape[0]
    assert depth == 1, "single-layer trunk"
    batch, seq, _ = x_prompt.shape
    dec_batch, dec_seq, _ = x_sample.shape
    assert dec_seq == 1
    past_len = page_table.shape[1] * PAGE_SIZE

    wi = w_in[0]
    ab0 = GDN_QKV + Z_COLS
    w_a = wi[:, :ab0].astype(BF16)
    w_b = wi[:, ab0 + 2 * H_A:].astype(BF16)
    w_ab = jnp.pad(wi[:, ab0:ab0 + 2 * H_A], ((0, 0), (0, AB_PAD - 2 * H_A))).astype(BF16)
    gpar = jnp.zeros((SUBLANES, LANES), F32)
    gpar = gpar.at[0, :H_A].set(a_log[0].astype(F32)).at[1, :H_A].set(dt_bias[0].astype(F32))
    nw1 = norm1_w[0].reshape(1, D_MODEL)
    cw = conv_w[0]
    gnw = gdn_norm_w[0].reshape(1, HEAD_DIM)
    woa = w_out[0][:Z_COLS].astype(BF16)
    wob = w_out[0][Z_COLS:].astype(BF16)
    n2 = norm2_w[0].reshape(1, D_MODEL)
    wg, wu, wd = w_gate[0].astype(BF16), w_up[0].astype(BF16), w_down[0].astype(BF16)
    fn = final_norm_w.reshape(1, D_MODEL)

    xp = x_prompt.reshape(batch * seq, D_MODEL)
    tabs_p = _rotary_tables(np.arange(seq))
    qkva_p, z_p, gb_p, qb_p, kb_p, vb_p, k5_p, v5_p = _in_proj(xp, nw1, w_a, w_b, w_ab, gpar, *tabs_p,
                                                               batch=batch, seq=seq, tm=512)
    oa_p, gdn_p, psums = _gdn_prompt(qkva_p, gb_p, z_p, cw, gnw, cache_k, page_table, batch, seq, tl=256)
    ob_p = _moba_prompt(qb_p, kb_p, vb_p, batch, seq)
    y_p = _out_ffn(xp, oa_p, ob_p, woa, wob, n2, wg, wu, wd, fn, tm=512)

    xs = x_sample.reshape(dec_batch, D_MODEL)
    tabs_s = _rotary_tables(np.full((dec_batch,), past_len))
    qkva_s, z_s, gb_s, qb_s, kb_s, vb_s, k5_s, v5_s = _in_proj(xs, nw1, w_a, w_b, w_ab, gpar, *tabs_s,
                                                               batch=1, seq=dec_batch, tm=dec_batch)
    qkva_s, z_s = qkva_s.astype(F32), z_s.astype(F32)
    sc = state_conv[0]
    oa_s, gdn_s = _gdn_sample(qkva_s, sc[:, 0], sc[:, 1], sc[:, 2], gb_s, z_s, cw, gnw, state_gdn[0])
    idx = _moba_select(qb_s, psums)
    idx_flat = idx[:, :H_B, :MOBA_TOPK].reshape(-1)
    ob_s = _moba_attend(idx_flat, page_table, cache_k, cache_v, qb_s, kb_s, vb_s)
    y_s = _out_ffn(xs, oa_s.reshape(dec_batch, Z_COLS), ob_s, woa, wob, n2, wg, wu, wd, fn, tm=dec_batch)

    conv_p = qkva_p.reshape(batch, seq, GDN_QKV)[:, seq - (CONV_W - 1):].astype(F32)
    conv_s = jnp.concatenate([sc[:, 1:], qkva_s[:, None, :]], axis=1)
    return (y_p.reshape(batch, seq, D_MODEL), y_s.reshape(dec_batch, 1, D_MODEL),
            k5_p, v5_p,
            k5_s.reshape(1, dec_batch, 1, H_B, HEAD_DIM), v5_s.reshape(1, dec_batch, 1, H_B, HEAD_DIM),
            gdn_p[None], gdn_s[None], conv_p[None], conv_s[None])
```
